```python
import jax, jax.numpy as jnp
from jax import lax
import numpy as np

D_MODEL = 1024
BATCH = 16
SEQ = 2048
DEPTH = 2

CTX_LEN = 256
GRID_W = 64
N_MIXERS = 2
HEAD_DIM = 64
ATT_Q_HEADS = D_MODEL // HEAD_DIM
ATT_KV_HEADS = ATT_Q_HEADS // 4
ATT_GROUP = ATT_Q_HEADS // ATT_KV_HEADS
ATT_Q_DIM = ATT_Q_HEADS * HEAD_DIM
ATT_KV_DIM = ATT_KV_HEADS * HEAD_DIM
ATT_IN_DIM = ATT_Q_DIM + 2 * ATT_KV_DIM
WINDOW = 128
ATT_BLOCK = 128
ROPE_BASE = 10000.0
ROPE_AXIS_PAIRS = HEAD_DIM // 4
GLA_HEADS = 4
GLA_DK = D_MODEL // 2 // GLA_HEADS
GLA_DV = D_MODEL // GLA_HEADS
GLA_QK = GLA_HEADS * GLA_DK
GLA_V = GLA_HEADS * GLA_DV
GLA_QR = GLA_QK + GLA_V
GLA_GATE_RANK = 16
GLA_TAU = 16.0
GLA_CHUNK = 64
GLA_IN_DIM = GLA_QR + GLA_QK + GLA_V + 2 * GLA_GATE_RANK
N_EXPERTS = 16
CAPACITY_FACTOR = 2
EXPERT_FF = 2816
NORM_EPS = 1e-6

kernel_name = "hybrid_swa_gla_ec_moe_dit"

F32 = jnp.float32


def rms_norm(x, g):
    xf = x.astype(F32)
    y = xf * lax.rsqrt(jnp.mean(xf * xf, axis=-1, keepdims=True) + NORM_EPS)
    return (y * g.astype(F32)).astype(x.dtype)


def modulate(h, shift, scale):
    return h * (1 + scale) + shift


def axial_rope_tables(n):
    rows = n // GRID_W
    row = jnp.repeat(jnp.arange(rows, dtype=F32), GRID_W)
    col = jnp.tile(jnp.arange(GRID_W, dtype=F32), rows)
    inv = ROPE_BASE ** (-jnp.arange(ROPE_AXIS_PAIRS, dtype=F32) / ROPE_AXIS_PAIRS)
    ar = row[:, None] * inv
    ac = col[:, None] * inv
    ang = jnp.concatenate([ar, ar, ac, ac], axis=-1)
    return jnp.cos(ang), jnp.sin(ang)


def apply_rope(x, cos, sin):
    bshape = (cos.shape[0],) + (1,) * (x.ndim - 3) + (HEAD_DIM,)
    xf = x.astype(F32)
    xs = xf.reshape(xf.shape[:-1] + (2, 2, ROPE_AXIS_PAIRS))
    rot = jnp.stack([-xs[..., 1, :], xs[..., 0, :]], axis=-2).reshape(xf.shape)
    return (xf * cos.reshape(bshape) + rot * sin.reshape(bshape)).astype(x.dtype)


def window_attention(h, hc, w_in, w_out, sink, cos, sin, ctx_out):
    bsz, n, _ = h.shape
    n_ctx = hc.shape[1]
    scale = HEAD_DIM ** -0.5
    p = h @ w_in
    q = p[..., :ATT_Q_DIM].reshape(bsz, n, ATT_KV_HEADS, ATT_GROUP, HEAD_DIM)
    k = p[..., ATT_Q_DIM:ATT_Q_DIM + ATT_KV_DIM].reshape(bsz, n, ATT_KV_HEADS, HEAD_DIM)
    v = p[..., ATT_Q_DIM + ATT_KV_DIM:].reshape(bsz, n, ATT_KV_HEADS, HEAD_DIM)
    q = apply_rope(q, cos, sin)
    k = apply_rope(k, cos, sin)
    pkv = hc @ w_in[:, ATT_Q_DIM:]
    kc = pkv[..., :ATT_KV_DIM].reshape(bsz, n_ctx, ATT_KV_HEADS, HEAD_DIM).astype(F32)
    vc = pkv[..., ATT_KV_DIM:].reshape(bsz, n_ctx, ATT_KV_HEADS, HEAD_DIM).astype(F32)
    sink_f = sink.astype(F32).reshape(ATT_KV_HEADS, ATT_GROUP)

    nb = n // ATT_BLOCK
    qb = q.astype(F32).reshape(bsz, nb, ATT_BLOCK, ATT_KV_HEADS, ATT_GROUP, HEAD_DIM) * scale

    def band(z):
        zp = jnp.pad(z, ((0, 0), (ATT_BLOCK, ATT_BLOCK), (0, 0), (0, 0)))
        zp = zp.reshape(bsz, nb + 2, ATT_BLOCK, ATT_KV_HEADS, HEAD_DIM)
        return jnp.concatenate([zp[:, :-2], zp[:, 1:-1], zp[:, 2:]], axis=2).astype(F32)

    kb, vb = band(k), band(v)
    qpos = jnp.arange(nb)[:, None] * ATT_BLOCK + jnp.arange(ATT_BLOCK)[None, :]
    kpos = jnp.arange(nb)[:, None] * ATT_BLOCK - ATT_BLOCK + jnp.arange(3 * ATT_BLOCK)[None, :]
    valid = ((jnp.abs(qpos[:, :, None] - kpos[:, None, :]) <= WINDOW)
             & (kpos[:, None, :] >= 0) & (kpos[:, None, :] < n))
    s_b = jnp.einsum('bnqhgd,bnkhd->bnhgqk', qb, kb)
    s_b = jnp.where(valid[None, :, None, None], s_b, -jnp.inf)
    s_c = jnp.einsum('bnqhgd,bchd->bnhgqc', qb, kc)
    sink_b = sink_f[None, None, :, :, None]
    m = jnp.maximum(jnp.maximum(s_b.max(-1), s_c.max(-1)), sink_b)
    e_b = jnp.exp(s_b - m[..., None])
    e_c = jnp.exp(s_c - m[..., None])
    den = e_b.sum(-1) + e_c.sum(-1) + jnp.exp(sink_b - m)
    o = (jnp.einsum('bnhgqk,bnkhd->bnqhgd', e_b, vb)
         + jnp.einsum('bnhgqc,bchd->bnqhgd', e_c, vc))
    o = o / jnp.transpose(den, (0, 1, 4, 2, 3))[..., None]
    y = o.reshape(bsz, n, ATT_Q_DIM).astype(h.dtype) @ w_out
    if not ctx_out:
        return y, None
    qc = (hc @ w_in[:, :ATT_Q_DIM]).astype(F32).reshape(bsz, n_ctx, ATT_KV_HEADS, ATT_GROUP, HEAD_DIM) * scale
    s = jnp.einsum('bqhgd,bkhd->bhgqk', qc, kc)
    sink_c = jnp.broadcast_to(sink_f[None, :, :, None, None], s.shape[:-1] + (1,))
    pr = jax.nn.softmax(jnp.concatenate([sink_c, s], axis=-1), axis=-1)[..., 1:]
    oc = jnp.einsum('bhgqk,bkhd->bqhgd', pr, vc).reshape(bsz, n_ctx, ATT_Q_DIM)
    yc = oc.astype(hc.dtype) @ w_out
    return y, yc


def gla_kv_gates(p, gate_w2, gate_b):
    bsz, L, _ = p.shape
    k = p[..., :GLA_QK].reshape(bsz, L, GLA_HEADS, GLA_DK)
    v = p[..., GLA_QK:GLA_QK + GLA_V].reshape(bsz, L, GLA_HEADS, GLA_DV)
    lr = p[..., GLA_QK + GLA_V:].reshape(bsz, L, 2, GLA_GATE_RANK)
    z = jnp.einsum('blzr,zrk->blzk', lr, gate_w2) + gate_b
    lg = (jax.nn.log_sigmoid(z.astype(F32)) / GLA_TAU).reshape(bsz, L, 2, GLA_HEADS, GLA_DK)
    return k, v, lg[:, :, 0], lg[:, :, 1]


def gla_chunked(q, k, v, lg, s0):
    bsz, L, H, _ = k.shape
    dv = v.shape[-1]
    nc = L // GLA_CHUNK

    def blocks(a):
        return a.astype(F32).reshape(bsz, nc, GLA_CHUNK, H, a.shape[-1])

    q, k, v, lg = blocks(q), blocks(k), blocks(v), blocks(lg)
    b = jnp.cumsum(lg, axis=2)
    b_last = b[:, :, -1:]
    qt = q * jnp.exp(b)
    kt = k * jnp.exp(-b)
    tril = jnp.tril(jnp.ones((GLA_CHUNK, GLA_CHUNK), dtype=bool))
    a = jnp.where(tril, jnp.einsum('bnihd,bnjhd->bnhij', qt, kt), 0.0)
    o = jnp.einsum('bnhij,bnjhe->bnihe', a, v)
    ds = jnp.einsum('bnjhd,bnjhe->bnhde', k * jnp.exp(b_last - b), v)
    decay = jnp.exp(b_last[:, :, 0])

    def step(s, inp):
        dec, d = inp
        return dec[..., None] * s + d, s

    s_fin, s_start = lax.scan(step, s0, (jnp.moveaxis(decay, 1, 0), jnp.moveaxis(ds, 1, 0)))
    o = o + jnp.einsum('bnihd,nbhde->bnihe', qt, s_start)
    return o.reshape(bsz, L, H, dv), s_fin


def gla_final_state(k, v, lg):
    cum = jnp.cumsum(lg, axis=1)
    w = jnp.exp(cum[:, -1:] - cum)
    return jnp.einsum('blhd,blhe->bhde', k.astype(F32) * w, v.astype(F32))


def gla_mixer(h, hc, w_in, gate_w2, gate_b, norm_g, w_out, ctx_out):
    bsz, n, _ = h.shape
    n_ctx = hc.shape[1]

    def flip(a):
        return jnp.flip(a, axis=1)

    def readout(o, r):
        o = o * lax.rsqrt(jnp.mean(o * o, axis=-1, keepdims=True) + NORM_EPS)
        o = o * norm_g.astype(F32).reshape(GLA_HEADS, GLA_DV)
        o = o.reshape(o.shape[0], o.shape[1], GLA_V).astype(h.dtype) * jax.nn.silu(r)
        return o @ w_out

    p = h @ w_in
    q = p[..., :GLA_QK].reshape(bsz, n, GLA_HEADS, GLA_DK) * GLA_DK ** -0.5
    r = p[..., GLA_QK:GLA_QR]
    k, v, lg_f, lg_b = gla_kv_gates(p[..., GLA_QR:], gate_w2, gate_b)
    kc, vc, lgc_f, lgc_b = gla_kv_gates(hc @ w_in[:, GLA_QR:], gate_w2, gate_b)
    if ctx_out:
        pq = hc @ w_in[:, :GLA_QR]
        qc = pq[..., :GLA_QK].reshape(bsz, n_ctx, GLA_HEADS, GLA_DK) * GLA_DK ** -0.5
        rc = pq[..., GLA_QK:]
        s0 = jnp.zeros((bsz, GLA_HEADS, GLA_DK, GLA_DV), F32)
        oc_f, sc_f = gla_chunked(qc, kc, vc, lgc_f, s0)
        oc_b, sc_b = gla_chunked(flip(qc), flip(kc), flip(vc), flip(lgc_b), s0)
        yc = readout(oc_f + flip(oc_b), rc)
    else:
        sc_f = gla_final_state(kc, vc, lgc_f)
        sc_b = gla_final_state(flip(kc), flip(vc), flip(lgc_b))
        yc = None
    o_f, _ = gla_chunked(q, k, v, lg_f, sc_f)
    o_b, _ = gla_chunked(flip(q), flip(k), flip(v), flip(lg_b), sc_b)
    y = readout(o_f + flip(o_b), r)
    return y, yc


def ec_moe(h, w_router, w_gate, w_up, w_down):
    bsz, L, d = h.shape
    cap = CAPACITY_FACTOR * L // N_EXPERTS
    aff = jax.nn.softmax((h @ w_router).astype(F32), axis=-1)
    g, idx = lax.top_k(jnp.swapaxes(aff, 1, 2), cap)
    xs = jax.vmap(lambda hb, ib: hb[ib])(h, idx)
    a = jnp.einsum('becd,edf->becf', xs, w_gate)
    u = jnp.einsum('becd,edf->becf', xs, w_up)
    y = jnp.einsum('becf,efd->becd', jax.nn.silu(a) * u, w_down) * g[..., None].astype(h.dtype)
    return jax.vmap(lambda yb, ib: jnp.zeros((L, d), yb.dtype).at[ib.reshape(-1)].add(yb.reshape(-1, d)))(y, idx)


def setup_inputs(seed: int = 0) -> dict:
    key = jax.random.key(seed)
    ks = jax.random.split(key, 24)
    n_attn = (DEPTH + N_MIXERS - 1) // N_MIXERS
    n_gla = DEPTH // N_MIXERS

    def nrm(k, shape, scale):
        return jax.random.normal(k, shape, F32) * scale

    return {
        "x": nrm(ks[0], (BATCH, SEQ, D_MODEL), 1.0),
        "c": nrm(ks[1], (BATCH, D_MODEL), 1.0),
        "ctx": nrm(ks[2], (BATCH, CTX_LEN, D_MODEL), 1.0),
        "c_ctx": nrm(ks[3], (D_MODEL,), 1.0),
        "ada_w": nrm(ks[4], (DEPTH, D_MODEL, 6 * D_MODEL), 0.5 * D_MODEL ** -0.5),
        "ada_b": nrm(ks[5], (DEPTH, 6 * D_MODEL), 0.02),
        "norm1_g": 1.0 + nrm(ks[6], (DEPTH, D_MODEL), 0.02),
        "norm2_g": 1.0 + nrm(ks[7], (DEPTH, D_MODEL), 0.02),
        "attn_w_in": nrm(ks[8], (n_attn, D_MODEL, ATT_IN_DIM), D_MODEL ** -0.5),
        "attn_w_out": nrm(ks[9], (n_attn, ATT_Q_DIM, D_MODEL), ATT_Q_DIM ** -0.5),
        "attn_sink": nrm(ks[10], (n_attn, ATT_Q_HEADS), 0.5),
        "gla_w_in": nrm(ks[11], (n_gla, D_MODEL, GLA_IN_DIM), D_MODEL ** -0.5),
        "gla_gate_w2": nrm(ks[12], (n_gla, 2, GLA_GATE_RANK, GLA_QK), GLA_GATE_RANK ** -0.5),
        "gla_gate_b": nrm(ks[13], (n_gla, 2, GLA_QK), 0.1),
        "gla_norm_g": 1.0 + nrm(ks[14], (n_gla, GLA_V), 0.02),
        "gla_w_out": nrm(ks[15], (n_gla, GLA_V, D_MODEL), GLA_V ** -0.5),
        "router_w": nrm(ks[16], (DEPTH, D_MODEL, N_EXPERTS), D_MODEL ** -0.5),
        "exp_w_gate": nrm(ks[17], (DEPTH, N_EXPERTS, D_MODEL, EXPERT_FF), D_MODEL ** -0.5),
        "exp_w_up": nrm(ks[18], (DEPTH, N_EXPERTS, D_MODEL, EXPERT_FF), D_MODEL ** -0.5),
        "exp_w_down": nrm(ks[19], (DEPTH, N_EXPERTS, EXPERT_FF, D_MODEL), EXPERT_FF ** -0.5),
        "final_norm_g": 1.0 + nrm(ks[20], (D_MODEL,), 0.02),
    }


def reference(x, c, ctx, c_ctx, ada_w, ada_b, norm1_g, norm2_g, attn_w_in, attn_w_out, attn_sink,
              gla_w_in, gla_gate_w2, gla_gate_b, gla_norm_g, gla_w_out, router_w, exp_w_gate,
              exp_w_up, exp_w_down, final_norm_g):
    n = x.shape[1]
    cos, sin = axial_rope_tables(n)
    xc = ctx
    src = jax.nn.silu(c)
    src_c = jax.nn.silu(c_ctx)
    for i in range(DEPTH):
        last = i == DEPTH - 1
        j = i // N_MIXERS
        mod = src @ ada_w[i] + ada_b[i]
        mod_c = src_c @ ada_w[i] + ada_b[i]
        sh1, sc1, g1, sh2, sc2, g2 = jnp.split(mod[:, None, :], 6, axis=-1)
        sh1c, sc1c, g1c, sh2c, sc2c, g2c = jnp.split(mod_c, 6, axis=-1)
        h = modulate(rms_norm(x, norm1_g[i]), sh1, sc1)
        hc = modulate(rms_norm(xc, norm1_g[i]), sh1c, sc1c)
        if i % N_MIXERS == 0:
            y, yc = window_attention(h, hc, attn_w_in[j], attn_w_out[j], attn_sink[j], cos, sin, not last)
        else:
            y, yc = gla_mixer(h, hc, gla_w_in[j], gla_gate_w2[j], gla_gate_b[j], gla_norm_g[j],
                              gla_w_out[j], not last)
        x = x + g1 * y
        h = modulate(rms_norm(x, norm2_g[i]), sh2, sc2)
        x = x + g2 * ec_moe(h, router_w[i], exp_w_gate[i], exp_w_up[i], exp_w_down[i])
        if not last:
            xc = xc + g1c * yc
            hc = modulate(rms_norm(xc, norm2_g[i]), sh2c, sc2c)
            xc = xc + g2c * ec_moe(hc, router_w[i], exp_w_gate[i], exp_w_up[i], exp_w_down[i])
    return rms_norm(x, final_norm_g)
```

```python
import functools

import jax
import jax.numpy as jnp
from jax import lax
from jax.experimental import pallas as pl
from jax.experimental.pallas import tpu as pltpu

F32 = jnp.float32
BF16 = jnp.bfloat16
I32 = jnp.int32

LANES = 128
HEAD_DIM = 64
ATT_GROUP = 4
ATT_BLOCK = 128
GRID_W = 64
ROPE_BASE = 10000.0
ROPE_PAIRS = HEAD_DIM // 4
GLA_HEADS = 4
GLA_RANK = 16
GLA_TAU = 16.0
GLA_CHUNK = 64
N_EXPERTS = 16
CAPACITY_FACTOR = 2
NORM_EPS = 1e-6
ROW_TILE = 256
MOE_ROW_TILE = 2048
MOE_DOWN_ROW_TILE = 512
FF_TILE = 256
GLA_TILE = 256
VMEM_LIMIT = 56 * 1024 * 1024

NT_DIMS = (((1,), (1,)), ((), ()))
TN_DIMS = (((0,), (0,)), ((), ()))


def _params(*sem):
    return pltpu.CompilerParams(dimension_semantics=sem, vmem_limit_bytes=VMEM_LIMIT)


def _dot(a, b):
    return jnp.dot(a, b, preferred_element_type=F32)


def _split2(a):
    hi = a.astype(BF16)
    lo = (a - hi.astype(F32)).astype(BF16)
    return hi, lo


def _dot3(a, b_hi, b_lo):
    a_hi, a_lo = _split2(a)
    return _dot(a_hi, b_hi) + _dot(a_hi, b_lo) + _dot(a_lo, b_hi)


def _silu(a):
    return a * jax.nn.sigmoid(a)


def _norm_mod(x, g, shift, scale):
    y = x * lax.rsqrt(jnp.mean(x * x, axis=-1, keepdims=True) + NORM_EPS)
    return (y * g) * (1.0 + scale) + shift


def _mod_kernel(src_ref, w_ref, b_ref, o_ref):
    w_hi, w_lo = _split2(w_ref[0])
    o_ref[...] = _dot3(_silu(src_ref[...]), w_hi, w_lo) + b_ref[0]


def _modulation(src, ada_w, ada_b, layer):
    rows, d = src.shape
    depth, _, n = ada_w.shape
    tn = 512
    return pl.pallas_call(
        _mod_kernel,
        grid=(n // tn,),
        in_specs=[pl.BlockSpec((rows, d), lambda j: (0, 0)),
                  pl.BlockSpec((1, d, tn), lambda j: (layer, 0, j)),
                  pl.BlockSpec((1, 1, tn), lambda j: (layer, 0, j))],
        out_specs=pl.BlockSpec((rows, tn), lambda j: (0, j)),
        out_shape=jax.ShapeDtypeStruct((rows, n), F32),
        compiler_params=_params("arbitrary"),
        name="adaln_mod",
    )(src, ada_w, ada_b.reshape(depth, 1, n))


def _rope(x, cos, sin_signed, lane_lo):
    outs = []
    for j in range(x.shape[1] // LANES):
        xb = x[:, j * LANES:(j + 1) * LANES]
        partner = jnp.where(lane_lo, pltpu.roll(xb, LANES - ROPE_PAIRS, 1), pltpu.roll(xb, ROPE_PAIRS, 1))
        outs.append(xb * cos + partner * sin_signed)
    return jnp.concatenate(outs, axis=1)


def _pre_attn_kernel(x_ref, g_ref, sh_ref, sc_ref, w_ref, cos_ref, sin_ref, q_ref, k_ref, v_ref, *, rope):
    d = x_ref.shape[-1]
    kv = k_ref.shape[-1]
    h = _norm_mod(x_ref[0], g_ref[...], sh_ref[0], sc_ref[0]).astype(BF16)
    p = _dot(h, w_ref[...])
    q, k, v = p[:, :d], p[:, d:d + kv], p[:, d + kv:]
    if rope:
        lane = lax.broadcasted_iota(I32, (x_ref.shape[1], LANES), 1)
        lane_lo = (lane & (2 * ROPE_PAIRS - 1)) < ROPE_PAIRS
        cos, sin_signed = cos_ref[...], sin_ref[...]
        q = _rope(q, cos, sin_signed, lane_lo)
        k = _rope(k, cos, sin_signed, lane_lo)
    q_ref[0] = (q * HEAD_DIM ** -0.5).astype(BF16)
    k_ref[0] = k.astype(BF16)
    v_ref[0] = v.astype(BF16)


def _pre_attn(x, g, shift, scale, w_in, cos, sin_signed, rope):
    b, t, d = x.shape
    n_in = w_in.shape[1]
    kv = (n_in - d) // 2
    tm = min(ROW_TILE, t)
    row = lambda i, j: (i, j, 0)
    per_b = lambda i, j: (i, 0, 0)
    fixed = lambda i, j: (0, 0)
    return pl.pallas_call(
        functools.partial(_pre_attn_kernel, rope=rope),
        grid=(b, t // tm),
        in_specs=[pl.BlockSpec((1, tm, d), row),
                  pl.BlockSpec((1, d), fixed),
                  pl.BlockSpec((1, 1, d), per_b),
                  pl.BlockSpec((1, 1, d), per_b),
                  pl.BlockSpec((d, n_in), fixed),
                  pl.BlockSpec((tm, LANES), lambda i, j: (j, 0)),
                  pl.BlockSpec((tm, LANES), lambda i, j: (j, 0))],
        out_specs=[pl.BlockSpec((1, tm, d), row),
                   pl.BlockSpec((1, tm, kv), row),
                   pl.BlockSpec((1, tm, kv), row)],
        out_shape=[jax.ShapeDtypeStruct((b, t, d), BF16),
                   jax.ShapeDtypeStruct((b, t, kv), BF16),
                   jax.ShapeDtypeStruct((b, t, kv), BF16)],
        compiler_params=_params("arbitrary", "arbitrary"),
        name="attn_in_proj",
    )(x, g, shift, scale, w_in, cos, sin_signed)


def _attn_heads(q, kcat, vcat, valid, sink_ref, o_ref):
    tq = q.shape[0]
    n_kv = kcat.shape[1] // HEAD_DIM
    if valid is not None:
        valid = jnp.concatenate([valid] * ATT_GROUP, axis=0)
    outs = []
    for h in range(n_kv):
        kh = kcat[:, h * HEAD_DIM:(h + 1) * HEAD_DIM]
        vh = vcat[:, h * HEAD_DIM:(h + 1) * HEAD_DIM]
        heads = [ATT_GROUP * h + g for g in range(ATT_GROUP)]
        qg = jnp.concatenate([q[:, i * HEAD_DIM:(i + 1) * HEAD_DIM] for i in heads], axis=0)
        sk = jnp.concatenate([jnp.full((tq, 1), sink_ref[0, i], F32) for i in heads], axis=0)
        s = lax.dot_general(qg, kh, NT_DIMS, preferred_element_type=F32)
        if valid is not None:
            s = jnp.where(valid, s, -jnp.inf)
        m = jnp.maximum(jnp.max(s, axis=-1, keepdims=True), sk)
        e = jnp.exp(s - m)
        den = jnp.sum(e, axis=-1, keepdims=True) + jnp.exp(sk - m)
        o = _dot(e.astype(BF16), vh) / den
        outs.extend(o[g * tq:(g + 1) * tq] for g in range(ATT_GROUP))
    o_ref[0] = jnp.concatenate(outs, axis=1).astype(o_ref.dtype)


def _attn_band_kernel(q_ref, kp_ref, ko_ref, kn_ref, vp_ref, vo_ref, vn_ref, kc_ref, vc_ref, sink_ref, o_ref):
    j = pl.program_id(1)
    nb = pl.num_programs(1)
    n_ctx = kc_ref.shape[1]
    blk = ATT_BLOCK
    kcat = jnp.concatenate([kc_ref[0], kp_ref[0], ko_ref[0], kn_ref[0]], axis=0)
    vcat = jnp.concatenate([vc_ref[0], vp_ref[0], vo_ref[0], vn_ref[0]], axis=0)
    nk = n_ctx + 3 * blk
    row = lax.broadcasted_iota(I32, (blk, nk), 0)
    col = lax.broadcasted_iota(I32, (blk, nk), 1) - n_ctx
    in_window = (col >= row) & (col <= row + 2 * blk)
    in_seq = ((j > 0) | (col >= blk)) & ((j < nb - 1) | (col < 2 * blk))
    valid = (col < 0) | (in_window & in_seq)
    _attn_heads(q_ref[0], kcat, vcat, valid, sink_ref, o_ref)


def _attn_ctx_kernel(q_ref, kc_ref, vc_ref, sink_ref, o_ref):
    _attn_heads(q_ref[0], kc_ref[0], vc_ref[0], None, sink_ref, o_ref)


def _attn_band(q, k, v, kc, vc, sink):
    b, t, d = q.shape
    kv = k.shape[-1]
    n_ctx = kc.shape[1]
    nb = t // ATT_BLOCK
    blk = lambda m: pl.BlockSpec((1, ATT_BLOCK, kv), m)
    prev = lambda i, j: (i, jnp.maximum(j - 1, 0), 0)
    own = lambda i, j: (i, j, 0)
    nxt = lambda i, j: (i, jnp.minimum(j + 1, nb - 1), 0)
    ctx_spec = pl.BlockSpec((1, n_ctx, kv), lambda i, j: (i, 0, 0))
    return pl.pallas_call(
        _attn_band_kernel,
        grid=(b, nb),
        in_specs=[pl.BlockSpec((1, ATT_BLOCK, d), own),
                  blk(prev), blk(own), blk(nxt), blk(prev), blk(own), blk(nxt),
                  ctx_spec, ctx_spec,
                  pl.BlockSpec(memory_space=pltpu.SMEM)],
        out_specs=pl.BlockSpec((1, ATT_BLOCK, d), own),
        out_shape=jax.ShapeDtypeStruct((b, t, d), BF16),
        compiler_params=_params("arbitrary", "arbitrary"),
        name="attn_band",
    )(q, k, k, k, v, v, v, kc, vc, sink)


def _attn_ctx(qc, kc, vc, sink):
    b, n_ctx, d = qc.shape
    kv = kc.shape[-1]
    return pl.pallas_call(
        _attn_ctx_kernel,
        grid=(b,),
        in_specs=[pl.BlockSpec((1, n_ctx, d), lambda i: (i, 0, 0)),
                  pl.BlockSpec((1, n_ctx, kv), lambda i: (i, 0, 0)),
                  pl.BlockSpec((1, n_ctx, kv), lambda i: (i, 0, 0)),
                  pl.BlockSpec(memory_space=pltpu.SMEM)],
        out_specs=pl.BlockSpec((1, n_ctx, d), lambda i: (i, 0, 0)),
        out_shape=jax.ShapeDtypeStruct((b, n_ctx, d), BF16),
        compiler_params=_params("arbitrary"),
        name="attn_ctx",
    )(qc, kc, vc, sink)


def _post_kernel(o_ref, w_ref, x_ref, g1_ref, n2_ref, sh_ref, sc_ref, wr_hi_ref, wr_lo_ref,
                 x1_ref, h2_ref, lg_ref):
    y = _dot(o_ref[0], w_ref[...])
    x1 = x_ref[0] + g1_ref[0] * y
    x1_ref[0] = x1
    h2 = _norm_mod(x1, n2_ref[...], sh_ref[0], sc_ref[0])
    h2_ref[0] = h2.astype(BF16)
    lg_ref[0] = _dot3(h2, wr_hi_ref[...], wr_lo_ref[...])


def _post_mixer(o, w_out, x, g1, n2, shift, scale, wr_hi, wr_lo):
    b, t, d = x.shape
    k_in = o.shape[-1]
    tm = min(ROW_TILE, t)
    row = lambda i, j: (i, j, 0)
    per_b = lambda i, j: (i, 0, 0)
    fixed = lambda i, j: (0, 0)
    return pl.pallas_call(
        _post_kernel,
        grid=(b, t // tm),
        in_specs=[pl.BlockSpec((1, tm, k_in), row),
                  pl.BlockSpec((k_in, d), fixed),
                  pl.BlockSpec((1, tm, d), row),
                  pl.BlockSpec((1, 1, d), per_b),
                  pl.BlockSpec((1, d), fixed),
                  pl.BlockSpec((1, 1, d), per_b),
                  pl.BlockSpec((1, 1, d), per_b),
                  pl.BlockSpec((d, LANES), fixed),
                  pl.BlockSpec((d, LANES), fixed)],
        out_specs=[pl.BlockSpec((1, tm, d), row),
                   pl.BlockSpec((1, tm, d), row),
                   pl.BlockSpec((1, tm, LANES), row)],
        out_shape=[jax.ShapeDtypeStruct((b, t, d), F32),
                   jax.ShapeDtypeStruct((b, t, d), BF16),
                   jax.ShapeDtypeStruct((b, t, LANES), F32)],
        compiler_params=_params("arbitrary", "arbitrary"),
        name="mixer_out_proj",
    )(o, w_out, x, g1, n2, shift, scale, wr_hi, wr_lo)


def _exclusive_cumsum_lanes(mask):
    rows, t = mask.shape
    width = min(2 * LANES, t)
    r = lax.broadcasted_iota(I32, (width, width), 0)
    c = lax.broadcasted_iota(I32, (width, width), 1)
    upper = (r < c).astype(BF16)
    carry = jnp.zeros((rows, 1), F32)
    outs = []
    for ch in range(t // width):
        m = mask[:, ch * width:(ch + 1) * width]
        outs.append(_dot(m.astype(BF16), upper) + carry)
        carry = carry + jnp.sum(m, axis=1, keepdims=True)
    return jnp.concatenate(outs, axis=1)


BISECT_STEPS = 176


def _affinity_kernel(lg_ref, aff_ref, aff_t_ref):
    t = lg_ref.shape[1]
    lane = lax.broadcasted_iota(I32, (t, LANES), 1)
    lg = jnp.where(lane < N_EXPERTS, lg_ref[0], -jnp.inf)
    e = jnp.exp(lg - jnp.max(lg, axis=-1, keepdims=True))
    aff = e / jnp.sum(e, axis=-1, keepdims=True)
    aff_ref[0] = aff
    aff_t_ref[0] = aff.T[:N_EXPERTS]


def _select_kernel(aff_t_ref, pos_t_ref, *, cap):
    rows = aff_t_ref.shape[0]

    def bisect(_, carry):
        lo, hi = carry
        mid = lo + (hi - lo) * 0.5
        cnt = jnp.sum((aff_t_ref[...] >= mid).astype(F32), axis=1, keepdims=True)
        ok = cnt >= cap
        return jnp.where(ok, mid, lo), jnp.where(ok, hi, mid)

    lo, hi = lax.fori_loop(0, BISECT_STEPS, bisect,
                           (jnp.zeros((rows, 1), F32), jnp.full((rows, 1), 2.0, F32)))
    aff_t = aff_t_ref[...]
    above = (aff_t >= hi).astype(F32)
    tied = ((aff_t >= lo) & (aff_t < hi)).astype(F32)
    need = cap - jnp.sum(above, axis=1, keepdims=True)
    sel = above + tied * (_exclusive_cumsum_lanes(tied) < need).astype(F32)
    slot = _exclusive_cumsum_lanes(sel)
    pos_t_ref[...] = jnp.where(sel > 0, slot, -1.0).astype(I32)


def _route(logits, cap):
    b, t, _ = logits.shape
    aff, aff_t = pl.pallas_call(
        _affinity_kernel,
        grid=(b,),
        in_specs=[pl.BlockSpec((1, t, LANES), lambda i: (i, 0, 0))],
        out_specs=[pl.BlockSpec((1, t, LANES), lambda i: (i, 0, 0)),
                   pl.BlockSpec((1, N_EXPERTS, t), lambda i: (i, 0, 0))],
        out_shape=[jax.ShapeDtypeStruct((b, t, LANES), F32),
                   jax.ShapeDtypeStruct((b, N_EXPERTS, t), F32)],
        compiler_params=_params("arbitrary"),
        name="ec_affinity",
    )(logits)
    rows = b * N_EXPERTS
    pos_t = pl.pallas_call(
        functools.partial(_select_kernel, cap=cap),
        grid=(1,),
        in_specs=[pl.BlockSpec((rows, t), lambda i: (0, 0))],
        out_specs=pl.BlockSpec((rows, t), lambda i: (0, 0)),
        out_shape=jax.ShapeDtypeStruct((rows, t), I32),
        compiler_params=_params("arbitrary"),
        name="ec_select",
    )(aff_t.reshape(rows, t))
    return pos_t.reshape(b, N_EXPERTS, t), aff


def _gather_kernel(pos_t_ref, h_ref, xs_ref, *, cap):
    t = h_ref.shape[1]
    slot = lax.broadcasted_iota(I32, (cap, t), 0)
    onehot = (slot == pos_t_ref[0, 0]).astype(BF16)
    xs_ref[0, 0] = _dot(onehot, h_ref[0]).astype(BF16)


def _gather(pos_t, h, cap):
    b, t, d = h.shape
    return pl.pallas_call(
        functools.partial(_gather_kernel, cap=cap),
        grid=(b, N_EXPERTS),
        in_specs=[pl.BlockSpec((1, 1, 1, t), lambda i, e: (i, e, 0, 0)),
                  pl.BlockSpec((1, t, d), lambda i, e: (i, 0, 0))],
        out_specs=pl.BlockSpec((1, 1, cap, d), lambda i, e: (e, i, 0, 0)),
        out_shape=jax.ShapeDtypeStruct((N_EXPERTS, b, cap, d), BF16),
        compiler_params=_params("arbitrary", "arbitrary"),
        name="ec_gather",
    )(pos_t.reshape(b, N_EXPERTS, 1, t), h)


def _ffn_up_kernel(xs_ref, wg_ref, wu_ref, hm_ref):
    xs = xs_ref[0]
    a = _dot(xs, wg_ref[0, 0].astype(BF16))
    u = _dot(xs, wu_ref[0, 0].astype(BF16))
    hm_ref[0] = (_silu(a) * u).astype(hm_ref.dtype)


def _ffn_down_kernel(hm_ref, wd_ref, y_ref, wd_bf16_ref):
    @pl.when(pl.program_id(1) == 0)
    def _():
        wd_bf16_ref[...] = wd_ref[0, 0].astype(BF16)

    y_ref[0] = _dot(hm_ref[0], wd_bf16_ref[...]).astype(y_ref.dtype)


def _expert_ffn(xs, w_gate, w_up, w_down, layer):
    n_exp, rows, d = xs.shape
    ff = w_gate.shape[-1]
    tm = min(MOE_ROW_TILE, rows)
    tf = min(FF_TILE, ff)
    hm = pl.pallas_call(
        _ffn_up_kernel,
        grid=(n_exp, rows // tm, ff // tf),
        in_specs=[pl.BlockSpec((1, tm, d), lambda e, m, f: (e, m, 0)),
                  pl.BlockSpec((1, 1, d, tf), lambda e, m, f: (layer, e, 0, f)),
                  pl.BlockSpec((1, 1, d, tf), lambda e, m, f: (layer, e, 0, f))],
        out_specs=pl.BlockSpec((1, tm, tf), lambda e, m, f: (e, m, f)),
        out_shape=jax.ShapeDtypeStruct((n_exp, rows, ff), BF16),
        compiler_params=_params("arbitrary", "arbitrary", "arbitrary"),
        name="ec_ffn_up",
    )(xs, w_gate, w_up)
    tm2 = min(MOE_DOWN_ROW_TILE, rows)
    return pl.pallas_call(
        _ffn_down_kernel,
        grid=(n_exp, rows // tm2),
        in_specs=[pl.BlockSpec((1, tm2, ff), lambda e, m: (e, m, 0)),
                  pl.BlockSpec((1, 1, ff, d), lambda e, m: (layer, e, 0, 0))],
        out_specs=pl.BlockSpec((1, tm2, d), lambda e, m: (e, m, 0)),
        out_shape=jax.ShapeDtypeStruct((n_exp, rows, d), BF16),
        scratch_shapes=[pltpu.VMEM((ff, d), BF16)],
        compiler_params=_params("arbitrary", "arbitrary"),
        name="ec_ffn_down",
    )(hm, w_down)


def _combine_kernel(x_ref, g2_ref, pos_t_ref, aff_ref, y_ref, fg_ref, o_ref, *, cap, final_norm):
    tt = x_ref.shape[1]
    pos_t = pos_t_ref[0].astype(F32)
    pad = jnp.full((LANES - N_EXPERTS, tt), -1.0, F32)
    pos = jnp.concatenate([pos_t, pad], axis=0).T.astype(I32)
    aff = aff_ref[0]
    lane = lax.broadcasted_iota(I32, (tt, cap), 1)
    acc = jnp.zeros(x_ref.shape[1:], F32)
    for e in range(N_EXPERTS):
        onehot = (lane == pos[:, e:e + 1]).astype(BF16)
        acc = acc + aff[:, e:e + 1] * _dot(onehot, y_ref[e, 0])
    x2 = x_ref[0] + g2_ref[0] * acc
    if final_norm:
        x2 = (x2 * lax.rsqrt(jnp.mean(x2 * x2, axis=-1, keepdims=True) + NORM_EPS)) * fg_ref[...]
    o_ref[0] = x2


def _combine(x1, g2, pos_t, aff, y, final_g, cap, final_norm):
    b, t, d = x1.shape
    tt = min(ROW_TILE, t)
    row = lambda i, j: (i, j, 0)
    return pl.pallas_call(
        functools.partial(_combine_kernel, cap=cap, final_norm=final_norm),
        grid=(b, t // tt),
        in_specs=[pl.BlockSpec((1, tt, d), row),
                  pl.BlockSpec((1, 1, d), lambda i, j: (i, 0, 0)),
                  pl.BlockSpec((1, N_EXPERTS, tt), lambda i, j: (i, 0, j)),
                  pl.BlockSpec((1, tt, LANES), row),
                  pl.BlockSpec((N_EXPERTS, 1, cap, d), lambda i, j: (0, i, 0, 0)),
                  pl.BlockSpec((1, d), lambda i, j: (0, 0))],
        out_specs=pl.BlockSpec((1, tt, d), row),
        out_shape=jax.ShapeDtypeStruct((b, t, d), F32),
        compiler_params=_params("arbitrary", "arbitrary"),
        name="ec_combine",
    )(x1, g2, pos_t, aff, y, final_g)


def _ec_moe(x1, h2, logits, g2, w_gate, w_up, w_down, layer, final_g, final_norm):
    b, t, d = x1.shape
    cap = CAPACITY_FACTOR * t // N_EXPERTS
    pos_t, aff = _route(logits, cap)
    xs = _gather(pos_t, h2, cap)
    y = _expert_ffn(xs.reshape(N_EXPERTS, b * cap, d), w_gate, w_up, w_down, layer)
    return _combine(x1, g2, pos_t, aff, y.reshape(N_EXPERTS, b, cap, d), final_g, cap, final_norm)


def _pre_gla_kernel(x_ref, g_ref, sh_ref, sc_ref, w_ref, wlr_ref, w2_ref, gb_ref,
                    q_ref, r_ref, k_ref, v_ref, lgf_ref, lgb_ref):
    qk = q_ref.shape[-1]
    vd = v_ref.shape[-1]
    h = _norm_mod(x_ref[0], g_ref[...], sh_ref[0], sc_ref[0]).astype(BF16)
    p = _dot(h, w_ref[...])
    q_ref[0] = (p[:, :qk] * (qk // GLA_HEADS) ** -0.5).astype(BF16)
    r_ref[0] = _silu(p[:, qk:qk + vd]).astype(BF16)
    k_ref[0] = p[:, qk + vd:2 * qk + vd].astype(BF16)
    v_ref[0] = p[:, 2 * qk + vd:].astype(BF16)
    lr = _dot(h, wlr_ref[...]).astype(BF16)
    z = _dot(lr, w2_ref[...]) + gb_ref[...]
    lg = (jnp.minimum(z, 0.0) - jnp.log1p(jnp.exp(-jnp.abs(z)))) * (1.0 / GLA_TAU)
    lgf_ref[0] = lg[:, :qk]
    lgb_ref[0] = lg[:, qk:]


def _pre_gla(x, g, shift, scale, w_main, w_lr, w2, gate_b):
    b, t, d = x.shape
    n_main = w_main.shape[1]
    qk = w2.shape[1] // 2
    vd = (n_main - 2 * qk) // 2
    tm = min(ROW_TILE, t)
    row = lambda i, j: (i, j, 0)
    per_b = lambda i, j: (i, 0, 0)
    fixed = lambda i, j: (0, 0)
    out = lambda n, dt: (pl.BlockSpec((1, tm, n), row), jax.ShapeDtypeStruct((b, t, n), dt))
    outs = [out(qk, BF16), out(vd, BF16), out(qk, BF16), out(vd, BF16), out(qk, F32), out(qk, F32)]
    return pl.pallas_call(
        _pre_gla_kernel,
        grid=(b, t // tm),
        in_specs=[pl.BlockSpec((1, tm, d), row),
                  pl.BlockSpec((1, d), fixed),
                  pl.BlockSpec((1, 1, d), per_b),
                  pl.BlockSpec((1, 1, d), per_b),
                  pl.BlockSpec((d, n_main), fixed),
                  pl.BlockSpec((d, LANES), fixed),
                  pl.BlockSpec((LANES, 2 * qk), fixed),
                  pl.BlockSpec((1, 2 * qk), fixed)],
        out_specs=[o[0] for o in outs],
        out_shape=[o[1] for o in outs],
        compiler_params=_params("arbitrary", "arbitrary"),
        name="gla_in_proj",
    )(x, g, shift, scale, w_main, w_lr, w2, gate_b)


def _gla_masks(rows):
    ri = lax.broadcasted_iota(I32, (rows, rows), 0)
    ci = lax.broadcasted_iota(I32, (rows, rows), 1)
    same = (ri // GLA_CHUNK) == (ci // GLA_CHUNK)
    return same & (ri >= ci), same & (ri <= ci)


def _gla_tile(q, k, v, lg, st_ref, mask, forward):
    ch = GLA_CHUNK
    rows, dk = k.shape
    n_chunks = rows // ch
    m = mask.astype(BF16)
    lg_hi, lg_lo = _split2(lg)
    b = _dot(m, lg_hi) + _dot(m, lg_lo)
    last = [c * ch + (ch - 1 if forward else 0) for c in range(n_chunks)]
    b_last = jnp.concatenate([jnp.broadcast_to(b[r:r + 1, :], (ch, dk)) for r in last], axis=0)
    kf = k.astype(F32)
    kd = (kf * jnp.exp(b_last - b)).astype(BF16)
    if q is not None:
        qt = (q.astype(F32) * jnp.exp(b)).astype(BF16)
        kt = (kf * jnp.exp(-b)).astype(BF16)
        a = lax.dot_general(qt, kt, NT_DIMS, preferred_element_type=F32)
        o_local = _dot(jnp.where(mask, a, 0.0).astype(BF16), v)
    st = st_ref[...]
    outs = [None] * n_chunks
    for c in (range(n_chunks) if forward else reversed(range(n_chunks))):
        rs = slice(c * ch, (c + 1) * ch)
        if q is not None:
            outs[c] = o_local[rs] + lax.dot_general(qt[rs], st.astype(BF16), NT_DIMS, preferred_element_type=F32)
        decay = jnp.exp(b[last[c]:last[c] + 1, :])
        st = decay * st + lax.dot_general(v[rs], kd[rs], TN_DIMS, preferred_element_type=F32)
    st_ref[...] = st
    return None if q is None else jnp.concatenate(outs, axis=0)


def _gla_kernel(q_ref, k_ref, v_ref, lgf_ref, lgb_ref, kc_ref, vc_ref, lgcf_ref, lgcb_ref, r_ref, ng_ref,
                o_ref, of_ref, ob_ref, stf_ref, stb_ref):
    t = q_ref.shape[1]
    t_ctx = kc_ref.shape[1]
    tile = min(GLA_TILE, t)
    ctile = min(GLA_TILE, t_ctx)
    mask_f, mask_b = _gla_masks(tile)
    cmask_f, cmask_b = (mask_f, mask_b) if ctile == tile else _gla_masks(ctile)
    stf_ref[...] = jnp.zeros_like(stf_ref)
    stb_ref[...] = jnp.zeros_like(stb_ref)

    n_ctx = t_ctx // ctile
    for i in range(n_ctx):
        rf = slice(i * ctile, (i + 1) * ctile)
        rb = slice((n_ctx - 1 - i) * ctile, (n_ctx - i) * ctile)
        _gla_tile(None, kc_ref[0, rf, :], vc_ref[0, rf, :], lgcf_ref[0, rf, :], stf_ref, cmask_f, True)
        _gla_tile(None, kc_ref[0, rb, :], vc_ref[0, rb, :], lgcb_ref[0, rb, :], stb_ref, cmask_b, False)

    n = t // tile

    def body(i, carry):
        rf = pl.ds(pl.multiple_of(i * tile, tile), tile)
        rb = pl.ds(pl.multiple_of((n - 1 - i) * tile, tile), tile)
        of_ref[rf, :] = _gla_tile(q_ref[0, rf, :], k_ref[0, rf, :], v_ref[0, rf, :], lgf_ref[0, rf, :],
                                  stf_ref, mask_f, True)
        ob_ref[rb, :] = _gla_tile(q_ref[0, rb, :], k_ref[0, rb, :], v_ref[0, rb, :], lgb_ref[0, rb, :],
                                  stb_ref, mask_b, False)
        return carry

    lax.fori_loop(0, n, body, 0)

    def readout(i, carry):
        rows = pl.ds(pl.multiple_of(i * tile, tile), tile)
        o = of_ref[rows, :] + ob_ref[rows, :]
        o = o * lax.rsqrt(jnp.mean(o * o, axis=-1, keepdims=True) + NORM_EPS) * ng_ref[...]
        o_ref[0, rows, :] = (o * r_ref[0, rows, :].astype(F32)).astype(o_ref.dtype)
        return carry

    lax.fori_loop(0, n, readout, 0)


def _gla(q, k, v, lgf, lgb, kc, vc, lgcf, lgcb, r, norm_g):
    b, t, qk = q.shape
    vd = v.shape[-1]
    t_ctx = kc.shape[1]
    dk, dv = qk // GLA_HEADS, vd // GLA_HEADS
    head = lambda i, h: (i, 0, h)
    lat = lambda n: pl.BlockSpec((1, t, n), head)
    ctx = lambda n: pl.BlockSpec((1, t_ctx, n), head)
    return pl.pallas_call(
        _gla_kernel,
        grid=(b, GLA_HEADS),
        in_specs=[lat(dk), lat(dk), lat(dv), lat(dk), lat(dk),
                  ctx(dk), ctx(dv), ctx(dk), ctx(dk),
                  lat(dv),
                  pl.BlockSpec((1, dv), lambda i, h: (0, h))],
        out_specs=lat(dv),
        out_shape=jax.ShapeDtypeStruct((b, t, vd), BF16),
        scratch_shapes=[pltpu.VMEM((t, dv), F32), pltpu.VMEM((t, dv), F32),
                        pltpu.VMEM((dv, dk), F32), pltpu.VMEM((dv, dk), F32)],
        compiler_params=_params("arbitrary", "arbitrary"),
        name="gla_scan",
    )(q, k, v, lgf, lgb, kc, vc, lgcf, lgcb, r, norm_g)


def _rope_tables(t):
    pos = jnp.arange(t)
    row = (pos // GRID_W).astype(F32)
    col = (pos % GRID_W).astype(F32)
    inv = ROPE_BASE ** (-jnp.arange(ROPE_PAIRS, dtype=F32) / ROPE_PAIRS)
    ar, ac = row[:, None] * inv, col[:, None] * inv
    ang = jnp.concatenate([ar, ar, ac, ac], axis=-1)
    sign = jnp.tile(jnp.repeat(jnp.array([-1.0, 1.0], F32), ROPE_PAIRS), 2)
    reps = LANES // HEAD_DIM
    return jnp.tile(jnp.cos(ang), (1, reps)), jnp.tile(jnp.sin(ang) * sign, (1, reps))


def _router_split(w):
    d, n = w.shape
    w = jnp.pad(w, ((0, 0), (0, LANES - n)))
    hi = w.astype(BF16)
    return hi, (w - hi.astype(F32)).astype(BF16)


def kernel(x, c, ctx, c_ctx, ada_w, ada_b, norm1_g, norm2_g, attn_w_in, attn_w_out, attn_sink, gla_w_in, gla_gate_w2, gla_gate_b, gla_norm_g, gla_w_out, router_w, exp_w_gate, exp_w_up, exp_w_down, final_norm_g):
    b, t, d = x.shape
    depth = ada_w.shape[0]
    assert depth == 2, "layer 0 is windowed attention with context output, layer 1 is GLA and last"
    rows = -(-(b + 1) // 16) * 16
    src = jnp.concatenate([c, c_ctx[None, :], jnp.zeros((rows - b - 1, d), F32)], axis=0)
    cos, sin_signed = _rope_tables(t)
    final_g = final_norm_g.reshape(1, d)

    def chunks(i):
        mod = _modulation(src, ada_w, ada_b, i)
        lat = [mod[:b, j * d:(j + 1) * d].reshape(b, 1, d) for j in range(6)]
        cx = [jnp.broadcast_to(mod[b, j * d:(j + 1) * d].reshape(1, 1, d), (b, 1, d)) for j in range(6)]
        return lat, cx

    (sh1, sc1, g1, sh2, sc2, g2), (sh1c, sc1c, g1c, sh2c, sc2c, g2c) = chunks(0)
    n1, n2 = norm1_g[0].reshape(1, d), norm2_g[0].reshape(1, d)
    w_in = attn_w_in[0].astype(BF16)
    w_out = attn_w_out[0].astype(BF16)
    sink = attn_sink[0].reshape(1, -1)
    wr_hi, wr_lo = _router_split(router_w[0])
    q, k, v = _pre_attn(x, n1, sh1, sc1, w_in, cos, sin_signed, rope=True)
    t_ctx = ctx.shape[1]
    qc, kc, vc = _pre_attn(ctx, n1, sh1c, sc1c, w_in, cos[:t_ctx], sin_signed[:t_ctx], rope=False)
    o = _attn_band(q, k, v, kc, vc, sink)
    oc = _attn_ctx(qc, kc, vc, sink)
    x1, h2, logits = _post_mixer(o, w_out, x, g1, n2, sh2, sc2, wr_hi, wr_lo)
    xc1, hc2, logits_c = _post_mixer(oc, w_out, ctx, g1c, n2, sh2c, sc2c, wr_hi, wr_lo)
    x = _ec_moe(x1, h2, logits, g2, exp_w_gate, exp_w_up, exp_w_down, 0, final_g, False)
    xc = _ec_moe(xc1, hc2, logits_c, g2c, exp_w_gate, exp_w_up, exp_w_down, 0, final_g, False)

    (sh1, sc1, g1, sh2, sc2, g2), (sh1c, sc1c, _, _, _, _) = chunks(1)
    n1, n2 = norm1_g[1].reshape(1, d), norm2_g[1].reshape(1, d)
    qk = gla_gate_w2.shape[-1]
    n_main = gla_w_in.shape[-1] - 2 * GLA_RANK
    w_main = gla_w_in[0][:, :n_main].astype(BF16)
    w_lr = jnp.pad(gla_w_in[0][:, n_main:], ((0, 0), (0, LANES - 2 * GLA_RANK))).astype(BF16)
    w2 = jnp.zeros((LANES, 2 * qk), F32)
    w2 = w2.at[:GLA_RANK, :qk].set(gla_gate_w2[0, 0]).at[GLA_RANK:2 * GLA_RANK, qk:].set(gla_gate_w2[0, 1])
    w2 = w2.astype(BF16)
    gate_b = gla_gate_b[0].reshape(1, 2 * qk)
    gq, gr, gk, gv, lgf, lgb = _pre_gla(x, n1, sh1, sc1, w_main, w_lr, w2, gate_b)
    _, _, gkc, gvc, lgcf, lgcb = _pre_gla(xc, n1, sh1c, sc1c, w_main, w_lr, w2, gate_b)
    og = _gla(gq, gk, gv, lgf, lgb, gkc, gvc, lgcf, lgcb, gr, gla_norm_g[0].reshape(1, -1))
    wr_hi, wr_lo = _router_split(router_w[1])
    x1, h2, logits = _post_mixer(og, gla_w_out[0].astype(BF16), x, g1, n2, sh2, sc2, wr_hi, wr_lo)
    return _ec_moe(x1, h2, logits, g2, exp_w_gate, exp_w_up, exp_w_down, 1, final_g, True)
```

```python
import functools

import jax
import jax.numpy as jnp
from jax import lax
from jax.experimental import pallas as pl
from jax.experimental.pallas import tpu as pltpu

F32 = jnp.float32
BF16 = jnp.bfloat16
I32 = jnp.int32

LANES = 128
HEAD_DIM = 64
ATT_GROUP = 4
ATT_BLOCK = 128
GRID_W = 64
ROPE_BASE = 10000.0
ROPE_PAIRS = HEAD_DIM // 4
GLA_HEADS = 4
GLA_RANK = 16
GLA_TAU = 16.0
GLA_CHUNK = 64
N_EXPERTS = 16
CAPACITY_FACTOR = 2
NORM_EPS = 1e-6
ROW_TILE = 256
MOE_ROW_TILE = 2048
MOE_DOWN_ROW_TILE = 512
FF_TILE = 256
GLA_TILE = 256
SLOT_WINDOW = 64
SLOT_ALIGN = 16
VMEM_LIMIT = 56 * 1024 * 1024

NT_DIMS = (((1,), (1,)), ((), ()))
TN_DIMS = (((0,), (0,)), ((), ()))


def _params(*sem):
    return pltpu.CompilerParams(dimension_semantics=sem, vmem_limit_bytes=VMEM_LIMIT)


def _dot(a, b):
    return jnp.dot(a, b, preferred_element_type=F32)


def _split2(a):
    hi = a.astype(BF16)
    lo = (a - hi.astype(F32)).astype(BF16)
    return hi, lo


def _dot3(a, b_hi, b_lo):
    a_hi, a_lo = _split2(a)
    return _dot(a_hi, b_hi) + _dot(a_hi, b_lo) + _dot(a_lo, b_hi)


def _silu(a):
    return a * jax.nn.sigmoid(a)


def _norm_mod(x, g, shift, scale):
    y = x * lax.rsqrt(jnp.mean(x * x, axis=-1, keepdims=True) + NORM_EPS)
    return (y * g) * (1.0 + scale) + shift


def _mod_kernel(src_ref, w_ref, b_ref, o_ref):
    w_hi, w_lo = _split2(w_ref[0])
    o_ref[...] = _dot3(_silu(src_ref[...]), w_hi, w_lo) + b_ref[0]


def _modulation(src, ada_w, ada_b, layer):
    rows, d = src.shape
    depth, _, n = ada_w.shape
    tn = 512
    return pl.pallas_call(
        _mod_kernel,
        grid=(n // tn,),
        in_specs=[pl.BlockSpec((rows, d), lambda j: (0, 0)),
                  pl.BlockSpec((1, d, tn), lambda j: (layer, 0, j)),
                  pl.BlockSpec((1, 1, tn), lambda j: (layer, 0, j))],
        out_specs=pl.BlockSpec((rows, tn), lambda j: (0, j)),
        out_shape=jax.ShapeDtypeStruct((rows, n), F32),
        compiler_params=_params("arbitrary"),
        name="adaln_mod",
    )(src, ada_w, ada_b.reshape(depth, 1, n))


def _rope(x, cos, sin_signed, lane_lo):
    outs = []
    for j in range(x.shape[1] // LANES):
        xb = x[:, j * LANES:(j + 1) * LANES]
        partner = jnp.where(lane_lo, pltpu.roll(xb, LANES - ROPE_PAIRS, 1), pltpu.roll(xb, ROPE_PAIRS, 1))
        outs.append(xb * cos + partner * sin_signed)
    return jnp.concatenate(outs, axis=1)


def _pre_attn_kernel(x_ref, g_ref, sh_ref, sc_ref, w_ref, cos_ref, sin_ref, q_ref, k_ref, v_ref, *, rope):
    d = x_ref.shape[-1]
    kv = k_ref.shape[-1]
    h = _norm_mod(x_ref[0], g_ref[...], sh_ref[0], sc_ref[0]).astype(BF16)
    p = _dot(h, w_ref[...])
    q, k, v = p[:, :d], p[:, d:d + kv], p[:, d + kv:]
    if rope:
        lane = lax.broadcasted_iota(I32, (x_ref.shape[1], LANES), 1)
        lane_lo = (lane & (2 * ROPE_PAIRS - 1)) < ROPE_PAIRS
        cos, sin_signed = cos_ref[...], sin_ref[...]
        q = _rope(q, cos, sin_signed, lane_lo)
        k = _rope(k, cos, sin_signed, lane_lo)
    q_ref[0] = (q * HEAD_DIM ** -0.5).astype(BF16)
    k_ref[0] = k.astype(BF16)
    v_ref[0] = v.astype(BF16)


def _pre_attn(x, g, shift, scale, w_in, cos, sin_signed, rope):
    b, t, d = x.shape
    n_in = w_in.shape[1]
    kv = (n_in - d) // 2
    tm = min(ROW_TILE, t)
    row = lambda i, j: (i, j, 0)
    per_b = lambda i, j: (i, 0, 0)
    fixed = lambda i, j: (0, 0)
    return pl.pallas_call(
        functools.partial(_pre_attn_kernel, rope=rope),
        grid=(b, t // tm),
        in_specs=[pl.BlockSpec((1, tm, d), row),
                  pl.BlockSpec((1, d), fixed),
                  pl.BlockSpec((1, 1, d), per_b),
                  pl.BlockSpec((1, 1, d), per_b),
                  pl.BlockSpec((d, n_in), fixed),
                  pl.BlockSpec((tm, LANES), lambda i, j: (j, 0)),
                  pl.BlockSpec((tm, LANES), lambda i, j: (j, 0))],
        out_specs=[pl.BlockSpec((1, tm, d), row),
                   pl.BlockSpec((1, tm, kv), row),
                   pl.BlockSpec((1, tm, kv), row)],
        out_shape=[jax.ShapeDtypeStruct((b, t, d), BF16),
                   jax.ShapeDtypeStruct((b, t, kv), BF16),
                   jax.ShapeDtypeStruct((b, t, kv), BF16)],
        compiler_params=_params("arbitrary", "arbitrary"),
        name="attn_in_proj",
    )(x, g, shift, scale, w_in, cos, sin_signed)


def _attn_heads(q, kcat, vcat, valid, sink_ref, o_ref):
    tq = q.shape[0]
    n_kv = kcat.shape[1] // HEAD_DIM
    if valid is not None:
        valid = jnp.concatenate([valid] * ATT_GROUP, axis=0)
    outs = []
    for h in range(n_kv):
        kh = kcat[:, h * HEAD_DIM:(h + 1) * HEAD_DIM]
        vh = vcat[:, h * HEAD_DIM:(h + 1) * HEAD_DIM]
        heads = [ATT_GROUP * h + g for g in range(ATT_GROUP)]
        qg = jnp.concatenate([q[:, i * HEAD_DIM:(i + 1) * HEAD_DIM] for i in heads], axis=0)
        sk = jnp.concatenate([jnp.full((tq, 1), sink_ref[0, i], F32) for i in heads], axis=0)
        s = lax.dot_general(qg, kh, NT_DIMS, preferred_element_type=F32)
        if valid is not None:
            s = jnp.where(valid, s, -jnp.inf)
        m = jnp.maximum(jnp.max(s, axis=-1, keepdims=True), sk)
        e = jnp.exp(s - m)
        den = jnp.sum(e, axis=-1, keepdims=True) + jnp.exp(sk - m)
        o = _dot(e.astype(BF16), vh) / den
        outs.extend(o[g * tq:(g + 1) * tq] for g in range(ATT_GROUP))
    o_ref[0] = jnp.concatenate(outs, axis=1).astype(o_ref.dtype)


def _attn_band_kernel(q_ref, kp_ref, ko_ref, kn_ref, vp_ref, vo_ref, vn_ref, kc_ref, vc_ref, sink_ref, o_ref):
    j = pl.program_id(1)
    nb = pl.num_programs(1)
    n_ctx = kc_ref.shape[1]
    blk = ATT_BLOCK
    kcat = jnp.concatenate([kc_ref[0], kp_ref[0], ko_ref[0], kn_ref[0]], axis=0)
    vcat = jnp.concatenate([vc_ref[0], vp_ref[0], vo_ref[0], vn_ref[0]], axis=0)
    nk = n_ctx + 3 * blk
    row = lax.broadcasted_iota(I32, (blk, nk), 0)
    col = lax.broadcasted_iota(I32, (blk, nk), 1) - n_ctx
    in_window = (col >= row) & (col <= row + 2 * blk)
    in_seq = ((j > 0) | (col >= blk)) & ((j < nb - 1) | (col < 2 * blk))
    valid = (col < 0) | (in_window & in_seq)
    _attn_heads(q_ref[0], kcat, vcat, valid, sink_ref, o_ref)


def _attn_ctx_kernel(q_ref, kc_ref, vc_ref, sink_ref, o_ref):
    _attn_heads(q_ref[0], kc_ref[0], vc_ref[0], None, sink_ref, o_ref)


def _attn_band(q, k, v, kc, vc, sink):
    b, t, d = q.shape
    kv = k.shape[-1]
    n_ctx = kc.shape[1]
    nb = t // ATT_BLOCK
    blk = lambda m: pl.BlockSpec((1, ATT_BLOCK, kv), m)
    prev = lambda i, j: (i, jnp.maximum(j - 1, 0), 0)
    own = lambda i, j: (i, j, 0)
    nxt = lambda i, j: (i, jnp.minimum(j + 1, nb - 1), 0)
    ctx_spec = pl.BlockSpec((1, n_ctx, kv), lambda i, j: (i, 0, 0))
    return pl.pallas_call(
        _attn_band_kernel,
        grid=(b, nb),
        in_specs=[pl.BlockSpec((1, ATT_BLOCK, d), own),
                  blk(prev), blk(own), blk(nxt), blk(prev), blk(own), blk(nxt),
                  ctx_spec, ctx_spec,
                  pl.BlockSpec(memory_space=pltpu.SMEM)],
        out_specs=pl.BlockSpec((1, ATT_BLOCK, d), own),
        out_shape=jax.ShapeDtypeStruct((b, t, d), BF16),
        compiler_params=_params("arbitrary", "arbitrary"),
        name="attn_band",
    )(q, k, k, k, v, v, v, kc, vc, sink)


def _attn_ctx(qc, kc, vc, sink):
    b, n_ctx, d = qc.shape
    kv = kc.shape[-1]
    return pl.pallas_call(
        _attn_ctx_kernel,
        grid=(b,),
        in_specs=[pl.BlockSpec((1, n_ctx, d), lambda i: (i, 0, 0)),
                  pl.BlockSpec((1, n_ctx, kv), lambda i: (i, 0, 0)),
                  pl.BlockSpec((1, n_ctx, kv), lambda i: (i, 0, 0)),
                  pl.BlockSpec(memory_space=pltpu.SMEM)],
        out_specs=pl.BlockSpec((1, n_ctx, d), lambda i: (i, 0, 0)),
        out_shape=jax.ShapeDtypeStruct((b, n_ctx, d), BF16),
        compiler_params=_params("arbitrary"),
        name="attn_ctx",
    )(qc, kc, vc, sink)


def _post_kernel(o_ref, w_ref, x_ref, g1_ref, n2_ref, sh_ref, sc_ref, wr_hi_ref, wr_lo_ref,
                 x1_ref, h2_ref, lg_ref):
    y = _dot(o_ref[0], w_ref[...])
    x1 = x_ref[0] + g1_ref[0] * y
    x1_ref[0] = x1
    h2 = _norm_mod(x1, n2_ref[...], sh_ref[0], sc_ref[0])
    h2_ref[0] = h2.astype(BF16)
    lg_ref[0] = _dot3(h2, wr_hi_ref[...], wr_lo_ref[...])


def _post_mixer(o, w_out, x, g1, n2, shift, scale, wr_hi, wr_lo):
    b, t, d = x.shape
    k_in = o.shape[-1]
    tm = min(ROW_TILE, t)
    row = lambda i, j: (i, j, 0)
    per_b = lambda i, j: (i, 0, 0)
    fixed = lambda i, j: (0, 0)
    return pl.pallas_call(
        _post_kernel,
        grid=(b, t // tm),
        in_specs=[pl.BlockSpec((1, tm, k_in), row),
                  pl.BlockSpec((k_in, d), fixed),
                  pl.BlockSpec((1, tm, d), row),
                  pl.BlockSpec((1, 1, d), per_b),
                  pl.BlockSpec((1, d), fixed),
                  pl.BlockSpec((1, 1, d), per_b),
                  pl.BlockSpec((1, 1, d), per_b),
                  pl.BlockSpec((d, LANES), fixed),
                  pl.BlockSpec((d, LANES), fixed)],
        out_specs=[pl.BlockSpec((1, tm, d), row),
                   pl.BlockSpec((1, tm, d), row),
                   pl.BlockSpec((1, tm, LANES), row)],
        out_shape=[jax.ShapeDtypeStruct((b, t, d), F32),
                   jax.ShapeDtypeStruct((b, t, d), BF16),
                   jax.ShapeDtypeStruct((b, t, LANES), F32)],
        compiler_params=_params("arbitrary", "arbitrary"),
        name="mixer_out_proj",
    )(o, w_out, x, g1, n2, shift, scale, wr_hi, wr_lo)


def _exclusive_cumsum_lanes(mask):
    rows, t = mask.shape
    width = min(2 * LANES, t)
    r = lax.broadcasted_iota(I32, (width, width), 0)
    c = lax.broadcasted_iota(I32, (width, width), 1)
    upper = (r < c).astype(BF16)
    carry = jnp.zeros((rows, 1), F32)
    outs = []
    for ch in range(t // width):
        m = mask[:, ch * width:(ch + 1) * width]
        outs.append(_dot(m.astype(BF16), upper) + carry)
        carry = carry + jnp.sum(m, axis=1, keepdims=True)
    return jnp.concatenate(outs, axis=1)


BISECT_STEPS = 176


def _affinity_kernel(lg_ref, aff_ref, aff_t_ref):
    t = lg_ref.shape[1]
    lane = lax.broadcasted_iota(I32, (t, LANES), 1)
    lg = jnp.where(lane < N_EXPERTS, lg_ref[0], -jnp.inf)
    e = jnp.exp(lg - jnp.max(lg, axis=-1, keepdims=True))
    aff = e / jnp.sum(e, axis=-1, keepdims=True)
    aff_ref[0] = aff
    aff_t_ref[0] = aff.T[:N_EXPERTS]


def _select_kernel(aff_t_ref, pos_t_ref, starts_ref, *, cap):
    rows = aff_t_ref.shape[0]

    def bisect(_, carry):
        lo, hi = carry
        mid = lo + (hi - lo) * 0.5
        cnt = jnp.sum((aff_t_ref[...] >= mid).astype(F32), axis=1, keepdims=True)
        ok = cnt >= cap
        return jnp.where(ok, mid, lo), jnp.where(ok, hi, mid)

    lo, hi = lax.fori_loop(0, BISECT_STEPS, bisect,
                           (jnp.zeros((rows, 1), F32), jnp.full((rows, 1), 2.0, F32)))
    aff_t = aff_t_ref[...]
    above = (aff_t >= hi).astype(F32)
    tied = ((aff_t >= lo) & (aff_t < hi)).astype(F32)
    need = cap - jnp.sum(above, axis=1, keepdims=True)
    sel = above + tied * (_exclusive_cumsum_lanes(tied) < need).astype(F32)
    slot = _exclusive_cumsum_lanes(sel)
    pos_t_ref[...] = jnp.where(sel > 0, slot, -1.0).astype(I32)
    t = aff_t.shape[1]
    tok = lax.broadcasted_iota(I32, (t, LANES), 0)
    edge = lax.broadcasted_iota(I32, (t, LANES), 1) * min(ROW_TILE, t)
    before = ((tok < edge) & (edge <= t)).astype(BF16)
    starts_ref[...] = _dot(sel.astype(BF16), before).astype(I32)


def _route(logits, cap):
    b, t, _ = logits.shape
    aff, aff_t = pl.pallas_call(
        _affinity_kernel,
        grid=(b,),
        in_specs=[pl.BlockSpec((1, t, LANES), lambda i: (i, 0, 0))],
        out_specs=[pl.BlockSpec((1, t, LANES), lambda i: (i, 0, 0)),
                   pl.BlockSpec((1, N_EXPERTS, t), lambda i: (i, 0, 0))],
        out_shape=[jax.ShapeDtypeStruct((b, t, LANES), F32),
                   jax.ShapeDtypeStruct((b, N_EXPERTS, t), F32)],
        compiler_params=_params("arbitrary"),
        name="ec_affinity",
    )(logits)
    rows = b * N_EXPERTS
    pos_t, starts = pl.pallas_call(
        functools.partial(_select_kernel, cap=cap),
        grid=(1,),
        in_specs=[pl.BlockSpec((rows, t), lambda i: (0, 0))],
        out_specs=[pl.BlockSpec((rows, t), lambda i: (0, 0)),
                   pl.BlockSpec((rows, LANES), lambda i: (0, 0))],
        out_shape=[jax.ShapeDtypeStruct((rows, t), I32),
                   jax.ShapeDtypeStruct((rows, LANES), I32)],
        compiler_params=_params("arbitrary"),
        name="ec_select",
    )(aff_t.reshape(rows, t))
    return pos_t.reshape(b, N_EXPERTS, t), starts.reshape(b, N_EXPERTS, LANES), aff


def _gather_kernel(pos_t_ref, h_ref, xs_ref, *, cap):
    t = h_ref.shape[1]
    slot = lax.broadcasted_iota(I32, (cap, t), 0)
    onehot = (slot == pos_t_ref[0, 0]).astype(BF16)
    xs_ref[0, 0] = _dot(onehot, h_ref[0]).astype(BF16)


def _gather(pos_t, h, cap):
    b, t, d = h.shape
    return pl.pallas_call(
        functools.partial(_gather_kernel, cap=cap),
        grid=(b, N_EXPERTS),
        in_specs=[pl.BlockSpec((1, 1, 1, t), lambda i, e: (i, e, 0, 0)),
                  pl.BlockSpec((1, t, d), lambda i, e: (i, 0, 0))],
        out_specs=pl.BlockSpec((1, 1, cap, d), lambda i, e: (e, i, 0, 0)),
        out_shape=jax.ShapeDtypeStruct((N_EXPERTS, b, cap, d), BF16),
        compiler_params=_params("arbitrary", "arbitrary"),
        name="ec_gather",
    )(pos_t.reshape(b, N_EXPERTS, 1, t), h)


def _tile_windows(starts_ref, cap, ntp):
    i, j = pl.program_id(0), pl.program_id(1)
    wins, fits = [], None
    for e in range(N_EXPERTS):
        at = (i * N_EXPERTS + e) * ntp + j
        win = jnp.minimum((starts_ref[at] // SLOT_ALIGN) * SLOT_ALIGN, cap - SLOT_WINDOW)
        ok = starts_ref[at + 1] - win <= SLOT_WINDOW
        wins.append(pl.multiple_of(win, SLOT_ALIGN))
        fits = ok if fits is None else jnp.logical_and(fits, ok)
    return wins, fits


def _gather_win_kernel(starts_ref, pos_t_ref, h_ref, xs_ref, *, cap, ntp):
    w = SLOT_WINDOW
    tile = h_ref.shape[1]
    wins, fits = _tile_windows(starts_ref, cap, ntp)
    h = h_ref[0]
    pos_t = pos_t_ref[0]

    @pl.when(pl.program_id(1) == 0)
    def _():
        xs_ref[...] = jnp.zeros_like(xs_ref)

    @pl.when(fits)
    def _():
        row = lax.broadcasted_iota(I32, (w, tile), 0)
        onehot = jnp.concatenate([(row + wins[e] == pos_t[e:e + 1, :]).astype(BF16) for e in range(N_EXPERTS)],
                                 axis=0)
        z = _dot(onehot, h).astype(BF16)
        for e in range(N_EXPERTS):
            xs_ref[e, 0, pl.ds(wins[e], w), :] += z[e * w:(e + 1) * w]

    @pl.when(jnp.logical_not(fits))
    def _():
        slot = lax.broadcasted_iota(I32, (cap, tile), 0)
        for e in range(N_EXPERTS):
            onehot = (slot == pos_t[e:e + 1, :]).astype(BF16)
            xs_ref[e, 0] += _dot(onehot, h).astype(BF16)


def _gather_windowed(pos_t, starts, h, cap):
    b, t, d = h.shape
    tile = min(ROW_TILE, t)
    nt = t // tile
    grid_spec = pltpu.PrefetchScalarGridSpec(
        num_scalar_prefetch=1,
        grid=(b, nt),
        in_specs=[pl.BlockSpec((1, N_EXPERTS, tile), lambda i, j, s: (i, 0, j)),
                  pl.BlockSpec((1, tile, d), lambda i, j, s: (i, j, 0))],
        out_specs=pl.BlockSpec((N_EXPERTS, 1, cap, d), lambda i, j, s: (0, i, 0, 0)),
    )
    return pl.pallas_call(
        functools.partial(_gather_win_kernel, cap=cap, ntp=nt + 1),
        grid_spec=grid_spec,
        out_shape=jax.ShapeDtypeStruct((N_EXPERTS, b, cap, d), BF16),
        compiler_params=_params("arbitrary", "arbitrary"),
        name="ec_gather_win",
    )(starts[:, :, :nt + 1].reshape(-1), pos_t, h)


def _ffn_up_kernel(xs_ref, wg_ref, wu_ref, hm_ref):
    xs = xs_ref[0]
    a = _dot(xs, wg_ref[0, 0].astype(BF16))
    u = _dot(xs, wu_ref[0, 0].astype(BF16))
    hm_ref[0] = (_silu(a) * u).astype(hm_ref.dtype)


def _ffn_down_kernel(hm_ref, wd_ref, y_ref, wd_bf16_ref):
    @pl.when(pl.program_id(1) == 0)
    def _():
        wd_bf16_ref[...] = wd_ref[0, 0].astype(BF16)

    y_ref[0] = _dot(hm_ref[0], wd_bf16_ref[...]).astype(y_ref.dtype)


def _expert_ffn(xs, w_gate, w_up, w_down, layer):
    n_exp, rows, d = xs.shape
    ff = w_gate.shape[-1]
    tm = min(MOE_ROW_TILE, rows)
    tf = min(FF_TILE, ff)
    hm = pl.pallas_call(
        _ffn_up_kernel,
        grid=(n_exp, rows // tm, ff // tf),
        in_specs=[pl.BlockSpec((1, tm, d), lambda e, m, f: (e, m, 0)),
                  pl.BlockSpec((1, 1, d, tf), lambda e, m, f: (layer, e, 0, f)),
                  pl.BlockSpec((1, 1, d, tf), lambda e, m, f: (layer, e, 0, f))],
        out_specs=pl.BlockSpec((1, tm, tf), lambda e, m, f: (e, m, f)),
        out_shape=jax.ShapeDtypeStruct((n_exp, rows, ff), BF16),
        compiler_params=_params("arbitrary", "arbitrary", "arbitrary"),
        name="ec_ffn_up",
    )(xs, w_gate, w_up)
    tm2 = min(MOE_DOWN_ROW_TILE, rows)
    return pl.pallas_call(
        _ffn_down_kernel,
        grid=(n_exp, rows // tm2),
        in_specs=[pl.BlockSpec((1, tm2, ff), lambda e, m: (e, m, 0)),
                  pl.BlockSpec((1, 1, ff, d), lambda e, m: (layer, e, 0, 0))],
        out_specs=pl.BlockSpec((1, tm2, d), lambda e, m: (e, m, 0)),
        out_shape=jax.ShapeDtypeStruct((n_exp, rows, d), BF16),
        scratch_shapes=[pltpu.VMEM((ff, d), BF16)],
        compiler_params=_params("arbitrary", "arbitrary"),
        name="ec_ffn_down",
    )(hm, w_down)


def _combine_kernel(x_ref, g2_ref, pos_t_ref, aff_ref, y_ref, fg_ref, o_ref, *, cap, final_norm):
    tt = x_ref.shape[1]
    pos = _token_major(pos_t_ref)
    aff = aff_ref[0]
    lane = lax.broadcasted_iota(I32, (tt, cap), 1)
    acc = jnp.zeros(x_ref.shape[1:], F32)
    for e in range(N_EXPERTS):
        onehot = (lane == pos[:, e:e + 1]).astype(BF16)
        acc = acc + aff[:, e:e + 1] * _dot(onehot, y_ref[e, 0])
    _finish_combine(x_ref, g2_ref, fg_ref, o_ref, acc, final_norm)


def _combine(x1, g2, pos_t, aff, y, final_g, cap, final_norm):
    b, t, d = x1.shape
    tt = min(ROW_TILE, t)
    row = lambda i, j: (i, j, 0)
    return pl.pallas_call(
        functools.partial(_combine_kernel, cap=cap, final_norm=final_norm),
        grid=(b, t // tt),
        in_specs=[pl.BlockSpec((1, tt, d), row),
                  pl.BlockSpec((1, 1, d), lambda i, j: (i, 0, 0)),
                  pl.BlockSpec((1, N_EXPERTS, tt), lambda i, j: (i, 0, j)),
                  pl.BlockSpec((1, tt, LANES), row),
                  pl.BlockSpec((N_EXPERTS, 1, cap, d), lambda i, j: (0, i, 0, 0)),
                  pl.BlockSpec((1, d), lambda i, j: (0, 0))],
        out_specs=pl.BlockSpec((1, tt, d), row),
        out_shape=jax.ShapeDtypeStruct((b, t, d), F32),
        compiler_params=_params("arbitrary", "arbitrary"),
        name="ec_combine",
    )(x1, g2, pos_t, aff, y, final_g)


def _token_major(pos_t_ref):
    pos_t = pos_t_ref[0].astype(F32)
    pad = jnp.full((LANES - N_EXPERTS, pos_t.shape[1]), -1.0, F32)
    return jnp.concatenate([pos_t, pad], axis=0).T.astype(I32)


def _finish_combine(x_ref, g2_ref, fg_ref, o_ref, acc, final_norm):
    x2 = x_ref[0] + g2_ref[0] * acc
    if final_norm:
        x2 = (x2 * lax.rsqrt(jnp.mean(x2 * x2, axis=-1, keepdims=True) + NORM_EPS)) * fg_ref[...]
    o_ref[0] = x2


def _combine_win_kernel(starts_ref, x_ref, g2_ref, pos_t_ref, aff_ref, y_ref, fg_ref, o_ref,
                        *, cap, ntp, final_norm):
    w = SLOT_WINDOW
    assert w & (w - 1) == 0, "window column index is taken with a bit mask"
    tile = x_ref.shape[1]
    wins, fits = _tile_windows(starts_ref, cap, ntp)
    pos = _token_major(pos_t_ref)
    aff = aff_ref[0]

    @pl.when(fits)
    def _():
        ywin = jnp.concatenate([y_ref[e, 0, pl.ds(wins[e], w), :] for e in range(N_EXPERTS)], axis=0)
        lane = lax.broadcasted_iota(I32, (1, LANES), 1)
        win_row = jnp.zeros((1, LANES), I32)
        for e in range(N_EXPERTS):
            win_row = jnp.where(lane == e, wins[e], win_row)
        rel = jnp.where(pos >= 0, jnp.clip(pos - win_row, -1, w), -1)
        aff_hi, aff_lo = _split2(aff)
        src = lax.broadcasted_iota(I32, (LANES, N_EXPERTS * w), 0) * w
        col = lax.broadcasted_iota(I32, (LANES, N_EXPERTS * w), 1)
        spread = ((col >= src) & (col < src + w)).astype(BF16)
        wide = _dot(jnp.concatenate([rel.astype(F32).astype(BF16), aff_hi, aff_lo], axis=0), spread)
        in_win = lax.broadcasted_iota(I32, (1, N_EXPERTS * w), 1) & (w - 1)
        match = wide[:tile] == in_win.astype(F32)
        gates = jnp.concatenate([jnp.where(match, wide[tile:2 * tile], 0.0),
                                 jnp.where(match, wide[2 * tile:], 0.0)], axis=0).astype(BF16)
        both = _dot(gates, ywin)
        _finish_combine(x_ref, g2_ref, fg_ref, o_ref, both[:tile] + both[tile:], final_norm)

    @pl.when(jnp.logical_not(fits))
    def _():
        lane = lax.broadcasted_iota(I32, (tile, cap), 1)
        acc = jnp.zeros(x_ref.shape[1:], F32)
        for e in range(N_EXPERTS):
            onehot = (lane == pos[:, e:e + 1]).astype(BF16)
            acc = acc + aff[:, e:e + 1] * _dot(onehot, y_ref[e, 0])
        _finish_combine(x_ref, g2_ref, fg_ref, o_ref, acc, final_norm)


def _combine_windowed(x1, g2, pos_t, starts, aff, y, final_g, cap, final_norm):
    b, t, d = x1.shape
    tile = min(ROW_TILE, t)
    nt = t // tile
    row = lambda i, j, s: (i, j, 0)
    grid_spec = pltpu.PrefetchScalarGridSpec(
        num_scalar_prefetch=1,
        grid=(b, nt),
        in_specs=[pl.BlockSpec((1, tile, d), row),
                  pl.BlockSpec((1, 1, d), lambda i, j, s: (i, 0, 0)),
                  pl.BlockSpec((1, N_EXPERTS, tile), lambda i, j, s: (i, 0, j)),
                  pl.BlockSpec((1, tile, LANES), row),
                  pl.BlockSpec((N_EXPERTS, 1, cap, d), lambda i, j, s: (0, i, 0, 0)),
                  pl.BlockSpec((1, d), lambda i, j, s: (0, 0))],
        out_specs=pl.BlockSpec((1, tile, d), row),
    )
    return pl.pallas_call(
        functools.partial(_combine_win_kernel, cap=cap, ntp=nt + 1, final_norm=final_norm),
        grid_spec=grid_spec,
        out_shape=jax.ShapeDtypeStruct((b, t, d), F32),
        compiler_params=_params("arbitrary", "arbitrary"),
        name="ec_combine_win",
    )(starts[:, :, :nt + 1].reshape(-1), x1, g2, pos_t, aff, y, final_g)


def _ec_moe(x1, h2, logits, g2, w_gate, w_up, w_down, layer, final_g, final_norm):
    b, t, d = x1.shape
    cap = CAPACITY_FACTOR * t // N_EXPERTS
    pos_t, starts, aff = _route(logits, cap)
    windowed = t > ROW_TILE and cap >= SLOT_WINDOW and (cap - SLOT_WINDOW) % SLOT_ALIGN == 0
    xs = _gather_windowed(pos_t, starts, h2, cap) if windowed else _gather(pos_t, h2, cap)
    y = _expert_ffn(xs.reshape(N_EXPERTS, b * cap, d), w_gate, w_up, w_down, layer)
    y = y.reshape(N_EXPERTS, b, cap, d)
    if windowed:
        return _combine_windowed(x1, g2, pos_t, starts, aff, y, final_g, cap, final_norm)
    return _combine(x1, g2, pos_t, aff, y, final_g, cap, final_norm)


def _pre_gla_kernel(x_ref, g_ref, sh_ref, sc_ref, w_ref, wlr_ref, w2_ref, gb_ref,
                    q_ref, r_ref, k_ref, v_ref, lgf_ref, lgb_ref):
    qk = q_ref.shape[-1]
    vd = v_ref.shape[-1]
    h = _norm_mod(x_ref[0], g_ref[...], sh_ref[0], sc_ref[0]).astype(BF16)
    p = _dot(h, w_ref[...])
    q_ref[0] = (p[:, :qk] * (qk // GLA_HEADS) ** -0.5).astype(BF16)
    r_ref[0] = _silu(p[:, qk:qk + vd]).astype(BF16)
    k_ref[0] = p[:, qk + vd:2 * qk + vd].astype(BF16)
    v_ref[0] = p[:, 2 * qk + vd:].astype(BF16)
    lr = _dot(h, wlr_ref[...]).astype(BF16)
    z = _dot(lr, w2_ref[...]) + gb_ref[...]
    lg = (jnp.minimum(z, 0.0) - jnp.log1p(jnp.exp(-jnp.abs(z)))) * (1.0 / GLA_TAU)
    lgf_ref[0] = lg[:, :qk]
    lgb_ref[0] = lg[:, qk:]


def _pre_gla(x, g, shift, scale, w_main, w_lr, w2, gate_b):
    b, t, d = x.shape
    n_main = w_main.shape[1]
    qk = w2.shape[1] // 2
    vd = (n_main - 2 * qk) // 2
    tm = min(ROW_TILE, t)
    row = lambda i, j: (i, j, 0)
    per_b = lambda i, j: (i, 0, 0)
    fixed = lambda i, j: (0, 0)
    out = lambda n, dt: (pl.BlockSpec((1, tm, n), row), jax.ShapeDtypeStruct((b, t, n), dt))
    outs = [out(qk, BF16), out(vd, BF16), out(qk, BF16), out(vd, BF16), out(qk, F32), out(qk, F32)]
    return pl.pallas_call(
        _pre_gla_kernel,
        grid=(b, t // tm),
        in_specs=[pl.BlockSpec((1, tm, d), row),
                  pl.BlockSpec((1, d), fixed),
                  pl.BlockSpec((1, 1, d), per_b),
                  pl.BlockSpec((1, 1, d), per_b),
                  pl.BlockSpec((d, n_main), fixed),
                  pl.BlockSpec((d, LANES), fixed),
                  pl.BlockSpec((LANES, 2 * qk), fixed),
                  pl.BlockSpec((1, 2 * qk), fixed)],
        out_specs=[o[0] for o in outs],
        out_shape=[o[1] for o in outs],
        compiler_params=_params("arbitrary", "arbitrary"),
        name="gla_in_proj",
    )(x, g, shift, scale, w_main, w_lr, w2, gate_b)


def _gla_masks(rows):
    ri = lax.broadcasted_iota(I32, (rows, rows), 0)
    ci = lax.broadcasted_iota(I32, (rows, rows), 1)
    same = (ri // GLA_CHUNK) == (ci // GLA_CHUNK)
    return same & (ri >= ci), same & (ri <= ci)


def _gla_tile(q, k, v, lg, st_ref, mask, forward):
    ch = GLA_CHUNK
    rows, dk = k.shape
    n_chunks = rows // ch
    m = mask.astype(BF16)
    lg_hi, lg_lo = _split2(lg)
    b = _dot(m, lg_hi) + _dot(m, lg_lo)
    last = [c * ch + (ch - 1 if forward else 0) for c in range(n_chunks)]
    b_last = jnp.concatenate([jnp.broadcast_to(b[r:r + 1, :], (ch, dk)) for r in last], axis=0)
    kf = k.astype(F32)
    kd = (kf * jnp.exp(b_last - b)).astype(BF16)
    if q is not None:
        qt = (q.astype(F32) * jnp.exp(b)).astype(BF16)
        kt = (kf * jnp.exp(-b)).astype(BF16)
        a = lax.dot_general(qt, kt, NT_DIMS, preferred_element_type=F32)
        o_local = _dot(jnp.where(mask, a, 0.0).astype(BF16), v)
    st = st_ref[...]
    outs = [None] * n_chunks
    for c in (range(n_chunks) if forward else reversed(range(n_chunks))):
        rs = slice(c * ch, (c + 1) * ch)
        if q is not None:
            outs[c] = o_local[rs] + lax.dot_general(qt[rs], st.astype(BF16), NT_DIMS, preferred_element_type=F32)
        decay = jnp.exp(b[last[c]:last[c] + 1, :])
        st = decay * st + lax.dot_general(v[rs], kd[rs], TN_DIMS, preferred_element_type=F32)
    st_ref[...] = st
    return None if q is None else jnp.concatenate(outs, axis=0)


def _gla_kernel(q_ref, k_ref, v_ref, lgf_ref, lgb_ref, kc_ref, vc_ref, lgcf_ref, lgcb_ref, r_ref, ng_ref,
                o_ref, of_ref, ob_ref, stf_ref, stb_ref):
    t = q_ref.shape[1]
    t_ctx = kc_ref.shape[1]
    tile = min(GLA_TILE, t)
    ctile = min(GLA_TILE, t_ctx)
    mask_f, mask_b = _gla_masks(tile)
    cmask_f, cmask_b = (mask_f, mask_b) if ctile == tile else _gla_masks(ctile)
    stf_ref[...] = jnp.zeros_like(stf_ref)
    stb_ref[...] = jnp.zeros_like(stb_ref)

    n_ctx = t_ctx // ctile
    for i in range(n_ctx):
        rf = slice(i * ctile, (i + 1) * ctile)
        rb = slice((n_ctx - 1 - i) * ctile, (n_ctx - i) * ctile)
        _gla_tile(None, kc_ref[0, rf, :], vc_ref[0, rf, :], lgcf_ref[0, rf, :], stf_ref, cmask_f, True)
        _gla_tile(None, kc_ref[0, rb, :], vc_ref[0, rb, :], lgcb_ref[0, rb, :], stb_ref, cmask_b, False)

    n = t // tile

    def body(i, carry):
        rf = pl.ds(pl.multiple_of(i * tile, tile), tile)
        rb = pl.ds(pl.multiple_of((n - 1 - i) * tile, tile), tile)
        of_ref[rf, :] = _gla_tile(q_ref[0, rf, :], k_ref[0, rf, :], v_ref[0, rf, :], lgf_ref[0, rf, :],
                                  stf_ref, mask_f, True)
        ob_ref[rb, :] = _gla_tile(q_ref[0, rb, :], k_ref[0, rb, :], v_ref[0, rb, :], lgb_ref[0, rb, :],
                                  stb_ref, mask_b, False)
        return carry

    lax.fori_loop(0, n, body, 0)

    def readout(i, carry):
        rows = pl.ds(pl.multiple_of(i * tile, tile), tile)
        o = of_ref[rows, :] + ob_ref[rows, :]
        o = o * lax.rsqrt(jnp.mean(o * o, axis=-1, keepdims=True) + NORM_EPS) * ng_ref[...]
        o_ref[0, rows, :] = (o * r_ref[0, rows, :].astype(F32)).astype(o_ref.dtype)
        return carry

    lax.fori_loop(0, n, readout, 0)


def _gla(q, k, v, lgf, lgb, kc, vc, lgcf, lgcb, r, norm_g):
    b, t, qk = q.shape
    vd = v.shape[-1]
    t_ctx = kc.shape[1]
    dk, dv = qk // GLA_HEADS, vd // GLA_HEADS
    head = lambda i, h: (i, 0, h)
    lat = lambda n: pl.BlockSpec((1, t, n), head)
    ctx = lambda n: pl.BlockSpec((1, t_ctx, n), head)
    return pl.pallas_call(
        _gla_kernel,
        grid=(b, GLA_HEADS),
        in_specs=[lat(dk), lat(dk), lat(dv), lat(dk), lat(dk),
                  ctx(dk), ctx(dv), ctx(dk), ctx(dk),
                  lat(dv),
                  pl.BlockSpec((1, dv), lambda i, h: (0, h))],
        out_specs=lat(dv),
        out_shape=jax.ShapeDtypeStruct((b, t, vd), BF16),
        scratch_shapes=[pltpu.VMEM((t, dv), F32), pltpu.VMEM((t, dv), F32),
                        pltpu.VMEM((dv, dk), F32), pltpu.VMEM((dv, dk), F32)],
        compiler_params=_params("arbitrary", "arbitrary"),
        name="gla_scan",
    )(q, k, v, lgf, lgb, kc, vc, lgcf, lgcb, r, norm_g)


def _rope_tables(t):
    pos = jnp.arange(t)
    row = (pos // GRID_W).astype(F32)
    col = (pos % GRID_W).astype(F32)
    inv = ROPE_BASE ** (-jnp.arange(ROPE_PAIRS, dtype=F32) / ROPE_PAIRS)
    ar, ac = row[:, None] * inv, col[:, None] * inv
    ang = jnp.concatenate([ar, ar, ac, ac], axis=-1)
    sign = jnp.tile(jnp.repeat(jnp.array([-1.0, 1.0], F32), ROPE_PAIRS), 2)
    reps = LANES // HEAD_DIM
    return jnp.tile(jnp.cos(ang), (1, reps)), jnp.tile(jnp.sin(ang) * sign, (1, reps))


def _router_split(w):
    d, n = w.shape
    w = jnp.pad(w, ((0, 0), (0, LANES - n)))
    hi = w.astype(BF16)
    return hi, (w - hi.astype(F32)).astype(BF16)


def kernel(x, c, ctx, c_ctx, ada_w, ada_b, norm1_g, norm2_g, attn_w_in, attn_w_out, attn_sink, gla_w_in, gla_gate_w2, gla_gate_b, gla_norm_g, gla_w_out, router_w, exp_w_gate, exp_w_up, exp_w_down, final_norm_g):
    b, t, d = x.shape
    depth = ada_w.shape[0]
    assert depth == 2, "layer 0 is windowed attention with context output, layer 1 is GLA and last"
    rows = -(-(b + 1) // 16) * 16
    src = jnp.concatenate([c, c_ctx[None, :], jnp.zeros((rows - b - 1, d), F32)], axis=0)
    cos, sin_signed = _rope_tables(t)
    final_g = final_norm_g.reshape(1, d)

    def chunks(i):
        mod = _modulation(src, ada_w, ada_b, i)
        lat = [mod[:b, j * d:(j + 1) * d].reshape(b, 1, d) for j in range(6)]
        cx = [jnp.broadcast_to(mod[b, j * d:(j + 1) * d].reshape(1, 1, d), (b, 1, d)) for j in range(6)]
        return lat, cx

    (sh1, sc1, g1, sh2, sc2, g2), (sh1c, sc1c, g1c, sh2c, sc2c, g2c) = chunks(0)
    n1, n2 = norm1_g[0].reshape(1, d), norm2_g[0].reshape(1, d)
    w_in = attn_w_in[0].astype(BF16)
    w_out = attn_w_out[0].astype(BF16)
    sink = attn_sink[0].reshape(1, -1)
    wr_hi, wr_lo = _router_split(router_w[0])
    q, k, v = _pre_attn(x, n1, sh1, sc1, w_in, cos, sin_signed, rope=True)
    t_ctx = ctx.shape[1]
    qc, kc, vc = _pre_attn(ctx, n1, sh1c, sc1c, w_in, cos[:t_ctx], sin_signed[:t_ctx], rope=False)
    o = _attn_band(q, k, v, kc, vc, sink)
    oc = _attn_ctx(qc, kc, vc, sink)
    x1, h2, logits = _post_mixer(o, w_out, x, g1, n2, sh2, sc2, wr_hi, wr_lo)
    xc1, hc2, logits_c = _post_mixer(oc, w_out, ctx, g1c, n2, sh2c, sc2c, wr_hi, wr_lo)
    x = _ec_moe(x1, h2, logits, g2, exp_w_gate, exp_w_up, exp_w_down, 0, final_g, False)
    xc = _ec_moe(xc1, hc2, logits_c, g2c, exp_w_gate, exp_w_up, exp_w_down, 0, final_g, False)

    (sh1, sc1, g1, sh2, sc2, g2), (sh1c, sc1c, _, _, _, _) = chunks(1)
    n1, n2 = norm1_g[1].reshape(1, d), norm2_g[1].reshape(1, d)
    qk = gla_gate_w2.shape[-1]
    n_main = gla_w_in.shape[-1] - 2 * GLA_RANK
    w_main = gla_w_in[0][:, :n_main].astype(BF16)
    w_lr = jnp.pad(gla_w_in[0][:, n_main:], ((0, 0), (0, LANES - 2 * GLA_RANK))).astype(BF16)
    w2 = jnp.zeros((LANES, 2 * qk), F32)
    w2 = w2.at[:GLA_RANK, :qk].set(gla_gate_w2[0, 0]).at[GLA_RANK:2 * GLA_RANK, qk:].set(gla_gate_w2[0, 1])
    w2 = w2.astype(BF16)
    gate_b = gla_gate_b[0].reshape(1, 2 * qk)
    gq, gr, gk, gv, lgf, lgb = _pre_gla(x, n1, sh1, sc1, w_main, w_lr, w2, gate_b)
    _, _, gkc, gvc, lgcf, lgcb = _pre_gla(xc, n1, sh1c, sc1c, w_main, w_lr, w2, gate_b)
    og = _gla(gq, gk, gv, lgf, lgb, gkc, gvc, lgcf, lgcb, gr, gla_norm_g[0].reshape(1, -1))
    wr_hi, wr_lo = _router_split(router_w[1])
    x1, h2, logits = _post_mixer(og, gla_w_out[0].astype(BF16), x, g1, n2, sh2, sc2, wr_hi, wr_lo)
    return _ec_moe(x1, h2, logits, g2, exp_w_gate, exp_w_up, exp_w_down, 1, final_g, True)
```

```python
import functools

import jax
import jax.numpy as jnp
from jax import lax
from jax.experimental import pallas as pl
from jax.experimental.pallas import tpu as pltpu

F32 = jnp.float32
BF16 = jnp.bfloat16
I32 = jnp.int32

LANES = 128
HEAD_DIM = 64
ATT_GROUP = 4
ATT_BLOCK = 128
GRID_W = 64
ROPE_BASE = 10000.0
ROPE_PAIRS = HEAD_DIM // 4
GLA_HEADS = 4
GLA_RANK = 16
GLA_TAU = 16.0
GLA_CHUNK = 64
N_EXPERTS = 16
CAPACITY_FACTOR = 2
NORM_EPS = 1e-6
LOG2E = 1.4426950408889634
ROW_TILE = 256
MOE_ROW_TILE = 4096
MOE_DOWN_ROW_TILE = 512
FF_TILE = 256
GLA_TILE = 256
GLA_HEADS_PER_STEP = 2
SLOT_WINDOW = 64
SLOT_ALIGN = 16
VMEM_LIMIT = 56 * 1024 * 1024

NT_DIMS = (((1,), (1,)), ((), ()))
TN_DIMS = (((0,), (0,)), ((), ()))


def _params(*sem):
    return pltpu.CompilerParams(dimension_semantics=sem, vmem_limit_bytes=VMEM_LIMIT)


def _dot(a, b):
    return jnp.dot(a, b, preferred_element_type=F32)


def _split2(a):
    hi = a.astype(BF16)
    lo = (a - hi.astype(F32)).astype(BF16)
    return hi, lo


def _dot3(a, b_hi, b_lo):
    a_hi, a_lo = _split2(a)
    return _dot(a_hi, b_hi) + _dot(a_hi, b_lo) + _dot(a_lo, b_hi)


def _silu(a):
    return a * jax.nn.sigmoid(a)


def _norm_mod(x, g, shift, scale):
    y = x * lax.rsqrt(jnp.mean(x * x, axis=-1, keepdims=True) + NORM_EPS)
    return (y * g) * (1.0 + scale) + shift


def _mod_kernel(src_ref, w_ref, b_ref, o_ref):
    w_hi, w_lo = _split2(w_ref[0])
    o_ref[...] = _dot3(_silu(src_ref[...]), w_hi, w_lo) + b_ref[0]


def _modulation(src, ada_w, ada_b, layer):
    rows, d = src.shape
    depth, _, n = ada_w.shape
    tn = 512
    return pl.pallas_call(
        _mod_kernel,
        grid=(n // tn,),
        in_specs=[pl.BlockSpec((rows, d), lambda j: (0, 0)),
                  pl.BlockSpec((1, d, tn), lambda j: (layer, 0, j)),
                  pl.BlockSpec((1, 1, tn), lambda j: (layer, 0, j))],
        out_specs=pl.BlockSpec((rows, tn), lambda j: (0, j)),
        out_shape=jax.ShapeDtypeStruct((rows, n), F32),
        compiler_params=_params("arbitrary"),
        name="adaln_mod",
    )(src, ada_w, ada_b.reshape(depth, 1, n))


def _rope(x, cos, sin_signed, lane_lo):
    outs = []
    for j in range(x.shape[1] // LANES):
        xb = x[:, j * LANES:(j + 1) * LANES]
        partner = jnp.where(lane_lo, pltpu.roll(xb, LANES - ROPE_PAIRS, 1), pltpu.roll(xb, ROPE_PAIRS, 1))
        outs.append(xb * cos + partner * sin_signed)
    return jnp.concatenate(outs, axis=1)


def _pre_attn_kernel(x_ref, g_ref, sh_ref, sc_ref, w_ref, cos_ref, sin_ref, q_ref, k_ref, v_ref, *, rope):
    d = x_ref.shape[-1]
    kv = k_ref.shape[-1]
    h = _norm_mod(x_ref[0], g_ref[...], sh_ref[0], sc_ref[0]).astype(BF16)
    p = _dot(h, w_ref[...])
    q, k, v = p[:, :d], p[:, d:d + kv], p[:, d + kv:]
    if rope:
        lane = lax.broadcasted_iota(I32, (x_ref.shape[1], LANES), 1)
        lane_lo = (lane & (2 * ROPE_PAIRS - 1)) < ROPE_PAIRS
        cos, sin_signed = cos_ref[...], sin_ref[...]
        q = _rope(q, cos, sin_signed, lane_lo)
        k = _rope(k, cos, sin_signed, lane_lo)
    q_ref[0] = (q * (HEAD_DIM ** -0.5 * LOG2E)).astype(BF16)
    k_ref[0] = k.astype(BF16)
    v_ref[0] = v.astype(BF16)


def _pre_attn(x, g, shift, scale, w_in, cos, sin_signed, rope):
    b, t, d = x.shape
    n_in = w_in.shape[1]
    kv = (n_in - d) // 2
    tm = min(ROW_TILE, t)
    row = lambda i, j: (i, j, 0)
    per_b = lambda i, j: (i, 0, 0)
    fixed = lambda i, j: (0, 0)
    return pl.pallas_call(
        functools.partial(_pre_attn_kernel, rope=rope),
        grid=(b, t // tm),
        in_specs=[pl.BlockSpec((1, tm, d), row),
                  pl.BlockSpec((1, d), fixed),
                  pl.BlockSpec((1, 1, d), per_b),
                  pl.BlockSpec((1, 1, d), per_b),
                  pl.BlockSpec((d, n_in), fixed),
                  pl.BlockSpec((tm, LANES), lambda i, j: (j, 0)),
                  pl.BlockSpec((tm, LANES), lambda i, j: (j, 0))],
        out_specs=[pl.BlockSpec((1, tm, d), row),
                   pl.BlockSpec((1, tm, kv), row),
                   pl.BlockSpec((1, tm, kv), row)],
        out_shape=[jax.ShapeDtypeStruct((b, t, d), BF16),
                   jax.ShapeDtypeStruct((b, t, kv), BF16),
                   jax.ShapeDtypeStruct((b, t, kv), BF16)],
        compiler_params=_params("arbitrary", "arbitrary"),
        name="attn_in_proj",
    )(x, g, shift, scale, w_in, cos, sin_signed)


def _attn_heads(q, k_parts, v_parts, bias_parts, sink_ref, o_ref):
    tq = q.shape[0]
    n_kv = k_parts[0].shape[1] // HEAD_DIM
    lane = lax.broadcasted_iota(I32, (1, ATT_GROUP * tq), 1)
    v_lane = lax.broadcasted_iota(I32, (1, LANES), 1)
    outs = []
    for h in range(n_kv):
        heads = [ATT_GROUP * h + g for g in range(ATT_GROUP)]
        qg = jnp.concatenate([q[:, i * HEAD_DIM:(i + 1) * HEAD_DIM] for i in heads], axis=0)
        sk = jnp.full((1, ATT_GROUP * tq), sink_ref[0, heads[0]], F32)
        for g in range(1, ATT_GROUP):
            sk = jnp.where(lane >= g * tq, sink_ref[0, heads[g]], sk)
        sk = sk * LOG2E
        s_parts = []
        for kp, bias in zip(k_parts, bias_parts):
            s = lax.dot_general(kp[:, h * HEAD_DIM:(h + 1) * HEAD_DIM], qg, NT_DIMS, preferred_element_type=F32)
            s_parts.append(s if bias is None else s + bias)
        m = sk
        for s in s_parts:
            m = jnp.maximum(m, jnp.max(s, axis=0, keepdims=True))
        e = jnp.concatenate([jnp.exp2(s - m).astype(BF16) for s in s_parts], axis=0)
        block = (h * HEAD_DIM // LANES) * LANES
        upper = (h * HEAD_DIM) % LANES != 0
        v_pair = jnp.concatenate([vp[:, block:block + LANES] for vp in v_parts], axis=0)
        own = (v_lane >= HEAD_DIM) if upper else (v_lane < HEAD_DIM)
        v_aug = jnp.where(own, v_pair, jnp.ones_like(v_pair))
        both = lax.dot_general(v_aug, e, TN_DIMS, preferred_element_type=F32)
        num = both[HEAD_DIM:] if upper else both[:HEAD_DIM]
        total = both[:1] if upper else both[HEAD_DIM:HEAD_DIM + 1]
        o_t = num / (total + jnp.exp2(sk - m))
        outs.extend(o_t[:, g * tq:(g + 1) * tq] for g in range(ATT_GROUP))
    o_ref[0] = jnp.concatenate(outs, axis=0).T.astype(o_ref.dtype)


def _band_bias():
    blk = ATT_BLOCK
    key = jnp.arange(blk)[:, None]
    qry = jnp.arange(ATT_GROUP * blk)[None, :] % blk
    prev_ok = key >= qry
    next_ok = key <= qry
    never = jnp.zeros_like(prev_ok)
    variants = [jnp.stack([never if first else prev_ok, never if last else next_ok])
                for last in (False, True) for first in (False, True)]
    return jnp.where(jnp.stack(variants), 0.0, -jnp.inf).astype(F32)


def _attn_band_kernel(q_ref, kp_ref, ko_ref, kn_ref, vp_ref, vo_ref, vn_ref, kc_ref, vc_ref, bias_ref, sink_ref,
                      o_ref):
    _attn_heads(q_ref[0], [kc_ref[0], kp_ref[0], ko_ref[0], kn_ref[0]], [vc_ref[0], vp_ref[0], vo_ref[0], vn_ref[0]],
                [None, bias_ref[0, 0], None, bias_ref[0, 1]], sink_ref, o_ref)


def _attn_ctx_kernel(q_ref, kc_ref, vc_ref, sink_ref, o_ref):
    _attn_heads(q_ref[0], [kc_ref[0]], [vc_ref[0]], [None], sink_ref, o_ref)


def _attn_band(q, k, v, kc, vc, sink):
    b, t, d = q.shape
    kv = k.shape[-1]
    n_ctx = kc.shape[1]
    nb = t // ATT_BLOCK
    blk = lambda m: pl.BlockSpec((1, ATT_BLOCK, kv), m)
    prev = lambda i, j: (i, jnp.maximum(j - 1, 0), 0)
    own = lambda i, j: (i, j, 0)
    nxt = lambda i, j: (i, jnp.minimum(j + 1, nb - 1), 0)
    ctx_spec = pl.BlockSpec((1, n_ctx, kv), lambda i, j: (i, 0, 0))
    bias = _band_bias()
    edge = lambda i, j: ((j == 0).astype(I32) + 2 * (j == nb - 1).astype(I32), 0, 0, 0)
    return pl.pallas_call(
        _attn_band_kernel,
        grid=(b, nb),
        in_specs=[pl.BlockSpec((1, ATT_BLOCK, d), own),
                  blk(prev), blk(own), blk(nxt), blk(prev), blk(own), blk(nxt),
                  ctx_spec, ctx_spec,
                  pl.BlockSpec((1,) + bias.shape[1:], edge),
                  pl.BlockSpec(memory_space=pltpu.SMEM)],
        out_specs=pl.BlockSpec((1, ATT_BLOCK, d), own),
        out_shape=jax.ShapeDtypeStruct((b, t, d), BF16),
        compiler_params=_params("arbitrary", "arbitrary"),
        name="attn_band",
    )(q, k, k, k, v, v, v, kc, vc, bias, sink)


def _attn_ctx(qc, kc, vc, sink):
    b, n_ctx, d = qc.shape
    kv = kc.shape[-1]
    return pl.pallas_call(
        _attn_ctx_kernel,
        grid=(b,),
        in_specs=[pl.BlockSpec((1, n_ctx, d), lambda i: (i, 0, 0)),
                  pl.BlockSpec((1, n_ctx, kv), lambda i: (i, 0, 0)),
                  pl.BlockSpec((1, n_ctx, kv), lambda i: (i, 0, 0)),
                  pl.BlockSpec(memory_space=pltpu.SMEM)],
        out_specs=pl.BlockSpec((1, n_ctx, d), lambda i: (i, 0, 0)),
        out_shape=jax.ShapeDtypeStruct((b, n_ctx, d), BF16),
        compiler_params=_params("arbitrary"),
        name="attn_ctx",
    )(qc, kc, vc, sink)


def _post_kernel(o_ref, w_ref, x_ref, g1_ref, n2_ref, sh_ref, sc_ref, wr_hi_ref, wr_lo_ref,
                 x1_ref, h2_ref, lg_ref):
    y = _dot(o_ref[0], w_ref[...])
    x1 = x_ref[0] + g1_ref[0] * y
    x1_ref[0] = x1
    h2 = _norm_mod(x1, n2_ref[...], sh_ref[0], sc_ref[0])
    h2_ref[0] = h2.astype(BF16)
    lg_ref[0] = _dot3(h2, wr_hi_ref[...], wr_lo_ref[...])


def _post_mixer(o, w_out, x, g1, n2, shift, scale, wr_hi, wr_lo):
    b, t, d = x.shape
    k_in = o.shape[-1]
    tm = min(ROW_TILE, t)
    row = lambda i, j: (i, j, 0)
    per_b = lambda i, j: (i, 0, 0)
    fixed = lambda i, j: (0, 0)
    return pl.pallas_call(
        _post_kernel,
        grid=(b, t // tm),
        in_specs=[pl.BlockSpec((1, tm, k_in), row),
                  pl.BlockSpec((k_in, d), fixed),
                  pl.BlockSpec((1, tm, d), row),
                  pl.BlockSpec((1, 1, d), per_b),
                  pl.BlockSpec((1, d), fixed),
                  pl.BlockSpec((1, 1, d), per_b),
                  pl.BlockSpec((1, 1, d), per_b),
                  pl.BlockSpec((d, LANES), fixed),
                  pl.BlockSpec((d, LANES), fixed)],
        out_specs=[pl.BlockSpec((1, tm, d), row),
                   pl.BlockSpec((1, tm, d), row),
                   pl.BlockSpec((1, tm, LANES), row)],
        out_shape=[jax.ShapeDtypeStruct((b, t, d), F32),
                   jax.ShapeDtypeStruct((b, t, d), BF16),
                   jax.ShapeDtypeStruct((b, t, LANES), F32)],
        compiler_params=_params("arbitrary", "arbitrary"),
        name="mixer_out_proj",
    )(o, w_out, x, g1, n2, shift, scale, wr_hi, wr_lo)


def _exclusive_cumsum_lanes(mask):
    rows, t = mask.shape
    width = min(2 * LANES, t)
    r = lax.broadcasted_iota(I32, (width, width), 0)
    c = lax.broadcasted_iota(I32, (width, width), 1)
    upper = (r < c).astype(BF16)
    carry = jnp.zeros((rows, 1), F32)
    outs = []
    for ch in range(t // width):
        m = mask[:, ch * width:(ch + 1) * width]
        outs.append(_dot(m.astype(BF16), upper) + carry)
        carry = carry + jnp.sum(m, axis=1, keepdims=True)
    return jnp.concatenate(outs, axis=1)


BISECT_STEPS = 176


def _affinity_kernel(lg_ref, aff_ref, aff_t_ref):
    t = lg_ref.shape[1]
    lane = lax.broadcasted_iota(I32, (t, LANES), 1)
    lg = jnp.where(lane < N_EXPERTS, lg_ref[0], -jnp.inf)
    e = jnp.exp(lg - jnp.max(lg, axis=-1, keepdims=True))
    aff = e / jnp.sum(e, axis=-1, keepdims=True)
    aff_ref[0] = aff
    aff_t_ref[0] = aff.T[:N_EXPERTS]


def _select_kernel(aff_t_ref, pos_t_ref, starts_ref, *, cap):
    rows = aff_t_ref.shape[0]

    def bisect(_, carry):
        lo, hi = carry
        mid = lo + (hi - lo) * 0.5
        cnt = jnp.sum((aff_t_ref[...] >= mid).astype(F32), axis=1, keepdims=True)
        ok = cnt >= cap
        return jnp.where(ok, mid, lo), jnp.where(ok, hi, mid)

    lo, hi = lax.fori_loop(0, BISECT_STEPS, bisect,
                           (jnp.zeros((rows, 1), F32), jnp.full((rows, 1), 2.0, F32)))
    aff_t = aff_t_ref[...]
    above = (aff_t >= hi).astype(F32)
    tied = ((aff_t >= lo) & (aff_t < hi)).astype(F32)
    need = cap - jnp.sum(above, axis=1, keepdims=True)
    sel = above + tied * (_exclusive_cumsum_lanes(tied) < need).astype(F32)
    slot = _exclusive_cumsum_lanes(sel)
    pos_t_ref[...] = jnp.where(sel > 0, slot, -1.0).astype(I32)
    t = aff_t.shape[1]
    tok = lax.broadcasted_iota(I32, (t, LANES), 0)
    edge = lax.broadcasted_iota(I32, (t, LANES), 1) * min(ROW_TILE, t)
    before = ((tok < edge) & (edge <= t)).astype(BF16)
    starts_ref[...] = _dot(sel.astype(BF16), before).astype(I32)


def _route(logits, cap):
    b, t, _ = logits.shape
    aff, aff_t = pl.pallas_call(
        _affinity_kernel,
        grid=(b,),
        in_specs=[pl.BlockSpec((1, t, LANES), lambda i: (i, 0, 0))],
        out_specs=[pl.BlockSpec((1, t, LANES), lambda i: (i, 0, 0)),
                   pl.BlockSpec((1, N_EXPERTS, t), lambda i: (i, 0, 0))],
        out_shape=[jax.ShapeDtypeStruct((b, t, LANES), F32),
                   jax.ShapeDtypeStruct((b, N_EXPERTS, t), F32)],
        compiler_params=_params("arbitrary"),
        name="ec_affinity",
    )(logits)
    rows = b * N_EXPERTS
    pos_t, starts = pl.pallas_call(
        functools.partial(_select_kernel, cap=cap),
        grid=(1,),
        in_specs=[pl.BlockSpec((rows, t), lambda i: (0, 0))],
        out_specs=[pl.BlockSpec((rows, t), lambda i: (0, 0)),
                   pl.BlockSpec((rows, LANES), lambda i: (0, 0))],
        out_shape=[jax.ShapeDtypeStruct((rows, t), I32),
                   jax.ShapeDtypeStruct((rows, LANES), I32)],
        compiler_params=_params("arbitrary"),
        name="ec_select",
    )(aff_t.reshape(rows, t))
    return pos_t.reshape(b, N_EXPERTS, t), starts.reshape(b, N_EXPERTS, LANES), aff


def _gather_kernel(pos_t_ref, h_ref, xs_ref, *, cap):
    t = h_ref.shape[1]
    slot = lax.broadcasted_iota(I32, (cap, t), 0)
    onehot = (slot == pos_t_ref[0, 0]).astype(BF16)
    xs_ref[0, 0] = _dot(onehot, h_ref[0]).astype(BF16)


def _gather(pos_t, h, cap):
    b, t, d = h.shape
    return pl.pallas_call(
        functools.partial(_gather_kernel, cap=cap),
        grid=(b, N_EXPERTS),
        in_specs=[pl.BlockSpec((1, 1, 1, t), lambda i, e: (i, e, 0, 0)),
                  pl.BlockSpec((1, t, d), lambda i, e: (i, 0, 0))],
        out_specs=pl.BlockSpec((1, 1, cap, d), lambda i, e: (e, i, 0, 0)),
        out_shape=jax.ShapeDtypeStruct((N_EXPERTS, b, cap, d), BF16),
        compiler_params=_params("arbitrary", "arbitrary"),
        name="ec_gather",
    )(pos_t.reshape(b, N_EXPERTS, 1, t), h)


def _tile_windows(starts_ref, cap, ntp):
    i, j = pl.program_id(0), pl.program_id(1)
    wins, fits = [], None
    for e in range(N_EXPERTS):
        at = (i * N_EXPERTS + e) * ntp + j
        win = jnp.minimum((starts_ref[at] // SLOT_ALIGN) * SLOT_ALIGN, cap - SLOT_WINDOW)
        ok = starts_ref[at + 1] - win <= SLOT_WINDOW
        wins.append(pl.multiple_of(win, SLOT_ALIGN))
        fits = ok if fits is None else jnp.logical_and(fits, ok)
    return wins, fits


def _gather_win_kernel(starts_ref, pos_t_ref, h_ref, xs_ref, *, cap, ntp):
    w = SLOT_WINDOW
    tile = h_ref.shape[1]
    wins, fits = _tile_windows(starts_ref, cap, ntp)
    h = h_ref[0]
    pos_t = pos_t_ref[0]

    @pl.when(pl.program_id(1) == 0)
    def _():
        xs_ref[...] = jnp.zeros_like(xs_ref)

    @pl.when(fits)
    def _():
        row = lax.broadcasted_iota(I32, (w, tile), 0)
        onehot = jnp.concatenate([(row + wins[e] == pos_t[e:e + 1, :]).astype(BF16) for e in range(N_EXPERTS)],
                                 axis=0)
        z = _dot(onehot, h).astype(BF16)
        for e in range(N_EXPERTS):
            xs_ref[e, 0, pl.ds(wins[e], w), :] += z[e * w:(e + 1) * w]

    @pl.when(jnp.logical_not(fits))
    def _():
        slot = lax.broadcasted_iota(I32, (cap, tile), 0)
        for e in range(N_EXPERTS):
            onehot = (slot == pos_t[e:e + 1, :]).astype(BF16)
            xs_ref[e, 0] += _dot(onehot, h).astype(BF16)


def _gather_windowed(pos_t, starts, h, cap):
    b, t, d = h.shape
    tile = min(ROW_TILE, t)
    nt = t // tile
    grid_spec = pltpu.PrefetchScalarGridSpec(
        num_scalar_prefetch=1,
        grid=(b, nt),
        in_specs=[pl.BlockSpec((1, N_EXPERTS, tile), lambda i, j, s: (i, 0, j)),
                  pl.BlockSpec((1, tile, d), lambda i, j, s: (i, j, 0))],
        out_specs=pl.BlockSpec((N_EXPERTS, 1, cap, d), lambda i, j, s: (0, i, 0, 0)),
    )
    return pl.pallas_call(
        functools.partial(_gather_win_kernel, cap=cap, ntp=nt + 1),
        grid_spec=grid_spec,
        out_shape=jax.ShapeDtypeStruct((N_EXPERTS, b, cap, d), BF16),
        compiler_params=_params("arbitrary", "arbitrary"),
        name="ec_gather_win",
    )(starts[:, :, :nt + 1].reshape(-1), pos_t, h)


def _ffn_up_kernel(xs_ref, wg_ref, wu_ref, hm_ref):
    xs = xs_ref[0]
    a = _dot(xs, wg_ref[0, 0].astype(BF16))
    u = _dot(xs, wu_ref[0, 0].astype(BF16))
    hm_ref[0] = (_silu(a) * u).astype(hm_ref.dtype)


def _ffn_down_kernel(hm_ref, wd_ref, y_ref, wd_bf16_ref):
    @pl.when(pl.program_id(1) == 0)
    def _():
        wd_bf16_ref[...] = wd_ref[0, 0].astype(BF16)

    y_ref[0] = _dot(hm_ref[0], wd_bf16_ref[...]).astype(y_ref.dtype)


def _expert_ffn(xs, w_gate, w_up, w_down, layer):
    n_exp, rows, d = xs.shape
    ff = w_gate.shape[-1]
    tm = min(MOE_ROW_TILE, rows)
    tf = min(FF_TILE, ff)
    hm = pl.pallas_call(
        _ffn_up_kernel,
        grid=(n_exp, rows // tm, ff // tf),
        in_specs=[pl.BlockSpec((1, tm, d), lambda e, m, f: (e, m, 0)),
                  pl.BlockSpec((1, 1, d, tf), lambda e, m, f: (layer, e, 0, f)),
                  pl.BlockSpec((1, 1, d, tf), lambda e, m, f: (layer, e, 0, f))],
        out_specs=pl.BlockSpec((1, tm, tf), lambda e, m, f: (e, m, f)),
        out_shape=jax.ShapeDtypeStruct((n_exp, rows, ff), BF16),
        compiler_params=_params("arbitrary", "arbitrary", "arbitrary"),
        name="ec_ffn_up",
    )(xs, w_gate, w_up)
    tm2 = min(MOE_DOWN_ROW_TILE, rows)
    return pl.pallas_call(
        _ffn_down_kernel,
        grid=(n_exp, rows // tm2),
        in_specs=[pl.BlockSpec((1, tm2, ff), lambda e, m: (e, m, 0)),
                  pl.BlockSpec((1, 1, ff, d), lambda e, m: (layer, e, 0, 0))],
        out_specs=pl.BlockSpec((1, tm2, d), lambda e, m: (e, m, 0)),
        out_shape=jax.ShapeDtypeStruct((n_exp, rows, d), BF16),
        scratch_shapes=[pltpu.VMEM((ff, d), BF16)],
        compiler_params=_params("arbitrary", "arbitrary"),
        name="ec_ffn_down",
    )(hm, w_down)


def _combine_kernel(x_ref, g2_ref, pos_t_ref, aff_ref, y_ref, fg_ref, o_ref, *, cap, final_norm):
    tt = x_ref.shape[1]
    pos = _token_major(pos_t_ref)
    aff = aff_ref[0]
    lane = lax.broadcasted_iota(I32, (tt, cap), 1)
    acc = jnp.zeros(x_ref.shape[1:], F32)
    for e in range(N_EXPERTS):
        onehot = (lane == pos[:, e:e + 1]).astype(BF16)
        acc = acc + aff[:, e:e + 1] * _dot(onehot, y_ref[e, 0])
    _finish_combine(x_ref, g2_ref, fg_ref, o_ref, acc, final_norm)


def _combine(x1, g2, pos_t, aff, y, final_g, cap, final_norm):
    b, t, d = x1.shape
    tt = min(ROW_TILE, t)
    row = lambda i, j: (i, j, 0)
    return pl.pallas_call(
        functools.partial(_combine_kernel, cap=cap, final_norm=final_norm),
        grid=(b, t // tt),
        in_specs=[pl.BlockSpec((1, tt, d), row),
                  pl.BlockSpec((1, 1, d), lambda i, j: (i, 0, 0)),
                  pl.BlockSpec((1, N_EXPERTS, tt), lambda i, j: (i, 0, j)),
                  pl.BlockSpec((1, tt, LANES), row),
                  pl.BlockSpec((N_EXPERTS, 1, cap, d), lambda i, j: (0, i, 0, 0)),
                  pl.BlockSpec((1, d), lambda i, j: (0, 0))],
        out_specs=pl.BlockSpec((1, tt, d), row),
        out_shape=jax.ShapeDtypeStruct((b, t, d), F32),
        compiler_params=_params("arbitrary", "arbitrary"),
        name="ec_combine",
    )(x1, g2, pos_t, aff, y, final_g)


def _token_major(pos_t_ref):
    pos_t = pos_t_ref[0].astype(F32)
    pad = jnp.full((LANES - N_EXPERTS, pos_t.shape[1]), -1.0, F32)
    return jnp.concatenate([pos_t, pad], axis=0).T.astype(I32)


def _finish_combine(x_ref, g2_ref, fg_ref, o_ref, acc, final_norm):
    x2 = x_ref[0] + g2_ref[0] * acc
    if final_norm:
        x2 = (x2 * lax.rsqrt(jnp.mean(x2 * x2, axis=-1, keepdims=True) + NORM_EPS)) * fg_ref[...]
    o_ref[0] = x2


def _combine_win_kernel(starts_ref, x_ref, g2_ref, pos_t_ref, aff_ref, y_ref, fg_ref, o_ref,
                        *, cap, ntp, final_norm):
    w = SLOT_WINDOW
    assert w & (w - 1) == 0, "window column index is taken with a bit mask"
    tile = x_ref.shape[1]
    wins, fits = _tile_windows(starts_ref, cap, ntp)
    pos = _token_major(pos_t_ref)
    aff = aff_ref[0]

    @pl.when(fits)
    def _():
        ywin = jnp.concatenate([y_ref[e, 0, pl.ds(wins[e], w), :] for e in range(N_EXPERTS)], axis=0)
        lane = lax.broadcasted_iota(I32, (1, LANES), 1)
        win_row = jnp.zeros((1, LANES), I32)
        for e in range(N_EXPERTS):
            win_row = jnp.where(lane == e, wins[e], win_row)
        rel = jnp.where(pos >= 0, jnp.clip(pos - win_row, -1, w), -1)
        aff_hi, aff_lo = _split2(aff)
        src = lax.broadcasted_iota(I32, (LANES, N_EXPERTS * w), 0) * w
        col = lax.broadcasted_iota(I32, (LANES, N_EXPERTS * w), 1)
        spread = ((col >= src) & (col < src + w)).astype(BF16)
        wide = _dot(jnp.concatenate([rel.astype(F32).astype(BF16), aff_hi, aff_lo], axis=0), spread)
        in_win = lax.broadcasted_iota(I32, (1, N_EXPERTS * w), 1) & (w - 1)
        match = wide[:tile] == in_win.astype(F32)
        gates = jnp.concatenate([jnp.where(match, wide[tile:2 * tile], 0.0),
                                 jnp.where(match, wide[2 * tile:], 0.0)], axis=0).astype(BF16)
        both = _dot(gates, ywin)
        _finish_combine(x_ref, g2_ref, fg_ref, o_ref, both[:tile] + both[tile:], final_norm)

    @pl.when(jnp.logical_not(fits))
    def _():
        lane = lax.broadcasted_iota(I32, (tile, cap), 1)
        acc = jnp.zeros(x_ref.shape[1:], F32)
        for e in range(N_EXPERTS):
            onehot = (lane == pos[:, e:e + 1]).astype(BF16)
            acc = acc + aff[:, e:e + 1] * _dot(onehot, y_ref[e, 0])
        _finish_combine(x_ref, g2_ref, fg_ref, o_ref, acc, final_norm)


def _combine_windowed(x1, g2, pos_t, starts, aff, y, final_g, cap, final_norm):
    b, t, d = x1.shape
    tile = min(ROW_TILE, t)
    nt = t // tile
    row = lambda i, j, s: (i, j, 0)
    grid_spec = pltpu.PrefetchScalarGridSpec(
        num_scalar_prefetch=1,
        grid=(b, nt),
        in_specs=[pl.BlockSpec((1, tile, d), row),
                  pl.BlockSpec((1, 1, d), lambda i, j, s: (i, 0, 0)),
                  pl.BlockSpec((1, N_EXPERTS, tile), lambda i, j, s: (i, 0, j)),
                  pl.BlockSpec((1, tile, LANES), row),
                  pl.BlockSpec((N_EXPERTS, 1, cap, d), lambda i, j, s: (0, i, 0, 0)),
                  pl.BlockSpec((1, d), lambda i, j, s: (0, 0))],
        out_specs=pl.BlockSpec((1, tile, d), row),
    )
    return pl.pallas_call(
        functools.partial(_combine_win_kernel, cap=cap, ntp=nt + 1, final_norm=final_norm),
        grid_spec=grid_spec,
        out_shape=jax.ShapeDtypeStruct((b, t, d), F32),
        compiler_params=_params("arbitrary", "arbitrary"),
        name="ec_combine_win",
    )(starts[:, :, :nt + 1].reshape(-1), x1, g2, pos_t, aff, y, final_g)


def _ec_moe(x1, h2, logits, g2, w_gate, w_up, w_down, layer, final_g, final_norm):
    b, t, d = x1.shape
    cap = CAPACITY_FACTOR * t // N_EXPERTS
    pos_t, starts, aff = _route(logits, cap)
    windowed = t > ROW_TILE and cap >= SLOT_WINDOW and (cap - SLOT_WINDOW) % SLOT_ALIGN == 0
    xs = _gather_windowed(pos_t, starts, h2, cap) if windowed else _gather(pos_t, h2, cap)
    y = _expert_ffn(xs.reshape(N_EXPERTS, b * cap, d), w_gate, w_up, w_down, layer)
    y = y.reshape(N_EXPERTS, b, cap, d)
    if windowed:
        return _combine_windowed(x1, g2, pos_t, starts, aff, y, final_g, cap, final_norm)
    return _combine(x1, g2, pos_t, aff, y, final_g, cap, final_norm)


def _pre_gla_kernel(x_ref, g_ref, sh_ref, sc_ref, w_ref, wlr_ref, w2_ref, gb_ref,
                    q_ref, r_ref, k_ref, v_ref, lgf_ref, lgb_ref):
    qk = q_ref.shape[-1]
    vd = v_ref.shape[-1]
    h = _norm_mod(x_ref[0], g_ref[...], sh_ref[0], sc_ref[0]).astype(BF16)
    p = _dot(h, w_ref[...])
    q_ref[0] = (p[:, :qk] * (qk // GLA_HEADS) ** -0.5).astype(BF16)
    r_ref[0] = _silu(p[:, qk:qk + vd]).astype(BF16)
    k_ref[0] = p[:, qk + vd:2 * qk + vd].astype(BF16)
    v_ref[0] = p[:, 2 * qk + vd:].astype(BF16)
    lr = _dot(h, wlr_ref[...]).astype(BF16)
    z = _dot(lr, w2_ref[...]) + gb_ref[...]
    lg = (jnp.minimum(z, 0.0) - jnp.log1p(jnp.exp(-jnp.abs(z)))) * (1.0 / GLA_TAU)
    lgf_ref[0] = lg[:, :qk]
    lgb_ref[0] = lg[:, qk:]


def _pre_gla(x, g, shift, scale, w_main, w_lr, w2, gate_b):
    b, t, d = x.shape
    n_main = w_main.shape[1]
    qk = w2.shape[1] // 2
    vd = (n_main - 2 * qk) // 2
    tm = min(ROW_TILE, t)
    row = lambda i, j: (i, j, 0)
    per_b = lambda i, j: (i, 0, 0)
    fixed = lambda i, j: (0, 0)
    out = lambda n, dt: (pl.BlockSpec((1, tm, n), row), jax.ShapeDtypeStruct((b, t, n), dt))
    outs = [out(qk, BF16), out(vd, BF16), out(qk, BF16), out(vd, BF16), out(qk, F32), out(qk, F32)]
    return pl.pallas_call(
        _pre_gla_kernel,
        grid=(b, t // tm),
        in_specs=[pl.BlockSpec((1, tm, d), row),
                  pl.BlockSpec((1, d), fixed),
                  pl.BlockSpec((1, 1, d), per_b),
                  pl.BlockSpec((1, 1, d), per_b),
                  pl.BlockSpec((d, n_main), fixed),
                  pl.BlockSpec((d, LANES), fixed),
                  pl.BlockSpec((LANES, 2 * qk), fixed),
                  pl.BlockSpec((1, 2 * qk), fixed)],
        out_specs=[o[0] for o in outs],
        out_shape=[o[1] for o in outs],
        compiler_params=_params("arbitrary", "arbitrary"),
        name="gla_in_proj",
    )(x, g, shift, scale, w_main, w_lr, w2, gate_b)


def _gla_masks(rows):
    ri = lax.broadcasted_iota(I32, (rows, rows), 0)
    ci = lax.broadcasted_iota(I32, (rows, rows), 1)
    same = (ri // GLA_CHUNK) == (ci // GLA_CHUNK)
    return same & (ri >= ci), same & (ri <= ci)


def _gla_tile(q, k, v, lg, st_ref, mask, forward):
    ch = GLA_CHUNK
    rows, dk = k.shape
    n_chunks = rows // ch
    m = mask.astype(BF16)
    lg_hi, lg_lo = _split2(lg)
    b = _dot(m, lg_hi) + _dot(m, lg_lo)
    last = [c * ch + (ch - 1 if forward else 0) for c in range(n_chunks)]
    b_last = jnp.concatenate([jnp.broadcast_to(b[r:r + 1, :], (ch, dk)) for r in last], axis=0)
    kf = k.astype(F32)
    kd = (kf * jnp.exp(b_last - b)).astype(BF16)
    if q is not None:
        qt = (q.astype(F32) * jnp.exp(b)).astype(BF16)
        kt = (kf * jnp.exp(-b)).astype(BF16)
        a = lax.dot_general(qt, kt, NT_DIMS, preferred_element_type=F32)
        o_local = _dot(jnp.where(mask, a, 0.0).astype(BF16), v)
    st = st_ref[...]
    outs = [None] * n_chunks
    for c in (range(n_chunks) if forward else reversed(range(n_chunks))):
        rs = slice(c * ch, (c + 1) * ch)
        if q is not None:
            outs[c] = o_local[rs] + lax.dot_general(qt[rs], st.astype(BF16), NT_DIMS, preferred_element_type=F32)
        decay = jnp.exp(b[last[c]:last[c] + 1, :])
        st = decay * st + lax.dot_general(v[rs], kd[rs], TN_DIMS, preferred_element_type=F32)
    st_ref[...] = st
    return None if q is None else jnp.concatenate(outs, axis=0)


def _gla_kernel(q_ref, k_ref, v_ref, lgf_ref, lgb_ref, kc_ref, vc_ref, lgcf_ref, lgcb_ref, r_ref, ng_ref,
                o_ref, of_ref, ob_ref, stf_ref, stb_ref):
    t = q_ref.shape[1]
    t_ctx = kc_ref.shape[1]
    n_heads = stf_ref.shape[0]
    dv, dk = stf_ref.shape[1:]
    tile = min(GLA_TILE, t)
    ctile = min(GLA_TILE, t_ctx)
    mask_f, mask_b = _gla_masks(tile)
    cmask_f, cmask_b = (mask_f, mask_b) if ctile == tile else _gla_masks(ctile)
    stf_ref[...] = jnp.zeros_like(stf_ref)
    stb_ref[...] = jnp.zeros_like(stb_ref)
    kcols = [slice(h * dk, (h + 1) * dk) for h in range(n_heads)]
    vcols = [slice(h * dv, (h + 1) * dv) for h in range(n_heads)]

    n_ctx = t_ctx // ctile
    for i in range(n_ctx):
        rf = slice(i * ctile, (i + 1) * ctile)
        rb = slice((n_ctx - 1 - i) * ctile, (n_ctx - i) * ctile)
        for h in range(n_heads):
            _gla_tile(None, kc_ref[0, rf, kcols[h]], vc_ref[0, rf, vcols[h]], lgcf_ref[0, rf, kcols[h]],
                      stf_ref.at[h], cmask_f, True)
            _gla_tile(None, kc_ref[0, rb, kcols[h]], vc_ref[0, rb, vcols[h]], lgcb_ref[0, rb, kcols[h]],
                      stb_ref.at[h], cmask_b, False)

    n = t // tile

    def body(i, carry):
        rf = pl.ds(pl.multiple_of(i * tile, tile), tile)
        rb = pl.ds(pl.multiple_of((n - 1 - i) * tile, tile), tile)
        for h in range(n_heads):
            of_ref[rf, vcols[h]] = _gla_tile(q_ref[0, rf, kcols[h]], k_ref[0, rf, kcols[h]], v_ref[0, rf, vcols[h]],
                                             lgf_ref[0, rf, kcols[h]], stf_ref.at[h], mask_f, True)
            ob_ref[rb, vcols[h]] = _gla_tile(q_ref[0, rb, kcols[h]], k_ref[0, rb, kcols[h]], v_ref[0, rb, vcols[h]],
                                             lgb_ref[0, rb, kcols[h]], stb_ref.at[h], mask_b, False)
        return carry

    lax.fori_loop(0, n, body, 0)

    def readout(i, carry):
        rows = pl.ds(pl.multiple_of(i * tile, tile), tile)
        for h in range(n_heads):
            o = of_ref[rows, vcols[h]] + ob_ref[rows, vcols[h]]
            o = o * lax.rsqrt(jnp.mean(o * o, axis=-1, keepdims=True) + NORM_EPS) * ng_ref[:, vcols[h]]
            o_ref[0, rows, vcols[h]] = (o * r_ref[0, rows, vcols[h]].astype(F32)).astype(o_ref.dtype)
        return carry

    lax.fori_loop(0, n, readout, 0)


def _gla(q, k, v, lgf, lgb, kc, vc, lgcf, lgcb, r, norm_g):
    b, t, qk = q.shape
    vd = v.shape[-1]
    t_ctx = kc.shape[1]
    dk, dv = qk // GLA_HEADS, vd // GLA_HEADS
    hp = GLA_HEADS_PER_STEP
    head = lambda i, h: (i, 0, h)
    lat = lambda n: pl.BlockSpec((1, t, hp * n), head)
    ctx = lambda n: pl.BlockSpec((1, t_ctx, hp * n), head)
    return pl.pallas_call(
        _gla_kernel,
        grid=(b, GLA_HEADS // hp),
        in_specs=[lat(dk), lat(dk), lat(dv), lat(dk), lat(dk),
                  ctx(dk), ctx(dv), ctx(dk), ctx(dk),
                  lat(dv),
                  pl.BlockSpec((1, hp * dv), lambda i, h: (0, h))],
        out_specs=lat(dv),
        out_shape=jax.ShapeDtypeStruct((b, t, vd), BF16),
        scratch_shapes=[pltpu.VMEM((t, hp * dv), F32), pltpu.VMEM((t, hp * dv), F32),
                        pltpu.VMEM((hp, dv, dk), F32), pltpu.VMEM((hp, dv, dk), F32)],
        compiler_params=_params("arbitrary", "arbitrary"),
        name="gla_scan",
    )(q, k, v, lgf, lgb, kc, vc, lgcf, lgcb, r, norm_g)


def _rope_tables(t):
    pos = jnp.arange(t)
    row = (pos // GRID_W).astype(F32)
    col = (pos % GRID_W).astype(F32)
    inv = ROPE_BASE ** (-jnp.arange(ROPE_PAIRS, dtype=F32) / ROPE_PAIRS)
    ar, ac = row[:, None] * inv, col[:, None] * inv
    ang = jnp.concatenate([ar, ar, ac, ac], axis=-1)
    sign = jnp.tile(jnp.repeat(jnp.array([-1.0, 1.0], F32), ROPE_PAIRS), 2)
    reps = LANES // HEAD_DIM
    return jnp.tile(jnp.cos(ang), (1, reps)), jnp.tile(jnp.sin(ang) * sign, (1, reps))


def _router_split(w):
    d, n = w.shape
    w = jnp.pad(w, ((0, 0), (0, LANES - n)))
    hi = w.astype(BF16)
    return hi, (w - hi.astype(F32)).astype(BF16)


def kernel(x, c, ctx, c_ctx, ada_w, ada_b, norm1_g, norm2_g, attn_w_in, attn_w_out, attn_sink, gla_w_in, gla_gate_w2, gla_gate_b, gla_norm_g, gla_w_out, router_w, exp_w_gate, exp_w_up, exp_w_down, final_norm_g):
    b, t, d = x.shape
    depth = ada_w.shape[0]
    assert depth == 2, "layer 0 is windowed attention with context output, layer 1 is GLA and last"
    rows = -(-(b + 1) // 16) * 16
    src = jnp.concatenate([c, c_ctx[None, :], jnp.zeros((rows - b - 1, d), F32)], axis=0)
    cos, sin_signed = _rope_tables(t)
    final_g = final_norm_g.reshape(1, d)

    def chunks(i):
        mod = _modulation(src, ada_w, ada_b, i)
        lat = [mod[:b, j * d:(j + 1) * d].reshape(b, 1, d) for j in range(6)]
        cx = [jnp.broadcast_to(mod[b, j * d:(j + 1) * d].reshape(1, 1, d), (b, 1, d)) for j in range(6)]
        return lat, cx

    (sh1, sc1, g1, sh2, sc2, g2), (sh1c, sc1c, g1c, sh2c, sc2c, g2c) = chunks(0)
    n1, n2 = norm1_g[0].reshape(1, d), norm2_g[0].reshape(1, d)
    w_in = attn_w_in[0].astype(BF16)
    w_out = attn_w_out[0].astype(BF16)
    sink = attn_sink[0].reshape(1, -1)
    wr_hi, wr_lo = _router_split(router_w[0])
    q, k, v = _pre_attn(x, n1, sh1, sc1, w_in, cos, sin_signed, rope=True)
    t_ctx = ctx.shape[1]
    qc, kc, vc = _pre_attn(ctx, n1, sh1c, sc1c, w_in, cos[:t_ctx], sin_signed[:t_ctx], rope=False)
    o = _attn_band(q, k, v, kc, vc, sink)
    oc = _attn_ctx(qc, kc, vc, sink)
    x1, h2, logits = _post_mixer(o, w_out, x, g1, n2, sh2, sc2, wr_hi, wr_lo)
    xc1, hc2, logits_c = _post_mixer(oc, w_out, ctx, g1c, n2, sh2c, sc2c, wr_hi, wr_lo)
    x = _ec_moe(x1, h2, logits, g2, exp_w_gate, exp_w_up, exp_w_down, 0, final_g, False)
    xc = _ec_moe(xc1, hc2, logits_c, g2c, exp_w_gate, exp_w_up, exp_w_down, 0, final_g, False)

    (sh1, sc1, g1, sh2, sc2, g2), (sh1c, sc1c, _, _, _, _) = chunks(1)
    n1, n2 = norm1_g[1].reshape(1, d), norm2_g[1].reshape(1, d)
    qk = gla_gate_w2.shape[-1]
    n_main = gla_w_in.shape[-1] - 2 * GLA_RANK
    w_main = gla_w_in[0][:, :n_main].astype(BF16)
    w_lr = jnp.pad(gla_w_in[0][:, n_main:], ((0, 0), (0, LANES - 2 * GLA_RANK))).astype(BF16)
    w2 = jnp.zeros((LANES, 2 * qk), F32)
    w2 = w2.at[:GLA_RANK, :qk].set(gla_gate_w2[0, 0]).at[GLA_RANK:2 * GLA_RANK, qk:].set(gla_gate_w2[0, 1])
    w2 = w2.astype(BF16)
    gate_b = gla_gate_b[0].reshape(1, 2 * qk)
    gq, gr, gk, gv, lgf, lgb = _pre_gla(x, n1, sh1, sc1, w_main, w_lr, w2, gate_b)
    _, _, gkc, gvc, lgcf, lgcb = _pre_gla(xc, n1, sh1c, sc1c, w_main, w_lr, w2, gate_b)
    og = _gla(gq, gk, gv, lgf, lgb, gkc, gvc, lgcf, lgcb, gr, gla_norm_g[0].reshape(1, -1))
    wr_hi, wr_lo = _router_split(router_w[1])
    x1, h2, logits = _post_mixer(og, gla_w_out[0].astype(BF16), x, g1, n2, sh2, sc2, wr_hi, wr_lo)
    return _ec_moe(x1, h2, logits, g2, exp_w_gate, exp_w_up, exp_w_down, 1, final_g, True)
```

```python
import functools

import jax
import jax.numpy as jnp
from jax import lax
from jax.experimental import pallas as pl
from jax.experimental.pallas import tpu as pltpu

F32 = jnp.float32
BF16 = jnp.bfloat16
I32 = jnp.int32

LANES = 128
HEAD_DIM = 64
ATT_GROUP = 4
ATT_BLOCK = 128
GRID_W = 64
ROPE_BASE = 10000.0
ROPE_PAIRS = HEAD_DIM // 4
GLA_HEADS = 4
GLA_RANK = 16
GLA_TAU = 16.0
GLA_CHUNK = 64
N_EXPERTS = 16
CAPACITY_FACTOR = 2
NORM_EPS = 1e-6
LOG2E = 1.4426950408889634
ROW_TILE = 256
ATTN_IN_ROW_TILE = 512
TOKEN_TILE = 256
MOE_ROW_TILE = 4096
MOE_DOWN_ROW_TILE = 512
FF_TILE = 256
FEW_ROWS = 512
FF_TILE_FEW_ROWS = 1408
GLA_TILE = 256
GLA_HEADS_PER_STEP = 2
SLOT_WINDOW = 64
SLOT_ALIGN = 16
VMEM_LIMIT = 56 * 1024 * 1024

NT_DIMS = (((1,), (1,)), ((), ()))
TN_DIMS = (((0,), (0,)), ((), ()))


def _params(*sem):
    return pltpu.CompilerParams(dimension_semantics=sem, vmem_limit_bytes=VMEM_LIMIT)


def _dot(a, b):
    return jnp.dot(a, b, preferred_element_type=F32)


def _split2(a):
    hi = a.astype(BF16)
    lo = (a - hi.astype(F32)).astype(BF16)
    return hi, lo


def _dot3(a, b_hi, b_lo):
    a_hi, a_lo = _split2(a)
    return _dot(a_hi, b_hi) + _dot(a_hi, b_lo) + _dot(a_lo, b_hi)


def _silu(a):
    return a * jax.nn.sigmoid(a)


def _norm_mod(x, g, shift, scale):
    y = x * lax.rsqrt(jnp.mean(x * x, axis=-1, keepdims=True) + NORM_EPS)
    return (y * g) * (1.0 + scale) + shift


def _mod_kernel(src_ref, w_ref, b_ref, o_ref):
    w_hi, w_lo = _split2(w_ref[0])
    o_ref[...] = _dot3(_silu(src_ref[...]), w_hi, w_lo) + b_ref[0]


def _modulation(src, ada_w, ada_b, layer):
    rows, d = src.shape
    depth, _, n = ada_w.shape
    tn = 512
    return pl.pallas_call(
        _mod_kernel,
        grid=(n // tn,),
        in_specs=[pl.BlockSpec((rows, d), lambda j: (0, 0)),
                  pl.BlockSpec((1, d, tn), lambda j: (layer, 0, j)),
                  pl.BlockSpec((1, 1, tn), lambda j: (layer, 0, j))],
        out_specs=pl.BlockSpec((rows, tn), lambda j: (0, j)),
        out_shape=jax.ShapeDtypeStruct((rows, n), F32),
        compiler_params=_params("arbitrary"),
        name="adaln_mod",
    )(src, ada_w, ada_b.reshape(depth, 1, n))


def _rope(x, cos, sin_signed, lane_lo):
    outs = []
    for j in range(x.shape[1] // LANES):
        xb = x[:, j * LANES:(j + 1) * LANES]
        partner = jnp.where(lane_lo, pltpu.roll(xb, LANES - ROPE_PAIRS, 1), pltpu.roll(xb, ROPE_PAIRS, 1))
        outs.append(xb * cos + partner * sin_signed)
    return jnp.concatenate(outs, axis=1)


def _pre_attn_kernel(x_ref, g_ref, sh_ref, sc_ref, w_ref, cos_ref, sin_ref, q_ref, k_ref, v_ref, *, rope):
    d = x_ref.shape[-1]
    kv = k_ref.shape[-1]
    h = _norm_mod(x_ref[0], g_ref[...], sh_ref[0], sc_ref[0]).astype(BF16)
    p = _dot(h, w_ref[...])
    q, k, v = p[:, :d], p[:, d:d + kv], p[:, d + kv:]
    if rope:
        lane = lax.broadcasted_iota(I32, (x_ref.shape[1], LANES), 1)
        lane_lo = (lane & (2 * ROPE_PAIRS - 1)) < ROPE_PAIRS
        cos, sin_signed = cos_ref[...], sin_ref[...]
        q = _rope(q, cos, sin_signed, lane_lo)
        k = _rope(k, cos, sin_signed, lane_lo)
    q_ref[0] = (q * (HEAD_DIM ** -0.5 * LOG2E)).astype(BF16)
    k_ref[0] = k.astype(BF16)
    v_ref[0] = v.astype(BF16)


def _pre_attn(x, g, shift, scale, w_in, cos, sin_signed, rope):
    b, t, d = x.shape
    n_in = w_in.shape[1]
    kv = (n_in - d) // 2
    tm = min(ATTN_IN_ROW_TILE, t)
    row = lambda i, j: (i, j, 0)
    per_b = lambda i, j: (i, 0, 0)
    fixed = lambda i, j: (0, 0)
    return pl.pallas_call(
        functools.partial(_pre_attn_kernel, rope=rope),
        grid=(b, t // tm),
        in_specs=[pl.BlockSpec((1, tm, d), row),
                  pl.BlockSpec((1, d), fixed),
                  pl.BlockSpec((1, 1, d), per_b),
                  pl.BlockSpec((1, 1, d), per_b),
                  pl.BlockSpec((d, n_in), fixed),
                  pl.BlockSpec((tm, LANES), lambda i, j: (j, 0)),
                  pl.BlockSpec((tm, LANES), lambda i, j: (j, 0))],
        out_specs=[pl.BlockSpec((1, tm, d), row),
                   pl.BlockSpec((1, tm, kv), row),
                   pl.BlockSpec((1, tm, kv), row)],
        out_shape=[jax.ShapeDtypeStruct((b, t, d), BF16),
                   jax.ShapeDtypeStruct((b, t, kv), BF16),
                   jax.ShapeDtypeStruct((b, t, kv), BF16)],
        compiler_params=_params("arbitrary", "arbitrary"),
        name="attn_in_proj",
    )(x, g, shift, scale, w_in, cos, sin_signed)


def _attn_heads(q, k_parts, v_parts, bias_parts, sink_ref, o_ref):
    tq = q.shape[0]
    n_kv = k_parts[0].shape[1] // HEAD_DIM
    lane = lax.broadcasted_iota(I32, (1, ATT_GROUP * tq), 1)
    v_lane = lax.broadcasted_iota(I32, (1, LANES), 1)
    outs = []
    for h in range(n_kv):
        heads = [ATT_GROUP * h + g for g in range(ATT_GROUP)]
        qg = jnp.concatenate([q[:, i * HEAD_DIM:(i + 1) * HEAD_DIM] for i in heads], axis=0)
        sk = jnp.full((1, ATT_GROUP * tq), sink_ref[0, heads[0]], F32)
        for g in range(1, ATT_GROUP):
            sk = jnp.where(lane >= g * tq, sink_ref[0, heads[g]], sk)
        sk = sk * LOG2E
        s_parts = []
        for kp, bias in zip(k_parts, bias_parts):
            s = lax.dot_general(kp[:, h * HEAD_DIM:(h + 1) * HEAD_DIM], qg, NT_DIMS, preferred_element_type=F32)
            s_parts.append(s if bias is None else s + bias)
        m = sk
        for s in s_parts:
            m = jnp.maximum(m, jnp.max(s, axis=0, keepdims=True))
        e = jnp.concatenate([jnp.exp2(s - m).astype(BF16) for s in s_parts], axis=0)
        block = (h * HEAD_DIM // LANES) * LANES
        upper = (h * HEAD_DIM) % LANES != 0
        v_pair = jnp.concatenate([vp[:, block:block + LANES] for vp in v_parts], axis=0)
        own = (v_lane >= HEAD_DIM) if upper else (v_lane < HEAD_DIM)
        v_aug = jnp.where(own, v_pair, jnp.ones_like(v_pair))
        both = lax.dot_general(v_aug, e, TN_DIMS, preferred_element_type=F32)
        num = both[HEAD_DIM:] if upper else both[:HEAD_DIM]
        total = both[:1] if upper else both[HEAD_DIM:HEAD_DIM + 1]
        o_t = num / (total + jnp.exp2(sk - m))
        outs.extend(o_t[:, g * tq:(g + 1) * tq] for g in range(ATT_GROUP))
    o_ref[0] = jnp.concatenate(outs, axis=0).T.astype(o_ref.dtype)


def _band_bias():
    blk = ATT_BLOCK
    key = jnp.arange(blk)[:, None]
    qry = jnp.arange(ATT_GROUP * blk)[None, :] % blk
    prev_ok = key >= qry
    next_ok = key <= qry
    never = jnp.zeros_like(prev_ok)
    variants = [jnp.stack([never if first else prev_ok, never if last else next_ok])
                for last in (False, True) for first in (False, True)]
    return jnp.where(jnp.stack(variants), 0.0, -jnp.inf).astype(F32)


def _attn_band_kernel(q_ref, kp_ref, ko_ref, kn_ref, vp_ref, vo_ref, vn_ref, kc_ref, vc_ref, bias_ref, sink_ref,
                      o_ref):
    _attn_heads(q_ref[0], [kc_ref[0], kp_ref[0], ko_ref[0], kn_ref[0]], [vc_ref[0], vp_ref[0], vo_ref[0], vn_ref[0]],
                [None, bias_ref[0, 0], None, bias_ref[0, 1]], sink_ref, o_ref)


def _attn_ctx_kernel(q_ref, kc_ref, vc_ref, sink_ref, o_ref):
    _attn_heads(q_ref[0], [kc_ref[0]], [vc_ref[0]], [None], sink_ref, o_ref)


def _attn_band(q, k, v, kc, vc, sink):
    b, t, d = q.shape
    kv = k.shape[-1]
    n_ctx = kc.shape[1]
    nb = t // ATT_BLOCK
    blk = lambda m: pl.BlockSpec((1, ATT_BLOCK, kv), m)
    prev = lambda i, j: (i, jnp.maximum(j - 1, 0), 0)
    own = lambda i, j: (i, j, 0)
    nxt = lambda i, j: (i, jnp.minimum(j + 1, nb - 1), 0)
    ctx_spec = pl.BlockSpec((1, n_ctx, kv), lambda i, j: (i, 0, 0))
    bias = _band_bias()
    edge = lambda i, j: ((j == 0).astype(I32) + 2 * (j == nb - 1).astype(I32), 0, 0, 0)
    return pl.pallas_call(
        _attn_band_kernel,
        grid=(b, nb),
        in_specs=[pl.BlockSpec((1, ATT_BLOCK, d), own),
                  blk(prev), blk(own), blk(nxt), blk(prev), blk(own), blk(nxt),
                  ctx_spec, ctx_spec,
                  pl.BlockSpec((1,) + bias.shape[1:], edge),
                  pl.BlockSpec(memory_space=pltpu.SMEM)],
        out_specs=pl.BlockSpec((1, ATT_BLOCK, d), own),
        out_shape=jax.ShapeDtypeStruct((b, t, d), BF16),
        compiler_params=_params("arbitrary", "arbitrary"),
        name="attn_band",
    )(q, k, k, k, v, v, v, kc, vc, bias, sink)


def _attn_ctx(qc, kc, vc, sink):
    b, n_ctx, d = qc.shape
    kv = kc.shape[-1]
    return pl.pallas_call(
        _attn_ctx_kernel,
        grid=(b,),
        in_specs=[pl.BlockSpec((1, n_ctx, d), lambda i: (i, 0, 0)),
                  pl.BlockSpec((1, n_ctx, kv), lambda i: (i, 0, 0)),
                  pl.BlockSpec((1, n_ctx, kv), lambda i: (i, 0, 0)),
                  pl.BlockSpec(memory_space=pltpu.SMEM)],
        out_specs=pl.BlockSpec((1, n_ctx, d), lambda i: (i, 0, 0)),
        out_shape=jax.ShapeDtypeStruct((b, n_ctx, d), BF16),
        compiler_params=_params("arbitrary"),
        name="attn_ctx",
    )(qc, kc, vc, sink)


def _post_kernel(o_ref, w_ref, x_ref, g1_ref, n2_ref, sh_ref, sc_ref, wr_hi_ref, wr_lo_ref,
                 x1_ref, h2_ref, lg_ref):
    y = _dot(o_ref[0], w_ref[...])
    x1 = x_ref[0] + g1_ref[0] * y
    x1_ref[0] = x1
    h2 = _norm_mod(x1, n2_ref[...], sh_ref[0], sc_ref[0])
    h2_ref[0] = h2.astype(BF16)
    lg_ref[0] = _dot3(h2, wr_hi_ref[...], wr_lo_ref[...])


def _post_mixer(o, w_out, x, g1, n2, shift, scale, wr_hi, wr_lo):
    b, t, d = x.shape
    k_in = o.shape[-1]
    tm = min(ROW_TILE, t)
    row = lambda i, j: (i, j, 0)
    per_b = lambda i, j: (i, 0, 0)
    fixed = lambda i, j: (0, 0)
    return pl.pallas_call(
        _post_kernel,
        grid=(b, t // tm),
        in_specs=[pl.BlockSpec((1, tm, k_in), row),
                  pl.BlockSpec((k_in, d), fixed),
                  pl.BlockSpec((1, tm, d), row),
                  pl.BlockSpec((1, 1, d), per_b),
                  pl.BlockSpec((1, d), fixed),
                  pl.BlockSpec((1, 1, d), per_b),
                  pl.BlockSpec((1, 1, d), per_b),
                  pl.BlockSpec((d, LANES), fixed),
                  pl.BlockSpec((d, LANES), fixed)],
        out_specs=[pl.BlockSpec((1, tm, d), row),
                   pl.BlockSpec((1, tm, d), row),
                   pl.BlockSpec((1, tm, LANES), row)],
        out_shape=[jax.ShapeDtypeStruct((b, t, d), F32),
                   jax.ShapeDtypeStruct((b, t, d), BF16),
                   jax.ShapeDtypeStruct((b, t, LANES), F32)],
        compiler_params=_params("arbitrary", "arbitrary"),
        name="mixer_out_proj",
    )(o, w_out, x, g1, n2, shift, scale, wr_hi, wr_lo)


def _exclusive_cumsum_lanes(mask):
    rows, t = mask.shape
    width = min(2 * LANES, t)
    r = lax.broadcasted_iota(I32, (width, width), 0)
    c = lax.broadcasted_iota(I32, (width, width), 1)
    upper = (r < c).astype(BF16)
    carry = jnp.zeros((rows, 1), F32)
    outs = []
    for ch in range(t // width):
        m = mask[:, ch * width:(ch + 1) * width]
        outs.append(_dot(m.astype(BF16), upper) + carry)
        carry = carry + jnp.sum(m, axis=1, keepdims=True)
    return jnp.concatenate(outs, axis=1)


LOG_FLOOR = -160.0
BISECT_LOG_STEPS = 40
BISECT_STEPS = 12


def _affinity_kernel(lg_ref, aff_ref, aff_t_ref):
    t = lg_ref.shape[1]
    lane = lax.broadcasted_iota(I32, (t, LANES), 1)
    lg = jnp.where(lane < N_EXPERTS, lg_ref[0], -jnp.inf)
    e = jnp.exp(lg - jnp.max(lg, axis=-1, keepdims=True))
    aff = e / jnp.sum(e, axis=-1, keepdims=True)
    aff_ref[0] = aff
    aff_t_ref[0] = aff.T[:N_EXPERTS]


def _select_kernel(aff_t_ref, pos_t_ref, starts_ref, *, cap):
    rows = aff_t_ref.shape[0]

    def enough(threshold):
        return jnp.sum((aff_t_ref[...] >= threshold).astype(F32), axis=1, keepdims=True) >= cap

    def bisect_log(_, carry):
        lo_u, hi_u = carry
        mid = lo_u + (hi_u - lo_u) * 0.5
        ok = enough(jnp.exp2(mid))
        return jnp.where(ok, mid, lo_u), jnp.where(ok, hi_u, mid)

    def bisect(_, carry):
        lo, hi = carry
        mid = lo + (hi - lo) * 0.5
        ok = enough(mid)
        return jnp.where(ok, mid, lo), jnp.where(ok, hi, mid)

    lo_u, hi_u = lax.fori_loop(0, BISECT_LOG_STEPS, bisect_log,
                               (jnp.full((rows, 1), LOG_FLOOR, F32), jnp.full((rows, 1), 1.0, F32)))
    lo, hi = lax.fori_loop(0, BISECT_STEPS, bisect, (jnp.exp2(lo_u), jnp.exp2(hi_u)))
    aff_t = aff_t_ref[...]
    above = (aff_t >= hi).astype(F32)
    tied = ((aff_t >= lo) & (aff_t < hi)).astype(F32)
    need = cap - jnp.sum(above, axis=1, keepdims=True)
    sel = above + tied * (_exclusive_cumsum_lanes(tied) < need).astype(F32)
    slot = _exclusive_cumsum_lanes(sel)
    pos_t_ref[...] = jnp.where(sel > 0, slot, -1.0).astype(I32)
    t = aff_t.shape[1]
    tok = lax.broadcasted_iota(I32, (t, LANES), 0)
    edge = lax.broadcasted_iota(I32, (t, LANES), 1) * min(TOKEN_TILE, t)
    before = ((tok < edge) & (edge <= t)).astype(BF16)
    starts_ref[...] = _dot(sel.astype(BF16), before).astype(I32)


def _route(logits, cap):
    b, t, _ = logits.shape
    aff, aff_t = pl.pallas_call(
        _affinity_kernel,
        grid=(b,),
        in_specs=[pl.BlockSpec((1, t, LANES), lambda i: (i, 0, 0))],
        out_specs=[pl.BlockSpec((1, t, LANES), lambda i: (i, 0, 0)),
                   pl.BlockSpec((1, N_EXPERTS, t), lambda i: (i, 0, 0))],
        out_shape=[jax.ShapeDtypeStruct((b, t, LANES), F32),
                   jax.ShapeDtypeStruct((b, N_EXPERTS, t), F32)],
        compiler_params=_params("arbitrary"),
        name="ec_affinity",
    )(logits)
    rows = b * N_EXPERTS
    pos_t, starts = pl.pallas_call(
        functools.partial(_select_kernel, cap=cap),
        grid=(1,),
        in_specs=[pl.BlockSpec((rows, t), lambda i: (0, 0))],
        out_specs=[pl.BlockSpec((rows, t), lambda i: (0, 0)),
                   pl.BlockSpec((rows, LANES), lambda i: (0, 0))],
        out_shape=[jax.ShapeDtypeStruct((rows, t), I32),
                   jax.ShapeDtypeStruct((rows, LANES), I32)],
        compiler_params=_params("arbitrary"),
        name="ec_select",
    )(aff_t.reshape(rows, t))
    return pos_t.reshape(b, N_EXPERTS, t), starts.reshape(b, N_EXPERTS, LANES), aff


def _gather_kernel(pos_t_ref, h_ref, xs_ref, *, cap):
    t = h_ref.shape[1]
    slot = lax.broadcasted_iota(I32, (cap, t), 0)
    onehot = (slot == pos_t_ref[0, 0]).astype(BF16)
    xs_ref[0, 0] = _dot(onehot, h_ref[0]).astype(BF16)


def _gather(pos_t, h, cap):
    b, t, d = h.shape
    return pl.pallas_call(
        functools.partial(_gather_kernel, cap=cap),
        grid=(b, N_EXPERTS),
        in_specs=[pl.BlockSpec((1, 1, 1, t), lambda i, e: (i, e, 0, 0)),
                  pl.BlockSpec((1, t, d), lambda i, e: (i, 0, 0))],
        out_specs=pl.BlockSpec((1, 1, cap, d), lambda i, e: (e, i, 0, 0)),
        out_shape=jax.ShapeDtypeStruct((N_EXPERTS, b, cap, d), BF16),
        compiler_params=_params("arbitrary", "arbitrary"),
        name="ec_gather",
    )(pos_t.reshape(b, N_EXPERTS, 1, t), h)


def _gather_stacked_kernel(pos_t_ref, h_ref, xs_ref, *, cap):
    t = h_ref.shape[1]
    pos_t = pos_t_ref[0]
    slot = lax.broadcasted_iota(I32, (cap, t), 0)
    onehot = jnp.concatenate([(slot == pos_t[e:e + 1, :]).astype(BF16) for e in range(N_EXPERTS)], axis=0)
    z = _dot(onehot, h_ref[0]).astype(BF16)
    for e in range(N_EXPERTS):
        xs_ref[e, 0] = z[e * cap:(e + 1) * cap]


def _gather_stacked(pos_t, h, cap):
    b, t, d = h.shape
    return pl.pallas_call(
        functools.partial(_gather_stacked_kernel, cap=cap),
        grid=(b,),
        in_specs=[pl.BlockSpec((1, N_EXPERTS, t), lambda i: (i, 0, 0)),
                  pl.BlockSpec((1, t, d), lambda i: (i, 0, 0))],
        out_specs=pl.BlockSpec((N_EXPERTS, 1, cap, d), lambda i: (0, i, 0, 0)),
        out_shape=jax.ShapeDtypeStruct((N_EXPERTS, b, cap, d), BF16),
        compiler_params=_params("arbitrary"),
        name="ec_gather_stacked",
    )(pos_t, h)


def _tile_windows(starts_ref, cap, ntp):
    i, j = pl.program_id(0), pl.program_id(1)
    wins, fits = [], None
    for e in range(N_EXPERTS):
        at = (i * N_EXPERTS + e) * ntp + j
        win = jnp.minimum((starts_ref[at] // SLOT_ALIGN) * SLOT_ALIGN, cap - SLOT_WINDOW)
        ok = starts_ref[at + 1] - win <= SLOT_WINDOW
        wins.append(pl.multiple_of(win, SLOT_ALIGN))
        fits = ok if fits is None else jnp.logical_and(fits, ok)
    return wins, fits


def _gather_win_kernel(starts_ref, pos_t_ref, h_ref, xs_ref, *, cap, ntp):
    w = SLOT_WINDOW
    tile = h_ref.shape[1]
    wins, fits = _tile_windows(starts_ref, cap, ntp)
    h = h_ref[0]
    pos_t = pos_t_ref[0]

    @pl.when(pl.program_id(1) == 0)
    def _():
        xs_ref[...] = jnp.zeros_like(xs_ref)

    @pl.when(fits)
    def _():
        row = lax.broadcasted_iota(I32, (w, tile), 0)
        onehot = jnp.concatenate([(row + wins[e] == pos_t[e:e + 1, :]).astype(BF16) for e in range(N_EXPERTS)],
                                 axis=0)
        z = _dot(onehot, h).astype(BF16)
        for e in range(N_EXPERTS):
            xs_ref[e, 0, pl.ds(wins[e], w), :] += z[e * w:(e + 1) * w]

    @pl.when(jnp.logical_not(fits))
    def _():
        slot = lax.broadcasted_iota(I32, (cap, tile), 0)
        for e in range(N_EXPERTS):
            onehot = (slot == pos_t[e:e + 1, :]).astype(BF16)
            xs_ref[e, 0] += _dot(onehot, h).astype(BF16)


def _gather_windowed(pos_t, starts, h, cap):
    b, t, d = h.shape
    tile = min(TOKEN_TILE, t)
    nt = t // tile
    grid_spec = pltpu.PrefetchScalarGridSpec(
        num_scalar_prefetch=1,
        grid=(b, nt),
        in_specs=[pl.BlockSpec((1, N_EXPERTS, tile), lambda i, j, s: (i, 0, j)),
                  pl.BlockSpec((1, tile, d), lambda i, j, s: (i, j, 0))],
        out_specs=pl.BlockSpec((N_EXPERTS, 1, cap, d), lambda i, j, s: (0, i, 0, 0)),
    )
    return pl.pallas_call(
        functools.partial(_gather_win_kernel, cap=cap, ntp=nt + 1),
        grid_spec=grid_spec,
        out_shape=jax.ShapeDtypeStruct((N_EXPERTS, b, cap, d), BF16),
        compiler_params=_params("arbitrary", "arbitrary"),
        name="ec_gather_win",
    )(starts[:, :, :nt + 1].reshape(-1), pos_t, h)


def _ffn_up_kernel(xs_ref, wg_ref, wu_ref, hm_ref):
    xs = xs_ref[0]
    a = _dot(xs, wg_ref[0, 0].astype(BF16))
    u = _dot(xs, wu_ref[0, 0].astype(BF16))
    hm_ref[0] = (_silu(a) * u).astype(hm_ref.dtype)


def _ffn_down_kernel(hm_ref, wd_ref, y_ref, wd_bf16_ref):
    @pl.when(pl.program_id(1) == 0)
    def _():
        wd_bf16_ref[...] = wd_ref[0, 0].astype(BF16)

    y_ref[0] = _dot(hm_ref[0], wd_bf16_ref[...]).astype(y_ref.dtype)


def _expert_ffn(xs, w_gate, w_up, w_down, layer):
    n_exp, rows, d = xs.shape
    ff = w_gate.shape[-1]
    tm = min(MOE_ROW_TILE, rows)
    tf = min(FF_TILE, ff)
    if rows <= FEW_ROWS and ff % FF_TILE_FEW_ROWS == 0:
        tf = FF_TILE_FEW_ROWS
    hm = pl.pallas_call(
        _ffn_up_kernel,
        grid=(n_exp, rows // tm, ff // tf),
        in_specs=[pl.BlockSpec((1, tm, d), lambda e, m, f: (e, m, 0)),
                  pl.BlockSpec((1, 1, d, tf), lambda e, m, f: (layer, e, 0, f)),
                  pl.BlockSpec((1, 1, d, tf), lambda e, m, f: (layer, e, 0, f))],
        out_specs=pl.BlockSpec((1, tm, tf), lambda e, m, f: (e, m, f)),
        out_shape=jax.ShapeDtypeStruct((n_exp, rows, ff), BF16),
        compiler_params=_params("arbitrary", "arbitrary", "arbitrary"),
        name="ec_ffn_up",
    )(xs, w_gate, w_up)
    tm2 = min(MOE_DOWN_ROW_TILE, rows)
    return pl.pallas_call(
        _ffn_down_kernel,
        grid=(n_exp, rows // tm2),
        in_specs=[pl.BlockSpec((1, tm2, ff), lambda e, m: (e, m, 0)),
                  pl.BlockSpec((1, 1, ff, d), lambda e, m: (layer, e, 0, 0))],
        out_specs=pl.BlockSpec((1, tm2, d), lambda e, m: (e, m, 0)),
        out_shape=jax.ShapeDtypeStruct((n_exp, rows, d), BF16),
        scratch_shapes=[pltpu.VMEM((ff, d), BF16)],
        compiler_params=_params("arbitrary", "arbitrary"),
        name="ec_ffn_down",
    )(hm, w_down)


def _combine_kernel(x_ref, g2_ref, pos_t_ref, aff_ref, y_ref, fg_ref, o_ref, *, cap, final_norm):
    tt = x_ref.shape[1]
    pos = _token_major(pos_t_ref)
    aff = aff_ref[0]
    lane = lax.broadcasted_iota(I32, (tt, cap), 1)
    acc = jnp.zeros(x_ref.shape[1:], F32)
    for e in range(N_EXPERTS):
        onehot = (lane == pos[:, e:e + 1]).astype(BF16)
        acc = acc + aff[:, e:e + 1] * _dot(onehot, y_ref[e, 0])
    _finish_combine(x_ref, g2_ref, fg_ref, o_ref, acc, final_norm)


def _combine(x1, g2, pos_t, aff, y, final_g, cap, final_norm):
    b, t, d = x1.shape
    tt = min(TOKEN_TILE, t)
    row = lambda i, j: (i, j, 0)
    return pl.pallas_call(
        functools.partial(_combine_kernel, cap=cap, final_norm=final_norm),
        grid=(b, t // tt),
        in_specs=[pl.BlockSpec((1, tt, d), row),
                  pl.BlockSpec((1, 1, d), lambda i, j: (i, 0, 0)),
                  pl.BlockSpec((1, N_EXPERTS, tt), lambda i, j: (i, 0, j)),
                  pl.BlockSpec((1, tt, LANES), row),
                  pl.BlockSpec((N_EXPERTS, 1, cap, d), lambda i, j: (0, i, 0, 0)),
                  pl.BlockSpec((1, d), lambda i, j: (0, 0))],
        out_specs=pl.BlockSpec((1, tt, d), row),
        out_shape=jax.ShapeDtypeStruct((b, t, d), F32),
        compiler_params=_params("arbitrary", "arbitrary"),
        name="ec_combine",
    )(x1, g2, pos_t, aff, y, final_g)


def _token_major(pos_t_ref):
    pos_t = pos_t_ref[0].astype(F32)
    pad = jnp.full((LANES - N_EXPERTS, pos_t.shape[1]), -1.0, F32)
    return jnp.concatenate([pos_t, pad], axis=0).T.astype(I32)


def _finish_combine(x_ref, g2_ref, fg_ref, o_ref, acc, final_norm):
    x2 = x_ref[0] + g2_ref[0] * acc
    if final_norm:
        x2 = (x2 * lax.rsqrt(jnp.mean(x2 * x2, axis=-1, keepdims=True) + NORM_EPS)) * fg_ref[...]
    o_ref[0] = x2


def _combine_win_kernel(starts_ref, x_ref, g2_ref, pos_t_ref, aff_ref, y_ref, fg_ref, o_ref,
                        *, cap, ntp, final_norm):
    w = SLOT_WINDOW
    assert w & (w - 1) == 0, "window column index is taken with a bit mask"
    tile = x_ref.shape[1]
    wins, fits = _tile_windows(starts_ref, cap, ntp)
    pos = _token_major(pos_t_ref)
    aff = aff_ref[0]

    @pl.when(fits)
    def _():
        ywin = jnp.concatenate([y_ref[e, 0, pl.ds(wins[e], w), :] for e in range(N_EXPERTS)], axis=0)
        lane = lax.broadcasted_iota(I32, (1, LANES), 1)
        win_row = jnp.zeros((1, LANES), I32)
        for e in range(N_EXPERTS):
            win_row = jnp.where(lane == e, wins[e], win_row)
        rel = jnp.where(pos >= 0, jnp.clip(pos - win_row, -1, w), -1)
        src = lax.broadcasted_iota(I32, (LANES, N_EXPERTS * w), 0) * w
        col = lax.broadcasted_iota(I32, (LANES, N_EXPERTS * w), 1)
        spread = ((col >= src) & (col < src + w)).astype(BF16)
        wide = _dot(jnp.concatenate([rel.astype(F32).astype(BF16), aff.astype(BF16)], axis=0), spread)
        in_win = lax.broadcasted_iota(I32, (1, N_EXPERTS * w), 1) & (w - 1)
        gates = jnp.where(wide[:tile] == in_win.astype(F32), wide[tile:], 0.0).astype(BF16)
        _finish_combine(x_ref, g2_ref, fg_ref, o_ref, _dot(gates, ywin), final_norm)

    @pl.when(jnp.logical_not(fits))
    def _():
        lane = lax.broadcasted_iota(I32, (tile, cap), 1)
        acc = jnp.zeros(x_ref.shape[1:], F32)
        for e in range(N_EXPERTS):
            onehot = (lane == pos[:, e:e + 1]).astype(BF16)
            acc = acc + aff[:, e:e + 1] * _dot(onehot, y_ref[e, 0])
        _finish_combine(x_ref, g2_ref, fg_ref, o_ref, acc, final_norm)


def _combine_windowed(x1, g2, pos_t, starts, aff, y, final_g, cap, final_norm):
    b, t, d = x1.shape
    tile = min(TOKEN_TILE, t)
    nt = t // tile
    row = lambda i, j, s: (i, j, 0)
    grid_spec = pltpu.PrefetchScalarGridSpec(
        num_scalar_prefetch=1,
        grid=(b, nt),
        in_specs=[pl.BlockSpec((1, tile, d), row),
                  pl.BlockSpec((1, 1, d), lambda i, j, s: (i, 0, 0)),
                  pl.BlockSpec((1, N_EXPERTS, tile), lambda i, j, s: (i, 0, j)),
                  pl.BlockSpec((1, tile, LANES), row),
                  pl.BlockSpec((N_EXPERTS, 1, cap, d), lambda i, j, s: (0, i, 0, 0)),
                  pl.BlockSpec((1, d), lambda i, j, s: (0, 0))],
        out_specs=pl.BlockSpec((1, tile, d), row),
    )
    return pl.pallas_call(
        functools.partial(_combine_win_kernel, cap=cap, ntp=nt + 1, final_norm=final_norm),
        grid_spec=grid_spec,
        out_shape=jax.ShapeDtypeStruct((b, t, d), F32),
        compiler_params=_params("arbitrary", "arbitrary"),
        name="ec_combine_win",
    )(starts[:, :, :nt + 1].reshape(-1), x1, g2, pos_t, aff, y, final_g)


def _ec_moe(x1, h2, logits, g2, w_gate, w_up, w_down, layer, final_g, final_norm):
    b, t, d = x1.shape
    cap = CAPACITY_FACTOR * t // N_EXPERTS
    pos_t, starts, aff = _route(logits, cap)
    windowed = t > TOKEN_TILE and cap >= SLOT_WINDOW and (cap - SLOT_WINDOW) % SLOT_ALIGN == 0
    if windowed:
        xs = _gather_windowed(pos_t, starts, h2, cap)
    elif t <= TOKEN_TILE and cap % SLOT_ALIGN == 0:
        xs = _gather_stacked(pos_t, h2, cap)
    else:
        xs = _gather(pos_t, h2, cap)
    y = _expert_ffn(xs.reshape(N_EXPERTS, b * cap, d), w_gate, w_up, w_down, layer)
    y = y.reshape(N_EXPERTS, b, cap, d)
    if windowed:
        return _combine_windowed(x1, g2, pos_t, starts, aff, y, final_g, cap, final_norm)
    return _combine(x1, g2, pos_t, aff, y, final_g, cap, final_norm)


def _pre_gla_kernel(x_ref, g_ref, sh_ref, sc_ref, w_ref, wlr_ref, w2_ref, gb_ref,
                    q_ref, r_ref, k_ref, v_ref, lgf_ref, lgb_ref):
    qk = q_ref.shape[-1]
    vd = v_ref.shape[-1]
    h = _norm_mod(x_ref[0], g_ref[...], sh_ref[0], sc_ref[0]).astype(BF16)
    p = _dot(h, w_ref[...])
    q_ref[0] = (p[:, :qk] * (qk // GLA_HEADS) ** -0.5).astype(BF16)
    r_ref[0] = _silu(p[:, qk:qk + vd]).astype(BF16)
    k_ref[0] = p[:, qk + vd:2 * qk + vd].astype(BF16)
    v_ref[0] = p[:, 2 * qk + vd:].astype(BF16)
    lr = _dot(h, wlr_ref[...]).astype(BF16)
    z = _dot(lr, w2_ref[...]) + gb_ref[...]
    lg = (jnp.minimum(z, 0.0) - jnp.log1p(jnp.exp(-jnp.abs(z)))) * (1.0 / GLA_TAU)
    lgf_ref[0] = lg[:, :qk]
    lgb_ref[0] = lg[:, qk:]


def _pre_gla(x, g, shift, scale, w_main, w_lr, w2, gate_b):
    b, t, d = x.shape
    n_main = w_main.shape[1]
    qk = w2.shape[1] // 2
    vd = (n_main - 2 * qk) // 2
    tm = min(ROW_TILE, t)
    row = lambda i, j: (i, j, 0)
    per_b = lambda i, j: (i, 0, 0)
    fixed = lambda i, j: (0, 0)
    out = lambda n, dt: (pl.BlockSpec((1, tm, n), row), jax.ShapeDtypeStruct((b, t, n), dt))
    outs = [out(qk, BF16), out(vd, BF16), out(qk, BF16), out(vd, BF16), out(qk, F32), out(qk, F32)]
    return pl.pallas_call(
        _pre_gla_kernel,
        grid=(b, t // tm),
        in_specs=[pl.BlockSpec((1, tm, d), row),
                  pl.BlockSpec((1, d), fixed),
                  pl.BlockSpec((1, 1, d), per_b),
                  pl.BlockSpec((1, 1, d), per_b),
                  pl.BlockSpec((d, n_main), fixed),
                  pl.BlockSpec((d, LANES), fixed),
                  pl.BlockSpec((LANES, 2 * qk), fixed),
                  pl.BlockSpec((1, 2 * qk), fixed)],
        out_specs=[o[0] for o in outs],
        out_shape=[o[1] for o in outs],
        compiler_params=_params("arbitrary", "arbitrary"),
        name="gla_in_proj",
    )(x, g, shift, scale, w_main, w_lr, w2, gate_b)


def _gla_masks(rows):
    ri = lax.broadcasted_iota(I32, (rows, rows), 0)
    ci = lax.broadcasted_iota(I32, (rows, rows), 1)
    same = (ri // GLA_CHUNK) == (ci // GLA_CHUNK)
    return same & (ri >= ci), same & (ri <= ci)


def _gla_tile(q, k, v, lg, st_ref, mask, forward):
    ch = GLA_CHUNK
    rows, dk = k.shape
    n_chunks = rows // ch
    m = mask.astype(BF16)
    lg_hi, lg_lo = _split2(lg)
    b = _dot(m, lg_hi) + _dot(m, lg_lo)
    last = [c * ch + (ch - 1 if forward else 0) for c in range(n_chunks)]
    b_last = jnp.concatenate([jnp.broadcast_to(b[r:r + 1, :], (ch, dk)) for r in last], axis=0)
    kf = k.astype(F32)
    kd = (kf * jnp.exp(b_last - b)).astype(BF16)
    if q is not None:
        qt = (q.astype(F32) * jnp.exp(b)).astype(BF16)
        kt = (kf * jnp.exp(-b)).astype(BF16)
        a = lax.dot_general(qt, kt, NT_DIMS, preferred_element_type=F32)
        o_local = _dot(jnp.where(mask, a, 0.0).astype(BF16), v)
    st = st_ref[...]
    outs = [None] * n_chunks
    for c in (range(n_chunks) if forward else reversed(range(n_chunks))):
        rs = slice(c * ch, (c + 1) * ch)
        if q is not None:
            outs[c] = o_local[rs] + lax.dot_general(qt[rs], st.astype(BF16), NT_DIMS, preferred_element_type=F32)
        decay = jnp.exp(b[last[c]:last[c] + 1, :])
        st = decay * st + lax.dot_general(v[rs], kd[rs], TN_DIMS, preferred_element_type=F32)
    st_ref[...] = st
    return None if q is None else jnp.concatenate(outs, axis=0)


def _gla_kernel(q_ref, k_ref, v_ref, lgf_ref, lgb_ref, kc_ref, vc_ref, lgcf_ref, lgcb_ref, r_ref, ng_ref,
                o_ref, of_ref, ob_ref, stf_ref, stb_ref):
    t = q_ref.shape[1]
    t_ctx = kc_ref.shape[1]
    n_heads = stf_ref.shape[0]
    dv, dk = stf_ref.shape[1:]
    tile = min(GLA_TILE, t)
    ctile = min(GLA_TILE, t_ctx)
    mask_f, mask_b = _gla_masks(tile)
    cmask_f, cmask_b = (mask_f, mask_b) if ctile == tile else _gla_masks(ctile)
    stf_ref[...] = jnp.zeros_like(stf_ref)
    stb_ref[...] = jnp.zeros_like(stb_ref)
    kcols = [slice(h * dk, (h + 1) * dk) for h in range(n_heads)]
    vcols = [slice(h * dv, (h + 1) * dv) for h in range(n_heads)]

    n_ctx = t_ctx // ctile
    for i in range(n_ctx):
        rf = slice(i * ctile, (i + 1) * ctile)
        rb = slice((n_ctx - 1 - i) * ctile, (n_ctx - i) * ctile)
        for h in range(n_heads):
            _gla_tile(None, kc_ref[0, rf, kcols[h]], vc_ref[0, rf, vcols[h]], lgcf_ref[0, rf, kcols[h]],
                      stf_ref.at[h], cmask_f, True)
            _gla_tile(None, kc_ref[0, rb, kcols[h]], vc_ref[0, rb, vcols[h]], lgcb_ref[0, rb, kcols[h]],
                      stb_ref.at[h], cmask_b, False)

    n = t // tile

    def body(i, carry):
        rf = pl.ds(pl.multiple_of(i * tile, tile), tile)
        rb = pl.ds(pl.multiple_of((n - 1 - i) * tile, tile), tile)
        for h in range(n_heads):
            of_ref[rf, vcols[h]] = _gla_tile(q_ref[0, rf, kcols[h]], k_ref[0, rf, kcols[h]], v_ref[0, rf, vcols[h]],
                                             lgf_ref[0, rf, kcols[h]], stf_ref.at[h], mask_f, True)
            ob_ref[rb, vcols[h]] = _gla_tile(q_ref[0, rb, kcols[h]], k_ref[0, rb, kcols[h]], v_ref[0, rb, vcols[h]],
                                             lgb_ref[0, rb, kcols[h]], stb_ref.at[h], mask_b, False)
        return carry

    lax.fori_loop(0, n, body, 0)

    def readout(i, carry):
        rows = pl.ds(pl.multiple_of(i * tile, tile), tile)
        for h in range(n_heads):
            o = of_ref[rows, vcols[h]] + ob_ref[rows, vcols[h]]
            o = o * lax.rsqrt(jnp.mean(o * o, axis=-1, keepdims=True) + NORM_EPS) * ng_ref[:, vcols[h]]
            o_ref[0, rows, vcols[h]] = (o * r_ref[0, rows, vcols[h]].astype(F32)).astype(o_ref.dtype)
        return carry

    lax.fori_loop(0, n, readout, 0)


def _gla(q, k, v, lgf, lgb, kc, vc, lgcf, lgcb, r, norm_g):
    b, t, qk = q.shape
    vd = v.shape[-1]
    t_ctx = kc.shape[1]
    dk, dv = qk // GLA_HEADS, vd // GLA_HEADS
    hp = GLA_HEADS_PER_STEP
    head = lambda i, h: (i, 0, h)
    lat = lambda n: pl.BlockSpec((1, t, hp * n), head)
    ctx = lambda n: pl.BlockSpec((1, t_ctx, hp * n), head)
    return pl.pallas_call(
        _gla_kernel,
        grid=(b, GLA_HEADS // hp),
        in_specs=[lat(dk), lat(dk), lat(dv), lat(dk), lat(dk),
                  ctx(dk), ctx(dv), ctx(dk), ctx(dk),
                  lat(dv),
                  pl.BlockSpec((1, hp * dv), lambda i, h: (0, h))],
        out_specs=lat(dv),
        out_shape=jax.ShapeDtypeStruct((b, t, vd), BF16),
        scratch_shapes=[pltpu.VMEM((t, hp * dv), F32), pltpu.VMEM((t, hp * dv), F32),
                        pltpu.VMEM((hp, dv, dk), F32), pltpu.VMEM((hp, dv, dk), F32)],
        compiler_params=_params("arbitrary", "arbitrary"),
        name="gla_scan",
    )(q, k, v, lgf, lgb, kc, vc, lgcf, lgcb, r, norm_g)


def _rope_tables(t):
    pos = jnp.arange(t)
    row = (pos // GRID_W).astype(F32)
    col = (pos % GRID_W).astype(F32)
    inv = ROPE_BASE ** (-jnp.arange(ROPE_PAIRS, dtype=F32) / ROPE_PAIRS)
    ar, ac = row[:, None] * inv, col[:, None] * inv
    ang = jnp.concatenate([ar, ar, ac, ac], axis=-1)
    sign = jnp.tile(jnp.repeat(jnp.array([-1.0, 1.0], F32), ROPE_PAIRS), 2)
    reps = LANES // HEAD_DIM
    return jnp.tile(jnp.cos(ang), (1, reps)), jnp.tile(jnp.sin(ang) * sign, (1, reps))


def _router_split(w):
    d, n = w.shape
    w = jnp.pad(w, ((0, 0), (0, LANES - n)))
    hi = w.astype(BF16)
    return hi, (w - hi.astype(F32)).astype(BF16)


def kernel(x, c, ctx, c_ctx, ada_w, ada_b, norm1_g, norm2_g, attn_w_in, attn_w_out, attn_sink, gla_w_in, gla_gate_w2, gla_gate_b, gla_norm_g, gla_w_out, router_w, exp_w_gate, exp_w_up, exp_w_down, final_norm_g):
    b, t, d = x.shape
    depth = ada_w.shape[0]
    assert depth == 2, "layer 0 is windowed attention with context output, layer 1 is GLA and last"
    rows = -(-(b + 1) // 16) * 16
    src = jnp.concatenate([c, c_ctx[None, :], jnp.zeros((rows - b - 1, d), F32)], axis=0)
    cos, sin_signed = _rope_tables(t)
    final_g = final_norm_g.reshape(1, d)

    def chunks(i):
        mod = _modulation(src, ada_w, ada_b, i)
        lat = [mod[:b, j * d:(j + 1) * d].reshape(b, 1, d) for j in range(6)]
        cx = [jnp.broadcast_to(mod[b, j * d:(j + 1) * d].reshape(1, 1, d), (b, 1, d)) for j in range(6)]
        return lat, cx

    (sh1, sc1, g1, sh2, sc2, g2), (sh1c, sc1c, g1c, sh2c, sc2c, g2c) = chunks(0)
    n1, n2 = norm1_g[0].reshape(1, d), norm2_g[0].reshape(1, d)
    w_in = attn_w_in[0].astype(BF16)
    w_out = attn_w_out[0].astype(BF16)
    sink = attn_sink[0].reshape(1, -1)
    wr_hi, wr_lo = _router_split(router_w[0])
    q, k, v = _pre_attn(x, n1, sh1, sc1, w_in, cos, sin_signed, rope=True)
    t_ctx = ctx.shape[1]
    qc, kc, vc = _pre_attn(ctx, n1, sh1c, sc1c, w_in, cos[:t_ctx], sin_signed[:t_ctx], rope=False)
    o = _attn_band(q, k, v, kc, vc, sink)
    oc = _attn_ctx(qc, kc, vc, sink)
    x1, h2, logits = _post_mixer(o, w_out, x, g1, n2, sh2, sc2, wr_hi, wr_lo)
    xc1, hc2, logits_c = _post_mixer(oc, w_out, ctx, g1c, n2, sh2c, sc2c, wr_hi, wr_lo)
    x = _ec_moe(x1, h2, logits, g2, exp_w_gate, exp_w_up, exp_w_down, 0, final_g, False)
    xc = _ec_moe(xc1, hc2, logits_c, g2c, exp_w_gate, exp_w_up, exp_w_down, 0, final_g, False)

    (sh1, sc1, g1, sh2, sc2, g2), (sh1c, sc1c, _, _, _, _) = chunks(1)
    n1, n2 = norm1_g[1].reshape(1, d), norm2_g[1].reshape(1, d)
    qk = gla_gate_w2.shape[-1]
    n_main = gla_w_in.shape[-1] - 2 * GLA_RANK
    w_main = gla_w_in[0][:, :n_main].astype(BF16)
    w_lr = jnp.pad(gla_w_in[0][:, n_main:], ((0, 0), (0, LANES - 2 * GLA_RANK))).astype(BF16)
    w2 = jnp.zeros((LANES, 2 * qk), F32)
    w2 = w2.at[:GLA_RANK, :qk].set(gla_gate_w2[0, 0]).at[GLA_RANK:2 * GLA_RANK, qk:].set(gla_gate_w2[0, 1])
    w2 = w2.astype(BF16)
    gate_b = gla_gate_b[0].reshape(1, 2 * qk)
    gq, gr, gk, gv, lgf, lgb = _pre_gla(x, n1, sh1, sc1, w_main, w_lr, w2, gate_b)
    _, _, gkc, gvc, lgcf, lgcb = _pre_gla(xc, n1, sh1c, sc1c, w_main, w_lr, w2, gate_b)
    og = _gla(gq, gk, gv, lgf, lgb, gkc, gvc, lgcf, lgcb, gr, gla_norm_g[0].reshape(1, -1))
    wr_hi, wr_lo = _router_split(router_w[1])
    x1, h2, logits = _post_mixer(og, gla_w_out[0].astype(BF16), x, g1, n2, sh2, sc2, wr_hi, wr_lo)
    return _ec_moe(x1, h2, logits, g2, exp_w_gate, exp_w_up, exp_w_down, 1, final_g, True)
```

```python
import functools

import jax
import jax.numpy as jnp
from jax import lax
from jax.experimental import pallas as pl
from jax.experimental.pallas import tpu as pltpu

F32 = jnp.float32
BF16 = jnp.bfloat16
I32 = jnp.int32

LANES = 128
HEAD_DIM = 64
ATT_GROUP = 4
ATT_BLOCK = 128
GRID_W = 64
ROPE_BASE = 10000.0
ROPE_PAIRS = HEAD_DIM // 4
GLA_HEADS = 4
GLA_RANK = 16
GLA_TAU = 16.0
GLA_CHUNK = 64
N_EXPERTS = 16
CAPACITY_FACTOR = 2
NORM_EPS = 1e-6
LOG2E = 1.4426950408889634
ROW_TILE = 256
ATTN_IN_ROW_TILE = 512
TOKEN_TILE = 256
MOE_ROW_TILE = 4096
MOE_DOWN_ROW_TILE = 512
FF_TILE = 256
FEW_ROWS = 512
FF_TILE_FEW_ROWS = 1408
GLA_TILE = 256
GLA_HEADS_PER_STEP = 2
SLOT_WINDOW = 64
SLOT_ALIGN = 16
VMEM_LIMIT = 56 * 1024 * 1024

NT_DIMS = (((1,), (1,)), ((), ()))
TN_DIMS = (((0,), (0,)), ((), ()))


def _params(*sem):
    return pltpu.CompilerParams(dimension_semantics=sem, vmem_limit_bytes=VMEM_LIMIT)


def _dot(a, b):
    return jnp.dot(a, b, preferred_element_type=F32)


def _split2(a):
    hi = a.astype(BF16)
    lo = (a - hi.astype(F32)).astype(BF16)
    return hi, lo


def _dot3(a, b_hi, b_lo):
    a_hi, a_lo = _split2(a)
    return _dot(a_hi, b_hi) + _dot(a_hi, b_lo) + _dot(a_lo, b_hi)


def _silu(a):
    return a * jax.nn.sigmoid(a)


def _norm_mod(x, g, shift, scale):
    y = x * lax.rsqrt(jnp.mean(x * x, axis=-1, keepdims=True) + NORM_EPS)
    return (y * g) * (1.0 + scale) + shift


def _mod_kernel(src_ref, w_ref, b_ref, o_ref):
    w_hi, w_lo = _split2(w_ref[0])
    o_ref[...] = _dot3(_silu(src_ref[...]), w_hi, w_lo) + b_ref[0]


def _modulation(src, ada_w, ada_b, layer):
    rows, d = src.shape
    depth, _, n = ada_w.shape
    tn = 512
    return pl.pallas_call(
        _mod_kernel,
        grid=(n // tn,),
        in_specs=[pl.BlockSpec((rows, d), lambda j: (0, 0)),
                  pl.BlockSpec((1, d, tn), lambda j: (layer, 0, j)),
                  pl.BlockSpec((1, 1, tn), lambda j: (layer, 0, j))],
        out_specs=pl.BlockSpec((rows, tn), lambda j: (0, j)),
        out_shape=jax.ShapeDtypeStruct((rows, n), F32),
        compiler_params=_params("arbitrary"),
        name="adaln_mod",
    )(src, ada_w, ada_b.reshape(depth, 1, n))


def _rope(x, cos, sin_signed, lane_lo):
    outs = []
    for j in range(x.shape[1] // LANES):
        xb = x[:, j * LANES:(j + 1) * LANES]
        partner = jnp.where(lane_lo, pltpu.roll(xb, LANES - ROPE_PAIRS, 1), pltpu.roll(xb, ROPE_PAIRS, 1))
        outs.append(xb * cos + partner * sin_signed)
    return jnp.concatenate(outs, axis=1)


def _pre_attn_kernel(x_ref, g_ref, sh_ref, sc_ref, w_ref, cos_ref, sin_ref, q_ref, k_ref, v_ref, *, rope):
    d = x_ref.shape[-1]
    kv = k_ref.shape[-1]
    h = _norm_mod(x_ref[0], g_ref[...], sh_ref[0], sc_ref[0]).astype(BF16)
    p = _dot(h, w_ref[...])
    q, k, v = p[:, :d], p[:, d:d + kv], p[:, d + kv:]
    if rope:
        lane = lax.broadcasted_iota(I32, (x_ref.shape[1], LANES), 1)
        lane_lo = (lane & (2 * ROPE_PAIRS - 1)) < ROPE_PAIRS
        cos, sin_signed = cos_ref[...], sin_ref[...]
        q = _rope(q, cos, sin_signed, lane_lo)
        k = _rope(k, cos, sin_signed, lane_lo)
    q_ref[0] = (q * (HEAD_DIM ** -0.5 * LOG2E)).astype(BF16)
    k_ref[0] = k.astype(BF16)
    v_ref[0] = v.astype(BF16)


def _pre_attn(x, g, shift, scale, w_in, cos, sin_signed, rope):
    b, t, d = x.shape
    n_in = w_in.shape[1]
    kv = (n_in - d) // 2
    tm = min(ATTN_IN_ROW_TILE, t)
    row = lambda i, j: (i, j, 0)
    per_b = lambda i, j: (i, 0, 0)
    fixed = lambda i, j: (0, 0)
    return pl.pallas_call(
        functools.partial(_pre_attn_kernel, rope=rope),
        grid=(b, t // tm),
        in_specs=[pl.BlockSpec((1, tm, d), row),
                  pl.BlockSpec((1, d), fixed),
                  pl.BlockSpec((1, 1, d), per_b),
                  pl.BlockSpec((1, 1, d), per_b),
                  pl.BlockSpec((d, n_in), fixed),
                  pl.BlockSpec((tm, LANES), lambda i, j: (j, 0)),
                  pl.BlockSpec((tm, LANES), lambda i, j: (j, 0))],
        out_specs=[pl.BlockSpec((1, tm, d), row),
                   pl.BlockSpec((1, tm, kv), row),
                   pl.BlockSpec((1, tm, kv), row)],
        out_shape=[jax.ShapeDtypeStruct((b, t, d), BF16),
                   jax.ShapeDtypeStruct((b, t, kv), BF16),
                   jax.ShapeDtypeStruct((b, t, kv), BF16)],
        compiler_params=_params("arbitrary", "arbitrary"),
        name="attn_in_proj",
    )(x, g, shift, scale, w_in, cos, sin_signed)


def _attn_heads(q, k_parts, v_parts, bias_parts, sink_ref, o_ref):
    tq = q.shape[0]
    n_kv = k_parts[0].shape[1] // HEAD_DIM
    lane = lax.broadcasted_iota(I32, (1, ATT_GROUP * tq), 1)
    v_lane = lax.broadcasted_iota(I32, (1, LANES), 1)
    sks, scores, ms, es, both = ([None] * n_kv for _ in range(5))

    def score_stage(h):
        heads = [ATT_GROUP * h + g for g in range(ATT_GROUP)]
        qg = jnp.concatenate([q[:, i * HEAD_DIM:(i + 1) * HEAD_DIM] for i in heads], axis=0)
        sk = jnp.full((1, ATT_GROUP * tq), sink_ref[0, heads[0]], F32)
        for g in range(1, ATT_GROUP):
            sk = jnp.where(lane >= g * tq, sink_ref[0, heads[g]], sk)
        sks[h] = sk * LOG2E
        m = sks[h]
        scores[h] = []
        for kp, bias in zip(k_parts, bias_parts):
            s = lax.dot_general(kp[:, h * HEAD_DIM:(h + 1) * HEAD_DIM], qg, NT_DIMS, preferred_element_type=F32)
            s = s if bias is None else s + bias
            m = jnp.maximum(m, jnp.max(s, axis=0, keepdims=True))
            scores[h].append(s)
        ms[h] = m

    def weight_stage(h):
        es[h] = jnp.concatenate([jnp.exp2(s - ms[h]).astype(BF16) for s in scores[h]], axis=0)

    def value_stage(h):
        block = (h * HEAD_DIM // LANES) * LANES
        upper = (h * HEAD_DIM) % LANES != 0
        v_pair = jnp.concatenate([vp[:, block:block + LANES] for vp in v_parts], axis=0)
        own = (v_lane >= HEAD_DIM) if upper else (v_lane < HEAD_DIM)
        v_aug = jnp.where(own, v_pair, jnp.ones_like(v_pair))
        both[h] = lax.dot_general(v_aug, es[h], TN_DIMS, preferred_element_type=F32)

    stages = (score_stage, weight_stage, value_stage)
    for step in range(n_kv + len(stages) - 1):
        for depth, stage in enumerate(stages):
            if 0 <= step - depth < n_kv:
                stage(step - depth)
    outs = []
    for h in range(n_kv):
        upper = (h * HEAD_DIM) % LANES != 0
        num = both[h][HEAD_DIM:] if upper else both[h][:HEAD_DIM]
        total = both[h][:1] if upper else both[h][HEAD_DIM:HEAD_DIM + 1]
        o_t = num / (total + jnp.exp2(sks[h] - ms[h]))
        outs.extend(o_t[:, g * tq:(g + 1) * tq] for g in range(ATT_GROUP))
    o_ref[0] = jnp.concatenate(outs, axis=0).T.astype(o_ref.dtype)


def _band_bias():
    blk = ATT_BLOCK
    key = jnp.arange(blk)[:, None]
    qry = jnp.arange(ATT_GROUP * blk)[None, :] % blk
    prev_ok = key >= qry
    next_ok = key <= qry
    never = jnp.zeros_like(prev_ok)
    variants = [jnp.stack([never if first else prev_ok, never if last else next_ok])
                for last in (False, True) for first in (False, True)]
    return jnp.where(jnp.stack(variants), 0.0, -jnp.inf).astype(F32)


def _attn_band_kernel(q_ref, kp_ref, ko_ref, kn_ref, vp_ref, vo_ref, vn_ref, kc_ref, vc_ref, bias_ref, sink_ref,
                      o_ref):
    _attn_heads(q_ref[0], [kc_ref[0], kp_ref[0], ko_ref[0], kn_ref[0]], [vc_ref[0], vp_ref[0], vo_ref[0], vn_ref[0]],
                [None, bias_ref[0, 0], None, bias_ref[0, 1]], sink_ref, o_ref)


def _attn_ctx_kernel(q_ref, kc_ref, vc_ref, sink_ref, o_ref):
    _attn_heads(q_ref[0], [kc_ref[0]], [vc_ref[0]], [None], sink_ref, o_ref)


def _attn_band(q, k, v, kc, vc, sink):
    b, t, d = q.shape
    kv = k.shape[-1]
    n_ctx = kc.shape[1]
    nb = t // ATT_BLOCK
    blk = lambda m: pl.BlockSpec((1, ATT_BLOCK, kv), m)
    prev = lambda i, j: (i, jnp.maximum(j - 1, 0), 0)
    own = lambda i, j: (i, j, 0)
    nxt = lambda i, j: (i, jnp.minimum(j + 1, nb - 1), 0)
    ctx_spec = pl.BlockSpec((1, n_ctx, kv), lambda i, j: (i, 0, 0))
    bias = _band_bias()
    edge = lambda i, j: ((j == 0).astype(I32) + 2 * (j == nb - 1).astype(I32), 0, 0, 0)
    return pl.pallas_call(
        _attn_band_kernel,
        grid=(b, nb),
        in_specs=[pl.BlockSpec((1, ATT_BLOCK, d), own),
                  blk(prev), blk(own), blk(nxt), blk(prev), blk(own), blk(nxt),
                  ctx_spec, ctx_spec,
                  pl.BlockSpec((1,) + bias.shape[1:], edge),
                  pl.BlockSpec(memory_space=pltpu.SMEM)],
        out_specs=pl.BlockSpec((1, ATT_BLOCK, d), own),
        out_shape=jax.ShapeDtypeStruct((b, t, d), BF16),
        compiler_params=_params("arbitrary", "arbitrary"),
        name="attn_band",
    )(q, k, k, k, v, v, v, kc, vc, bias, sink)


def _attn_ctx(qc, kc, vc, sink):
    b, n_ctx, d = qc.shape
    kv = kc.shape[-1]
    return pl.pallas_call(
        _attn_ctx_kernel,
        grid=(b,),
        in_specs=[pl.BlockSpec((1, n_ctx, d), lambda i: (i, 0, 0)),
                  pl.BlockSpec((1, n_ctx, kv), lambda i: (i, 0, 0)),
                  pl.BlockSpec((1, n_ctx, kv), lambda i: (i, 0, 0)),
                  pl.BlockSpec(memory_space=pltpu.SMEM)],
        out_specs=pl.BlockSpec((1, n_ctx, d), lambda i: (i, 0, 0)),
        out_shape=jax.ShapeDtypeStruct((b, n_ctx, d), BF16),
        compiler_params=_params("arbitrary"),
        name="attn_ctx",
    )(qc, kc, vc, sink)


def _post_kernel(o_ref, w_ref, x_ref, g1_ref, n2_ref, sh_ref, sc_ref, wr_hi_ref, wr_lo_ref,
                 x1_ref, h2_ref, lg_ref):
    y = _dot(o_ref[0], w_ref[...])
    x1 = x_ref[0] + g1_ref[0] * y
    x1_ref[0] = x1
    h2 = _norm_mod(x1, n2_ref[...], sh_ref[0], sc_ref[0])
    h2_ref[0] = h2.astype(BF16)
    lg_ref[0] = _dot3(h2, wr_hi_ref[...], wr_lo_ref[...])


def _post_mixer(o, w_out, x, g1, n2, shift, scale, wr_hi, wr_lo):
    b, t, d = x.shape
    k_in = o.shape[-1]
    tm = min(ROW_TILE, t)
    row = lambda i, j: (i, j, 0)
    per_b = lambda i, j: (i, 0, 0)
    fixed = lambda i, j: (0, 0)
    return pl.pallas_call(
        _post_kernel,
        grid=(b, t // tm),
        in_specs=[pl.BlockSpec((1, tm, k_in), row),
                  pl.BlockSpec((k_in, d), fixed),
                  pl.BlockSpec((1, tm, d), row),
                  pl.BlockSpec((1, 1, d), per_b),
                  pl.BlockSpec((1, d), fixed),
                  pl.BlockSpec((1, 1, d), per_b),
                  pl.BlockSpec((1, 1, d), per_b),
                  pl.BlockSpec((d, LANES), fixed),
                  pl.BlockSpec((d, LANES), fixed)],
        out_specs=[pl.BlockSpec((1, tm, d), row),
                   pl.BlockSpec((1, tm, d), row),
                   pl.BlockSpec((1, tm, LANES), row)],
        out_shape=[jax.ShapeDtypeStruct((b, t, d), F32),
                   jax.ShapeDtypeStruct((b, t, d), BF16),
                   jax.ShapeDtypeStruct((b, t, LANES), F32)],
        compiler_params=_params("arbitrary", "arbitrary"),
        name="mixer_out_proj",
    )(o, w_out, x, g1, n2, shift, scale, wr_hi, wr_lo)


def _exclusive_cumsum_lanes(mask):
    rows, t = mask.shape
    width = min(2 * LANES, t)
    r = lax.broadcasted_iota(I32, (width, width), 0)
    c = lax.broadcasted_iota(I32, (width, width), 1)
    upper = (r < c).astype(BF16)
    carry = jnp.zeros((rows, 1), F32)
    outs = []
    for ch in range(t // width):
        m = mask[:, ch * width:(ch + 1) * width]
        outs.append(_dot(m.astype(BF16), upper) + carry)
        carry = carry + jnp.sum(m, axis=1, keepdims=True)
    return jnp.concatenate(outs, axis=1)


LOG_FLOOR = -160.0
BISECT_LOG_STEPS = 40
BISECT_STEPS = 12


def _affinity_kernel(lg_ref, aff_ref, aff_t_ref):
    t = lg_ref.shape[1]
    lane = lax.broadcasted_iota(I32, (t, LANES), 1)
    lg = jnp.where(lane < N_EXPERTS, lg_ref[0], -jnp.inf)
    e = jnp.exp(lg - jnp.max(lg, axis=-1, keepdims=True))
    aff = e / jnp.sum(e, axis=-1, keepdims=True)
    aff_ref[0] = aff
    aff_t_ref[0] = aff.T[:N_EXPERTS]


def _select_kernel(aff_t_ref, pos_t_ref, starts_ref, *, cap):
    rows = aff_t_ref.shape[0]

    def enough(threshold):
        return jnp.sum((aff_t_ref[...] >= threshold).astype(F32), axis=1, keepdims=True) >= cap

    def bisect_log(_, carry):
        lo_u, hi_u = carry
        mid = lo_u + (hi_u - lo_u) * 0.5
        ok = enough(jnp.exp2(mid))
        return jnp.where(ok, mid, lo_u), jnp.where(ok, hi_u, mid)

    def bisect(_, carry):
        lo, hi = carry
        mid = lo + (hi - lo) * 0.5
        ok = enough(mid)
        return jnp.where(ok, mid, lo), jnp.where(ok, hi, mid)

    lo_u, hi_u = lax.fori_loop(0, BISECT_LOG_STEPS, bisect_log,
                               (jnp.full((rows, 1), LOG_FLOOR, F32), jnp.full((rows, 1), 1.0, F32)))
    lo, hi = lax.fori_loop(0, BISECT_STEPS, bisect, (jnp.exp2(lo_u), jnp.exp2(hi_u)))
    aff_t = aff_t_ref[...]
    above = (aff_t >= hi).astype(F32)
    tied = ((aff_t >= lo) & (aff_t < hi)).astype(F32)
    need = cap - jnp.sum(above, axis=1, keepdims=True)
    sel = above + tied * (_exclusive_cumsum_lanes(tied) < need).astype(F32)
    slot = _exclusive_cumsum_lanes(sel)
    pos_t_ref[...] = jnp.where(sel > 0, slot, -1.0).astype(I32)
    t = aff_t.shape[1]
    tok = lax.broadcasted_iota(I32, (t, LANES), 0)
    edge = lax.broadcasted_iota(I32, (t, LANES), 1) * min(TOKEN_TILE, t)
    before = ((tok < edge) & (edge <= t)).astype(BF16)
    starts_ref[...] = _dot(sel.astype(BF16), before).astype(I32)


def _route(logits, cap):
    b, t, _ = logits.shape
    aff, aff_t = pl.pallas_call(
        _affinity_kernel,
        grid=(b,),
        in_specs=[pl.BlockSpec((1, t, LANES), lambda i: (i, 0, 0))],
        out_specs=[pl.BlockSpec((1, t, LANES), lambda i: (i, 0, 0)),
                   pl.BlockSpec((1, N_EXPERTS, t), lambda i: (i, 0, 0))],
        out_shape=[jax.ShapeDtypeStruct((b, t, LANES), F32),
                   jax.ShapeDtypeStruct((b, N_EXPERTS, t), F32)],
        compiler_params=_params("arbitrary"),
        name="ec_affinity",
    )(logits)
    rows = b * N_EXPERTS
    pos_t, starts = pl.pallas_call(
        functools.partial(_select_kernel, cap=cap),
        grid=(1,),
        in_specs=[pl.BlockSpec((rows, t), lambda i: (0, 0))],
        out_specs=[pl.BlockSpec((rows, t), lambda i: (0, 0)),
                   pl.BlockSpec((rows, LANES), lambda i: (0, 0))],
        out_shape=[jax.ShapeDtypeStruct((rows, t), I32),
                   jax.ShapeDtypeStruct((rows, LANES), I32)],
        compiler_params=_params("arbitrary"),
        name="ec_select",
    )(aff_t.reshape(rows, t))
    return pos_t.reshape(b, N_EXPERTS, t), starts.reshape(b, N_EXPERTS, LANES), aff


def _gather_kernel(pos_t_ref, h_ref, xs_ref, *, cap):
    t = h_ref.shape[1]
    slot = lax.broadcasted_iota(I32, (cap, t), 0)
    onehot = (slot == pos_t_ref[0, 0]).astype(BF16)
    xs_ref[0, 0] = _dot(onehot, h_ref[0]).astype(BF16)


def _gather(pos_t, h, cap):
    b, t, d = h.shape
    return pl.pallas_call(
        functools.partial(_gather_kernel, cap=cap),
        grid=(b, N_EXPERTS),
        in_specs=[pl.BlockSpec((1, 1, 1, t), lambda i, e: (i, e, 0, 0)),
                  pl.BlockSpec((1, t, d), lambda i, e: (i, 0, 0))],
        out_specs=pl.BlockSpec((1, 1, cap, d), lambda i, e: (e, i, 0, 0)),
        out_shape=jax.ShapeDtypeStruct((N_EXPERTS, b, cap, d), BF16),
        compiler_params=_params("arbitrary", "arbitrary"),
        name="ec_gather",
    )(pos_t.reshape(b, N_EXPERTS, 1, t), h)


def _gather_stacked_kernel(pos_t_ref, h_ref, xs_ref, *, cap):
    t = h_ref.shape[1]
    pos_t = pos_t_ref[0]
    slot = lax.broadcasted_iota(I32, (cap, t), 0)
    onehot = jnp.concatenate([(slot == pos_t[e:e + 1, :]).astype(BF16) for e in range(N_EXPERTS)], axis=0)
    z = _dot(onehot, h_ref[0]).astype(BF16)
    for e in range(N_EXPERTS):
        xs_ref[e, 0] = z[e * cap:(e + 1) * cap]


def _gather_stacked(pos_t, h, cap):
    b, t, d = h.shape
    return pl.pallas_call(
        functools.partial(_gather_stacked_kernel, cap=cap),
        grid=(b,),
        in_specs=[pl.BlockSpec((1, N_EXPERTS, t), lambda i: (i, 0, 0)),
                  pl.BlockSpec((1, t, d), lambda i: (i, 0, 0))],
        out_specs=pl.BlockSpec((N_EXPERTS, 1, cap, d), lambda i: (0, i, 0, 0)),
        out_shape=jax.ShapeDtypeStruct((N_EXPERTS, b, cap, d), BF16),
        compiler_params=_params("arbitrary"),
        name="ec_gather_stacked",
    )(pos_t, h)


def _tile_windows(starts_ref, cap, ntp):
    i, j = pl.program_id(0), pl.program_id(1)
    wins, fits = [], None
    for e in range(N_EXPERTS):
        at = (i * N_EXPERTS + e) * ntp + j
        win = jnp.minimum((starts_ref[at] // SLOT_ALIGN) * SLOT_ALIGN, cap - SLOT_WINDOW)
        ok = starts_ref[at + 1] - win <= SLOT_WINDOW
        wins.append(pl.multiple_of(win, SLOT_ALIGN))
        fits = ok if fits is None else jnp.logical_and(fits, ok)
    return wins, fits


def _gather_win_kernel(starts_ref, pos_t_ref, h_ref, xs_ref, *, cap, ntp):
    w = SLOT_WINDOW
    tile = h_ref.shape[1]
    wins, fits = _tile_windows(starts_ref, cap, ntp)
    h = h_ref[0]
    pos_t = pos_t_ref[0]

    @pl.when(pl.program_id(1) == 0)
    def _():
        xs_ref[...] = jnp.zeros_like(xs_ref)

    @pl.when(fits)
    def _():
        row = lax.broadcasted_iota(I32, (w, tile), 0)
        onehot = jnp.concatenate([(row + wins[e] == pos_t[e:e + 1, :]).astype(BF16) for e in range(N_EXPERTS)],
                                 axis=0)
        z = _dot(onehot, h).astype(BF16)
        for e in range(N_EXPERTS):
            xs_ref[e, 0, pl.ds(wins[e], w), :] += z[e * w:(e + 1) * w]

    @pl.when(jnp.logical_not(fits))
    def _():
        slot = lax.broadcasted_iota(I32, (cap, tile), 0)
        for e in range(N_EXPERTS):
            onehot = (slot == pos_t[e:e + 1, :]).astype(BF16)
            xs_ref[e, 0] += _dot(onehot, h).astype(BF16)


def _gather_windowed(pos_t, starts, h, cap):
    b, t, d = h.shape
    tile = min(TOKEN_TILE, t)
    nt = t // tile
    grid_spec = pltpu.PrefetchScalarGridSpec(
        num_scalar_prefetch=1,
        grid=(b, nt),
        in_specs=[pl.BlockSpec((1, N_EXPERTS, tile), lambda i, j, s: (i, 0, j)),
                  pl.BlockSpec((1, tile, d), lambda i, j, s: (i, j, 0))],
        out_specs=pl.BlockSpec((N_EXPERTS, 1, cap, d), lambda i, j, s: (0, i, 0, 0)),
    )
    return pl.pallas_call(
        functools.partial(_gather_win_kernel, cap=cap, ntp=nt + 1),
        grid_spec=grid_spec,
        out_shape=jax.ShapeDtypeStruct((N_EXPERTS, b, cap, d), BF16),
        compiler_params=_params("arbitrary", "arbitrary"),
        name="ec_gather_win",
    )(starts[:, :, :nt + 1].reshape(-1), pos_t, h)


def _ffn_up_kernel(xs_ref, wg_ref, wu_ref, hm_ref):
    xs = xs_ref[0]
    a = _dot(xs, wg_ref[0, 0].astype(BF16))
    u = _dot(xs, wu_ref[0, 0].astype(BF16))
    hm_ref[0] = (_silu(a) * u).astype(hm_ref.dtype)


def _ffn_down_kernel(hm_ref, wd_ref, y_ref, wd_bf16_ref):
    @pl.when(pl.program_id(1) == 0)
    def _():
        wd_bf16_ref[...] = wd_ref[0, 0].astype(BF16)

    y_ref[0] = _dot(hm_ref[0], wd_bf16_ref[...]).astype(y_ref.dtype)


def _expert_ffn(xs, w_gate, w_up, w_down, layer):
    n_exp, rows, d = xs.shape
    ff = w_gate.shape[-1]
    tm = min(MOE_ROW_TILE, rows)
    tf = min(FF_TILE, ff)
    if rows <= FEW_ROWS and ff % FF_TILE_FEW_ROWS == 0:
        tf = FF_TILE_FEW_ROWS
    hm = pl.pallas_call(
        _ffn_up_kernel,
        grid=(n_exp, rows // tm, ff // tf),
        in_specs=[pl.BlockSpec((1, tm, d), lambda e, m, f: (e, m, 0)),
                  pl.BlockSpec((1, 1, d, tf), lambda e, m, f: (layer, e, 0, f)),
                  pl.BlockSpec((1, 1, d, tf), lambda e, m, f: (layer, e, 0, f))],
        out_specs=pl.BlockSpec((1, tm, tf), lambda e, m, f: (e, m, f)),
        out_shape=jax.ShapeDtypeStruct((n_exp, rows, ff), BF16),
        compiler_params=_params("arbitrary", "arbitrary", "arbitrary"),
        name="ec_ffn_up",
    )(xs, w_gate, w_up)
    tm2 = min(MOE_DOWN_ROW_TILE, rows)
    return pl.pallas_call(
        _ffn_down_kernel,
        grid=(n_exp, rows // tm2),
        in_specs=[pl.BlockSpec((1, tm2, ff), lambda e, m: (e, m, 0)),
                  pl.BlockSpec((1, 1, ff, d), lambda e, m: (layer, e, 0, 0))],
        out_specs=pl.BlockSpec((1, tm2, d), lambda e, m: (e, m, 0)),
        out_shape=jax.ShapeDtypeStruct((n_exp, rows, d), BF16),
        scratch_shapes=[pltpu.VMEM((ff, d), BF16)],
        compiler_params=_params("arbitrary", "arbitrary"),
        name="ec_ffn_down",
    )(hm, w_down)


def _combine_kernel(x_ref, g2_ref, pos_t_ref, aff_ref, y_ref, fg_ref, o_ref, *, cap, final_norm):
    tt = x_ref.shape[1]
    pos = _token_major(pos_t_ref)
    aff = aff_ref[0]
    lane = lax.broadcasted_iota(I32, (tt, cap), 1)
    acc = jnp.zeros(x_ref.shape[1:], F32)
    for e in range(N_EXPERTS):
        onehot = (lane == pos[:, e:e + 1]).astype(BF16)
        acc = acc + aff[:, e:e + 1] * _dot(onehot, y_ref[e, 0])
    _finish_combine(x_ref, g2_ref, fg_ref, o_ref, acc, final_norm)


def _combine(x1, g2, pos_t, aff, y, final_g, cap, final_norm):
    b, t, d = x1.shape
    tt = min(TOKEN_TILE, t)
    row = lambda i, j: (i, j, 0)
    return pl.pallas_call(
        functools.partial(_combine_kernel, cap=cap, final_norm=final_norm),
        grid=(b, t // tt),
        in_specs=[pl.BlockSpec((1, tt, d), row),
                  pl.BlockSpec((1, 1, d), lambda i, j: (i, 0, 0)),
                  pl.BlockSpec((1, N_EXPERTS, tt), lambda i, j: (i, 0, j)),
                  pl.BlockSpec((1, tt, LANES), row),
                  pl.BlockSpec((N_EXPERTS, 1, cap, d), lambda i, j: (0, i, 0, 0)),
                  pl.BlockSpec((1, d), lambda i, j: (0, 0))],
        out_specs=pl.BlockSpec((1, tt, d), row),
        out_shape=jax.ShapeDtypeStruct((b, t, d), F32),
        compiler_params=_params("arbitrary", "arbitrary"),
        name="ec_combine",
    )(x1, g2, pos_t, aff, y, final_g)


def _token_major(pos_t_ref):
    pos_t = pos_t_ref[0].astype(F32)
    pad = jnp.full((LANES - N_EXPERTS, pos_t.shape[1]), -1.0, F32)
    return jnp.concatenate([pos_t, pad], axis=0).T.astype(I32)


def _finish_combine(x_ref, g2_ref, fg_ref, o_ref, acc, final_norm):
    x2 = x_ref[0] + g2_ref[0] * acc
    if final_norm:
        x2 = (x2 * lax.rsqrt(jnp.mean(x2 * x2, axis=-1, keepdims=True) + NORM_EPS)) * fg_ref[...]
    o_ref[0] = x2


def _combine_win_kernel(starts_ref, x_ref, g2_ref, pos_t_ref, aff_ref, y_ref, fg_ref, o_ref,
                        *, cap, ntp, final_norm):
    w = SLOT_WINDOW
    assert w & (w - 1) == 0, "window column index is taken with a bit mask"
    tile = x_ref.shape[1]
    wins, fits = _tile_windows(starts_ref, cap, ntp)
    pos = _token_major(pos_t_ref)
    aff = aff_ref[0]

    @pl.when(fits)
    def _():
        ywin = jnp.concatenate([y_ref[e, 0, pl.ds(wins[e], w), :] for e in range(N_EXPERTS)], axis=0)
        lane = lax.broadcasted_iota(I32, (1, LANES), 1)
        win_row = jnp.zeros((1, LANES), I32)
        for e in range(N_EXPERTS):
            win_row = jnp.where(lane == e, wins[e], win_row)
        rel = jnp.where(pos >= 0, jnp.clip(pos - win_row, -1, w), -1)
        src = lax.broadcasted_iota(I32, (LANES, N_EXPERTS * w), 0) * w
        col = lax.broadcasted_iota(I32, (LANES, N_EXPERTS * w), 1)
        spread = ((col >= src) & (col < src + w)).astype(BF16)
        wide = _dot(jnp.concatenate([rel.astype(F32).astype(BF16), aff.astype(BF16)], axis=0), spread)
        in_win = lax.broadcasted_iota(I32, (1, N_EXPERTS * w), 1) & (w - 1)
        gates = jnp.where(wide[:tile] == in_win.astype(F32), wide[tile:], 0.0).astype(BF16)
        _finish_combine(x_ref, g2_ref, fg_ref, o_ref, _dot(gates, ywin), final_norm)

    @pl.when(jnp.logical_not(fits))
    def _():
        lane = lax.broadcasted_iota(I32, (tile, cap), 1)
        acc = jnp.zeros(x_ref.shape[1:], F32)
        for e in range(N_EXPERTS):
            onehot = (lane == pos[:, e:e + 1]).astype(BF16)
            acc = acc + aff[:, e:e + 1] * _dot(onehot, y_ref[e, 0])
        _finish_combine(x_ref, g2_ref, fg_ref, o_ref, acc, final_norm)


def _combine_windowed(x1, g2, pos_t, starts, aff, y, final_g, cap, final_norm):
    b, t, d = x1.shape
    tile = min(TOKEN_TILE, t)
    nt = t // tile
    row = lambda i, j, s: (i, j, 0)
    grid_spec = pltpu.PrefetchScalarGridSpec(
        num_scalar_prefetch=1,
        grid=(b, nt),
        in_specs=[pl.BlockSpec((1, tile, d), row),
                  pl.BlockSpec((1, 1, d), lambda i, j, s: (i, 0, 0)),
                  pl.BlockSpec((1, N_EXPERTS, tile), lambda i, j, s: (i, 0, j)),
                  pl.BlockSpec((1, tile, LANES), row),
                  pl.BlockSpec((N_EXPERTS, 1, cap, d), lambda i, j, s: (0, i, 0, 0)),
                  pl.BlockSpec((1, d), lambda i, j, s: (0, 0))],
        out_specs=pl.BlockSpec((1, tile, d), row),
    )
    return pl.pallas_call(
        functools.partial(_combine_win_kernel, cap=cap, ntp=nt + 1, final_norm=final_norm),
        grid_spec=grid_spec,
        out_shape=jax.ShapeDtypeStruct((b, t, d), F32),
        compiler_params=_params("arbitrary", "arbitrary"),
        name="ec_combine_win",
    )(starts[:, :, :nt + 1].reshape(-1), x1, g2, pos_t, aff, y, final_g)


def _ec_moe(x1, h2, logits, g2, w_gate, w_up, w_down, layer, final_g, final_norm):
    b, t, d = x1.shape
    cap = CAPACITY_FACTOR * t // N_EXPERTS
    pos_t, starts, aff = _route(logits, cap)
    windowed = t > TOKEN_TILE and cap >= SLOT_WINDOW and (cap - SLOT_WINDOW) % SLOT_ALIGN == 0
    if windowed:
        xs = _gather_windowed(pos_t, starts, h2, cap)
    elif t <= TOKEN_TILE and cap % SLOT_ALIGN == 0:
        xs = _gather_stacked(pos_t, h2, cap)
    else:
        xs = _gather(pos_t, h2, cap)
    y = _expert_ffn(xs.reshape(N_EXPERTS, b * cap, d), w_gate, w_up, w_down, layer)
    y = y.reshape(N_EXPERTS, b, cap, d)
    if windowed:
        return _combine_windowed(x1, g2, pos_t, starts, aff, y, final_g, cap, final_norm)
    return _combine(x1, g2, pos_t, aff, y, final_g, cap, final_norm)


def _pre_gla_kernel(x_ref, g_ref, sh_ref, sc_ref, w_ref, wlr_ref, w2_ref, gb_ref,
                    q_ref, r_ref, k_ref, v_ref, lgf_ref, lgb_ref):
    qk = q_ref.shape[-1]
    vd = v_ref.shape[-1]
    h = _norm_mod(x_ref[0], g_ref[...], sh_ref[0], sc_ref[0]).astype(BF16)
    p = _dot(h, w_ref[...])
    q_ref[0] = (p[:, :qk] * (qk // GLA_HEADS) ** -0.5).astype(BF16)
    r_ref[0] = _silu(p[:, qk:qk + vd]).astype(BF16)
    k_ref[0] = p[:, qk + vd:2 * qk + vd].astype(BF16)
    v_ref[0] = p[:, 2 * qk + vd:].astype(BF16)
    lr = _dot(h, wlr_ref[...]).astype(BF16)
    z = _dot(lr, w2_ref[...]) + gb_ref[...]
    lg = (jnp.minimum(z, 0.0) - jnp.log1p(jnp.exp(-jnp.abs(z)))) * (1.0 / GLA_TAU)
    lgf_ref[0] = lg[:, :qk]
    lgb_ref[0] = lg[:, qk:]


def _pre_gla(x, g, shift, scale, w_main, w_lr, w2, gate_b):
    b, t, d = x.shape
    n_main = w_main.shape[1]
    qk = w2.shape[1] // 2
    vd = (n_main - 2 * qk) // 2
    tm = min(ROW_TILE, t)
    row = lambda i, j: (i, j, 0)
    per_b = lambda i, j: (i, 0, 0)
    fixed = lambda i, j: (0, 0)
    out = lambda n, dt: (pl.BlockSpec((1, tm, n), row), jax.ShapeDtypeStruct((b, t, n), dt))
    outs = [out(qk, BF16), out(vd, BF16), out(qk, BF16), out(vd, BF16), out(qk, F32), out(qk, F32)]
    return pl.pallas_call(
        _pre_gla_kernel,
        grid=(b, t // tm),
        in_specs=[pl.BlockSpec((1, tm, d), row),
                  pl.BlockSpec((1, d), fixed),
                  pl.BlockSpec((1, 1, d), per_b),
                  pl.BlockSpec((1, 1, d), per_b),
                  pl.BlockSpec((d, n_main), fixed),
                  pl.BlockSpec((d, LANES), fixed),
                  pl.BlockSpec((LANES, 2 * qk), fixed),
                  pl.BlockSpec((1, 2 * qk), fixed)],
        out_specs=[o[0] for o in outs],
        out_shape=[o[1] for o in outs],
        compiler_params=_params("arbitrary", "arbitrary"),
        name="gla_in_proj",
    )(x, g, shift, scale, w_main, w_lr, w2, gate_b)


def _gla_masks(rows):
    ri = lax.broadcasted_iota(I32, (rows, rows), 0)
    ci = lax.broadcasted_iota(I32, (rows, rows), 1)
    same = (ri // GLA_CHUNK) == (ci // GLA_CHUNK)
    return same & (ri >= ci), same & (ri <= ci)


def _gla_tiles(jobs):
    ch = GLA_CHUNK
    n = len(jobs)
    rows, dk = jobs[0][1].shape
    n_chunks = rows // ch
    last = [[c * ch + (ch - 1 if job[6] else 0) for c in range(n_chunks)] for job in jobs]
    b = []
    for q, k, v, lg, st_ref, mask, forward in jobs:
        lg_hi, lg_lo = _split2(lg)
        m = mask.astype(BF16)
        b.append(_dot(m, lg_hi) + _dot(m, lg_lo))
    qt, kt, kd = [None] * n, [None] * n, [None] * n
    for i, (q, k, v, lg, st_ref, mask, forward) in enumerate(jobs):
        b_last = jnp.concatenate([jnp.broadcast_to(b[i][r:r + 1, :], (ch, dk)) for r in last[i]], axis=0)
        kf = k.astype(F32)
        kd[i] = (kf * jnp.exp(b_last - b[i])).astype(BF16)
        if q is not None:
            qt[i] = (q.astype(F32) * jnp.exp(b[i])).astype(BF16)
            kt[i] = (kf * jnp.exp(-b[i])).astype(BF16)
    a = [None if jobs[i][0] is None else lax.dot_general(qt[i], kt[i], NT_DIMS, preferred_element_type=F32)
         for i in range(n)]
    a = [None if a[i] is None else jnp.where(jobs[i][5], a[i], 0.0).astype(BF16) for i in range(n)]
    o_local = [None if a[i] is None else _dot(a[i], jobs[i][2]) for i in range(n)]
    d_st = [[lax.dot_general(jobs[i][2][c * ch:(c + 1) * ch], kd[i][c * ch:(c + 1) * ch], TN_DIMS,
                             preferred_element_type=F32) for c in range(n_chunks)] for i in range(n)]
    results = []
    for i, (q, k, v, lg, st_ref, mask, forward) in enumerate(jobs):
        st = st_ref[...]
        outs = [None] * n_chunks
        for c in (range(n_chunks) if forward else reversed(range(n_chunks))):
            rs = slice(c * ch, (c + 1) * ch)
            if q is not None:
                outs[c] = o_local[i][rs] + lax.dot_general(qt[i][rs], st.astype(BF16), NT_DIMS,
                                                           preferred_element_type=F32)
            st = jnp.exp(b[i][last[i][c]:last[i][c] + 1, :]) * st + d_st[i][c]
        st_ref[...] = st
        results.append(None if q is None else jnp.concatenate(outs, axis=0))
    return results


def _gla_kernel(q_ref, k_ref, v_ref, lgf_ref, lgb_ref, kc_ref, vc_ref, lgcf_ref, lgcb_ref, r_ref, ng_ref,
                o_ref, of_ref, ob_ref, stf_ref, stb_ref):
    t = q_ref.shape[1]
    t_ctx = kc_ref.shape[1]
    n_heads = stf_ref.shape[0]
    dv, dk = stf_ref.shape[1:]
    tile = min(GLA_TILE, t)
    ctile = min(GLA_TILE, t_ctx)
    mask_f, mask_b = _gla_masks(tile)
    cmask_f, cmask_b = (mask_f, mask_b) if ctile == tile else _gla_masks(ctile)
    stf_ref[...] = jnp.zeros_like(stf_ref)
    stb_ref[...] = jnp.zeros_like(stb_ref)
    kcols = [slice(h * dk, (h + 1) * dk) for h in range(n_heads)]
    vcols = [slice(h * dv, (h + 1) * dv) for h in range(n_heads)]

    n_ctx = t_ctx // ctile
    for i in range(n_ctx):
        rf = slice(i * ctile, (i + 1) * ctile)
        rb = slice((n_ctx - 1 - i) * ctile, (n_ctx - i) * ctile)
        jobs = []
        for h in range(n_heads):
            jobs.append((None, kc_ref[0, rf, kcols[h]], vc_ref[0, rf, vcols[h]], lgcf_ref[0, rf, kcols[h]],
                         stf_ref.at[h], cmask_f, True))
            jobs.append((None, kc_ref[0, rb, kcols[h]], vc_ref[0, rb, vcols[h]], lgcb_ref[0, rb, kcols[h]],
                         stb_ref.at[h], cmask_b, False))
        _gla_tiles(jobs)

    n = t // tile

    def body(i, carry):
        rf = pl.ds(pl.multiple_of(i * tile, tile), tile)
        rb = pl.ds(pl.multiple_of((n - 1 - i) * tile, tile), tile)
        jobs = []
        for h in range(n_heads):
            jobs.append((q_ref[0, rf, kcols[h]], k_ref[0, rf, kcols[h]], v_ref[0, rf, vcols[h]],
                         lgf_ref[0, rf, kcols[h]], stf_ref.at[h], mask_f, True))
            jobs.append((q_ref[0, rb, kcols[h]], k_ref[0, rb, kcols[h]], v_ref[0, rb, vcols[h]],
                         lgb_ref[0, rb, kcols[h]], stb_ref.at[h], mask_b, False))
        outs = _gla_tiles(jobs)
        for h in range(n_heads):
            of_ref[rf, vcols[h]] = outs[2 * h]
            ob_ref[rb, vcols[h]] = outs[2 * h + 1]
        return carry

    lax.fori_loop(0, n, body, 0)

    def readout(i, carry):
        rows = pl.ds(pl.multiple_of(i * tile, tile), tile)
        for h in range(n_heads):
            o = of_ref[rows, vcols[h]] + ob_ref[rows, vcols[h]]
            o = o * lax.rsqrt(jnp.mean(o * o, axis=-1, keepdims=True) + NORM_EPS) * ng_ref[:, vcols[h]]
            o_ref[0, rows, vcols[h]] = (o * r_ref[0, rows, vcols[h]].astype(F32)).astype(o_ref.dtype)
        return carry

    lax.fori_loop(0, n, readout, 0)


def _gla(q, k, v, lgf, lgb, kc, vc, lgcf, lgcb, r, norm_g):
    b, t, qk = q.shape
    vd = v.shape[-1]
    t_ctx = kc.shape[1]
    dk, dv = qk // GLA_HEADS, vd // GLA_HEADS
    hp = GLA_HEADS_PER_STEP
    head = lambda i, h: (i, 0, h)
    lat = lambda n: pl.BlockSpec((1, t, hp * n), head)
    ctx = lambda n: pl.BlockSpec((1, t_ctx, hp * n), head)
    return pl.pallas_call(
        _gla_kernel,
        grid=(b, GLA_HEADS // hp),
        in_specs=[lat(dk), lat(dk), lat(dv), lat(dk), lat(dk),
                  ctx(dk), ctx(dv), ctx(dk), ctx(dk),
                  lat(dv),
                  pl.BlockSpec((1, hp * dv), lambda i, h: (0, h))],
        out_specs=lat(dv),
        out_shape=jax.ShapeDtypeStruct((b, t, vd), BF16),
        scratch_shapes=[pltpu.VMEM((t, hp * dv), F32), pltpu.VMEM((t, hp * dv), F32),
                        pltpu.VMEM((hp, dv, dk), F32), pltpu.VMEM((hp, dv, dk), F32)],
        compiler_params=_params("arbitrary", "arbitrary"),
        name="gla_scan",
    )(q, k, v, lgf, lgb, kc, vc, lgcf, lgcb, r, norm_g)


def _rope_tables(t):
    pos = jnp.arange(t)
    row = (pos // GRID_W).astype(F32)
    col = (pos % GRID_W).astype(F32)
    inv = ROPE_BASE ** (-jnp.arange(ROPE_PAIRS, dtype=F32) / ROPE_PAIRS)
    ar, ac = row[:, None] * inv, col[:, None] * inv
    ang = jnp.concatenate([ar, ar, ac, ac], axis=-1)
    sign = jnp.tile(jnp.repeat(jnp.array([-1.0, 1.0], F32), ROPE_PAIRS), 2)
    reps = LANES // HEAD_DIM
    return jnp.tile(jnp.cos(ang), (1, reps)), jnp.tile(jnp.sin(ang) * sign, (1, reps))


def _router_split(w):
    d, n = w.shape
    w = jnp.pad(w, ((0, 0), (0, LANES - n)))
    hi = w.astype(BF16)
    return hi, (w - hi.astype(F32)).astype(BF16)


def kernel(x, c, ctx, c_ctx, ada_w, ada_b, norm1_g, norm2_g, attn_w_in, attn_w_out, attn_sink, gla_w_in, gla_gate_w2, gla_gate_b, gla_norm_g, gla_w_out, router_w, exp_w_gate, exp_w_up, exp_w_down, final_norm_g):
    b, t, d = x.shape
    depth = ada_w.shape[0]
    assert depth == 2, "layer 0 is windowed attention with context output, layer 1 is GLA and last"
    rows = -(-(b + 1) // 16) * 16
    src = jnp.concatenate([c, c_ctx[None, :], jnp.zeros((rows - b - 1, d), F32)], axis=0)
    cos, sin_signed = _rope_tables(t)
    final_g = final_norm_g.reshape(1, d)

    def chunks(i):
        mod = _modulation(src, ada_w, ada_b, i)
        lat = [mod[:b, j * d:(j + 1) * d].reshape(b, 1, d) for j in range(6)]
        cx = [jnp.broadcast_to(mod[b, j * d:(j + 1) * d].reshape(1, 1, d), (b, 1, d)) for j in range(6)]
        return lat, cx

    (sh1, sc1, g1, sh2, sc2, g2), (sh1c, sc1c, g1c, sh2c, sc2c, g2c) = chunks(0)
    n1, n2 = norm1_g[0].reshape(1, d), norm2_g[0].reshape(1, d)
    w_in = attn_w_in[0].astype(BF16)
    w_out = attn_w_out[0].astype(BF16)
    sink = attn_sink[0].reshape(1, -1)
    wr_hi, wr_lo = _router_split(router_w[0])
    q, k, v = _pre_attn(x, n1, sh1, sc1, w_in, cos, sin_signed, rope=True)
    t_ctx = ctx.shape[1]
    qc, kc, vc = _pre_attn(ctx, n1, sh1c, sc1c, w_in, cos[:t_ctx], sin_signed[:t_ctx], rope=False)
    o = _attn_band(q, k, v, kc, vc, sink)
    oc = _attn_ctx(qc, kc, vc, sink)
    x1, h2, logits = _post_mixer(o, w_out, x, g1, n2, sh2, sc2, wr_hi, wr_lo)
    xc1, hc2, logits_c = _post_mixer(oc, w_out, ctx, g1c, n2, sh2c, sc2c, wr_hi, wr_lo)
    x = _ec_moe(x1, h2, logits, g2, exp_w_gate, exp_w_up, exp_w_down, 0, final_g, False)
    xc = _ec_moe(xc1, hc2, logits_c, g2c, exp_w_gate, exp_w_up, exp_w_down, 0, final_g, False)

    (sh1, sc1, g1, sh2, sc2, g2), (sh1c, sc1c, _, _, _, _) = chunks(1)
    n1, n2 = norm1_g[1].reshape(1, d), norm2_g[1].reshape(1, d)
    qk = gla_gate_w2.shape[-1]
    n_main = gla_w_in.shape[-1] - 2 * GLA_RANK
    w_main = gla_w_in[0][:, :n_main].astype(BF16)
    w_lr = jnp.pad(gla_w_in[0][:, n_main:], ((0, 0), (0, LANES - 2 * GLA_RANK))).astype(BF16)
    w2 = jnp.zeros((LANES, 2 * qk), F32)
    w2 = w2.at[:GLA_RANK, :qk].set(gla_gate_w2[0, 0]).at[GLA_RANK:2 * GLA_RANK, qk:].set(gla_gate_w2[0, 1])
    w2 = w2.astype(BF16)
    gate_b = gla_gate_b[0].reshape(1, 2 * qk)
    gq, gr, gk, gv, lgf, lgb = _pre_gla(x, n1, sh1, sc1, w_main, w_lr, w2, gate_b)
    _, _, gkc, gvc, lgcf, lgcb = _pre_gla(xc, n1, sh1c, sc1c, w_main, w_lr, w2, gate_b)
    og = _gla(gq, gk, gv, lgf, lgb, gkc, gvc, lgcf, lgcb, gr, gla_norm_g[0].reshape(1, -1))
    wr_hi, wr_lo = _router_split(router_w[1])
    x1, h2, logits = _post_mixer(og, gla_w_out[0].astype(BF16), x, g1, n2, sh2, sc2, wr_hi, wr_lo)
    return _ec_moe(x1, h2, logits, g2, exp_w_gate, exp_w_up, exp_w_down, 1, final_g, True)
```

```python
import functools

import jax
import jax.numpy as jnp
from jax import lax
from jax.experimental import pallas as pl
from jax.experimental.pallas import tpu as pltpu

F32 = jnp.float32
BF16 = jnp.bfloat16
I32 = jnp.int32

LANES = 128
HEAD_DIM = 64
ATT_GROUP = 4
ATT_BLOCK = 128
GRID_W = 64
ROPE_BASE = 10000.0
ROPE_PAIRS = HEAD_DIM // 4
GLA_HEADS = 4
GLA_RANK = 16
GLA_TAU = 16.0
GLA_CHUNK = 64
N_EXPERTS = 16
CAPACITY_FACTOR = 2
NORM_EPS = 1e-6
LOG2E = 1.4426950408889634
ROW_TILE = 256
ATTN_IN_ROW_TILE = 512
ATTN_IN_SUBTILE_ROWS = 256
POST_SUBTILES = 2
TOKEN_TILE = 256
MOE_ROW_TILE = 4608
MOE_DOWN_ROW_TILE = 1024
FF_TILE = 256
FEW_ROWS = 512
FF_TILE_FEW_ROWS = 1408
GLA_TILE = 256
GLA_HEADS_PER_STEP = 2
SLOT_WINDOW = 64
SLOT_ALIGN = 16
VMEM_LIMIT = 56 * 1024 * 1024

NT_DIMS = (((1,), (1,)), ((), ()))
TN_DIMS = (((0,), (0,)), ((), ()))


def _params(*sem):
    return pltpu.CompilerParams(dimension_semantics=sem, vmem_limit_bytes=VMEM_LIMIT)


def _dot(a, b):
    return jnp.dot(a, b, preferred_element_type=F32)


def _split2(a):
    hi = a.astype(BF16)
    lo = (a - hi.astype(F32)).astype(BF16)
    return hi, lo


def _dot3(a, b_hi, b_lo):
    a_hi, a_lo = _split2(a)
    return _dot(a_hi, b_hi) + _dot(a_hi, b_lo) + _dot(a_lo, b_hi)


def _silu(a):
    return a * jax.nn.sigmoid(a)


def _norm_mod(x, g, shift, scale):
    y = x * lax.rsqrt(jnp.mean(x * x, axis=-1, keepdims=True) + NORM_EPS)
    return (y * g) * (1.0 + scale) + shift


def _mod_kernel(src_ref, w_ref, b_ref, o_ref):
    w_hi, w_lo = _split2(w_ref[0])
    o_ref[...] = _dot3(_silu(src_ref[...]), w_hi, w_lo) + b_ref[0]


def _modulation(src, ada_w, ada_b, layer):
    rows, d = src.shape
    depth, _, n = ada_w.shape
    tn = 512
    return pl.pallas_call(
        _mod_kernel,
        grid=(n // tn,),
        in_specs=[pl.BlockSpec((rows, d), lambda j: (0, 0)),
                  pl.BlockSpec((1, d, tn), lambda j: (layer, 0, j)),
                  pl.BlockSpec((1, 1, tn), lambda j: (layer, 0, j))],
        out_specs=pl.BlockSpec((rows, tn), lambda j: (0, j)),
        out_shape=jax.ShapeDtypeStruct((rows, n), F32),
        compiler_params=_params("arbitrary"),
        name="adaln_mod",
    )(src, ada_w, ada_b.reshape(depth, 1, n))


def _rope(x, cos, sin_signed, lane_lo):
    outs = []
    for j in range(x.shape[1] // LANES):
        xb = x[:, j * LANES:(j + 1) * LANES]
        partner = jnp.where(lane_lo, pltpu.roll(xb, LANES - ROPE_PAIRS, 1), pltpu.roll(xb, ROPE_PAIRS, 1))
        outs.append(xb * cos + partner * sin_signed)
    return jnp.concatenate(outs, axis=1)


def _pre_attn_kernel(x_ref, g_ref, sh_ref, sc_ref, w_ref, cos_ref, sin_ref, q_ref, k_ref, v_ref, *, rope):
    d = x_ref.shape[-1]
    kv = k_ref.shape[-1]
    tm = x_ref.shape[1]
    n_sub = max(1, tm // ATTN_IN_SUBTILE_ROWS)
    subs = [slice(i * tm // n_sub, (i + 1) * tm // n_sub) for i in range(n_sub)]
    hs = [_norm_mod(x_ref[0, rs, :], g_ref[...], sh_ref[0], sc_ref[0]).astype(BF16) for rs in subs]
    ps = [_dot(h, w_ref[...]) for h in hs]
    for rs, p in zip(subs, ps):
        q, k, v = p[:, :d], p[:, d:d + kv], p[:, d + kv:]
        if rope:
            lane = lax.broadcasted_iota(I32, (tm // n_sub, LANES), 1)
            lane_lo = (lane & (2 * ROPE_PAIRS - 1)) < ROPE_PAIRS
            cos, sin_signed = cos_ref[rs, :], sin_ref[rs, :]
            q = _rope(q, cos, sin_signed, lane_lo)
            k = _rope(k, cos, sin_signed, lane_lo)
        q_ref[0, rs, :] = (q * (HEAD_DIM ** -0.5 * LOG2E)).astype(BF16)
        k_ref[0, rs, :] = k.astype(BF16)
        v_ref[0, rs, :] = v.astype(BF16)


def _pre_attn(x, g, shift, scale, w_in, cos, sin_signed, rope):
    b, t, d = x.shape
    n_in = w_in.shape[1]
    kv = (n_in - d) // 2
    tm = min(ATTN_IN_ROW_TILE, t)
    row = lambda i, j: (i, j, 0)
    per_b = lambda i, j: (i, 0, 0)
    fixed = lambda i, j: (0, 0)
    return pl.pallas_call(
        functools.partial(_pre_attn_kernel, rope=rope),
        grid=(b, t // tm),
        in_specs=[pl.BlockSpec((1, tm, d), row),
                  pl.BlockSpec((1, d), fixed),
                  pl.BlockSpec((1, 1, d), per_b),
                  pl.BlockSpec((1, 1, d), per_b),
                  pl.BlockSpec((d, n_in), fixed),
                  pl.BlockSpec((tm, LANES), lambda i, j: (j, 0)),
                  pl.BlockSpec((tm, LANES), lambda i, j: (j, 0))],
        out_specs=[pl.BlockSpec((1, tm, d), row),
                   pl.BlockSpec((1, tm, kv), row),
                   pl.BlockSpec((1, tm, kv), row)],
        out_shape=[jax.ShapeDtypeStruct((b, t, d), BF16),
                   jax.ShapeDtypeStruct((b, t, kv), BF16),
                   jax.ShapeDtypeStruct((b, t, kv), BF16)],
        compiler_params=_params("arbitrary", "arbitrary"),
        name="attn_in_proj",
    )(x, g, shift, scale, w_in, cos, sin_signed)


def _attn_heads(q, k_parts, v_parts, bias_parts, sink_ref, o_ref):
    tq = q.shape[0]
    n_kv = k_parts[0].shape[1] // HEAD_DIM
    lane = lax.broadcasted_iota(I32, (1, ATT_GROUP * tq), 1)
    v_lane = lax.broadcasted_iota(I32, (1, LANES), 1)
    sks, scores, ms, es, both = ([None] * n_kv for _ in range(5))

    def score_stage(h):
        heads = [ATT_GROUP * h + g for g in range(ATT_GROUP)]
        qg = jnp.concatenate([q[:, i * HEAD_DIM:(i + 1) * HEAD_DIM] for i in heads], axis=0)
        sk = jnp.full((1, ATT_GROUP * tq), sink_ref[0, heads[0]], F32)
        for g in range(1, ATT_GROUP):
            sk = jnp.where(lane >= g * tq, sink_ref[0, heads[g]], sk)
        sks[h] = sk * LOG2E
        m = sks[h]
        scores[h] = []
        for kp, bias in zip(k_parts, bias_parts):
            s = lax.dot_general(kp[:, h * HEAD_DIM:(h + 1) * HEAD_DIM], qg, NT_DIMS, preferred_element_type=F32)
            s = s if bias is None else s + bias
            m = jnp.maximum(m, jnp.max(s, axis=0, keepdims=True))
            scores[h].append(s)
        ms[h] = m

    def weight_stage(h):
        es[h] = jnp.concatenate([jnp.exp2(s - ms[h]).astype(BF16) for s in scores[h]], axis=0)

    def value_stage(h):
        block = (h * HEAD_DIM // LANES) * LANES
        upper = (h * HEAD_DIM) % LANES != 0
        v_pair = jnp.concatenate([vp[:, block:block + LANES] for vp in v_parts], axis=0)
        own = (v_lane >= HEAD_DIM) if upper else (v_lane < HEAD_DIM)
        v_aug = jnp.where(own, v_pair, jnp.ones_like(v_pair))
        both[h] = lax.dot_general(v_aug, es[h], TN_DIMS, preferred_element_type=F32)

    stages = (score_stage, weight_stage, value_stage)
    for step in range(n_kv + len(stages) - 1):
        for depth, stage in enumerate(stages):
            if 0 <= step - depth < n_kv:
                stage(step - depth)
    outs = []
    for h in range(n_kv):
        upper = (h * HEAD_DIM) % LANES != 0
        num = both[h][HEAD_DIM:] if upper else both[h][:HEAD_DIM]
        total = both[h][:1] if upper else both[h][HEAD_DIM:HEAD_DIM + 1]
        o_t = num / (total + jnp.exp2(sks[h] - ms[h]))
        outs.extend(o_t[:, g * tq:(g + 1) * tq] for g in range(ATT_GROUP))
    o_ref[0] = jnp.concatenate(outs, axis=0).T.astype(o_ref.dtype)


def _band_bias():
    blk = ATT_BLOCK
    key = jnp.arange(blk)[:, None]
    qry = jnp.arange(ATT_GROUP * blk)[None, :] % blk
    prev_ok = key >= qry
    next_ok = key <= qry
    never = jnp.zeros_like(prev_ok)
    variants = [jnp.stack([never if first else prev_ok, never if last else next_ok])
                for last in (False, True) for first in (False, True)]
    return jnp.where(jnp.stack(variants), 0.0, -jnp.inf).astype(F32)


def _attn_band_kernel(q_ref, kp_ref, ko_ref, kn_ref, vp_ref, vo_ref, vn_ref, kc_ref, vc_ref, bias_ref, sink_ref,
                      o_ref):
    _attn_heads(q_ref[0], [kc_ref[0], kp_ref[0], ko_ref[0], kn_ref[0]], [vc_ref[0], vp_ref[0], vo_ref[0], vn_ref[0]],
                [None, bias_ref[0, 0], None, bias_ref[0, 1]], sink_ref, o_ref)


def _attn_ctx_kernel(q_ref, kc_ref, vc_ref, sink_ref, o_ref):
    _attn_heads(q_ref[0], [kc_ref[0]], [vc_ref[0]], [None], sink_ref, o_ref)


def _attn_band(q, k, v, kc, vc, sink):
    b, t, d = q.shape
    kv = k.shape[-1]
    n_ctx = kc.shape[1]
    nb = t // ATT_BLOCK
    blk = lambda m: pl.BlockSpec((1, ATT_BLOCK, kv), m)
    prev = lambda i, j: (i, jnp.maximum(j - 1, 0), 0)
    own = lambda i, j: (i, j, 0)
    nxt = lambda i, j: (i, jnp.minimum(j + 1, nb - 1), 0)
    ctx_spec = pl.BlockSpec((1, n_ctx, kv), lambda i, j: (i, 0, 0))
    bias = _band_bias()
    edge = lambda i, j: ((j == 0).astype(I32) + 2 * (j == nb - 1).astype(I32), 0, 0, 0)
    return pl.pallas_call(
        _attn_band_kernel,
        grid=(b, nb),
        in_specs=[pl.BlockSpec((1, ATT_BLOCK, d), own),
                  blk(prev), blk(own), blk(nxt), blk(prev), blk(own), blk(nxt),
                  ctx_spec, ctx_spec,
                  pl.BlockSpec((1,) + bias.shape[1:], edge),
                  pl.BlockSpec(memory_space=pltpu.SMEM)],
        out_specs=pl.BlockSpec((1, ATT_BLOCK, d), own),
        out_shape=jax.ShapeDtypeStruct((b, t, d), BF16),
        compiler_params=_params("arbitrary", "arbitrary"),
        name="attn_band",
    )(q, k, k, k, v, v, v, kc, vc, bias, sink)


def _attn_ctx(qc, kc, vc, sink):
    b, n_ctx, d = qc.shape
    kv = kc.shape[-1]
    return pl.pallas_call(
        _attn_ctx_kernel,
        grid=(b,),
        in_specs=[pl.BlockSpec((1, n_ctx, d), lambda i: (i, 0, 0)),
                  pl.BlockSpec((1, n_ctx, kv), lambda i: (i, 0, 0)),
                  pl.BlockSpec((1, n_ctx, kv), lambda i: (i, 0, 0)),
                  pl.BlockSpec(memory_space=pltpu.SMEM)],
        out_specs=pl.BlockSpec((1, n_ctx, d), lambda i: (i, 0, 0)),
        out_shape=jax.ShapeDtypeStruct((b, n_ctx, d), BF16),
        compiler_params=_params("arbitrary"),
        name="attn_ctx",
    )(qc, kc, vc, sink)


def _post_kernel(o_ref, w_ref, x_ref, g1_ref, n2_ref, sh_ref, sc_ref, wr_hi_ref, wr_lo_ref,
                 x1_ref, h2_ref, lg_ref):
    tm = x_ref.shape[1]
    subs = [slice(i * tm // POST_SUBTILES, (i + 1) * tm // POST_SUBTILES) for i in range(POST_SUBTILES)]
    ys = [_dot(o_ref[0, rs, :], w_ref[...]) for rs in subs]
    h2s = []
    for rs, y in zip(subs, ys):
        x1 = x_ref[0, rs, :] + g1_ref[0] * y
        x1_ref[0, rs, :] = x1
        h2 = _norm_mod(x1, n2_ref[...], sh_ref[0], sc_ref[0])
        h2_ref[0, rs, :] = h2.astype(BF16)
        h2s.append(h2)
    for rs, h2 in zip(subs, h2s):
        lg_ref[0, rs, :] = _dot3(h2, wr_hi_ref[...], wr_lo_ref[...])


def _post_mixer(o, w_out, x, g1, n2, shift, scale, wr_hi, wr_lo):
    b, t, d = x.shape
    k_in = o.shape[-1]
    tm = min(ROW_TILE, t)
    row = lambda i, j: (i, j, 0)
    per_b = lambda i, j: (i, 0, 0)
    fixed = lambda i, j: (0, 0)
    return pl.pallas_call(
        _post_kernel,
        grid=(b, t // tm),
        in_specs=[pl.BlockSpec((1, tm, k_in), row),
                  pl.BlockSpec((k_in, d), fixed),
                  pl.BlockSpec((1, tm, d), row),
                  pl.BlockSpec((1, 1, d), per_b),
                  pl.BlockSpec((1, d), fixed),
                  pl.BlockSpec((1, 1, d), per_b),
                  pl.BlockSpec((1, 1, d), per_b),
                  pl.BlockSpec((d, LANES), fixed),
                  pl.BlockSpec((d, LANES), fixed)],
        out_specs=[pl.BlockSpec((1, tm, d), row),
                   pl.BlockSpec((1, tm, d), row),
                   pl.BlockSpec((1, tm, LANES), row)],
        out_shape=[jax.ShapeDtypeStruct((b, t, d), F32),
                   jax.ShapeDtypeStruct((b, t, d), BF16),
                   jax.ShapeDtypeStruct((b, t, LANES), F32)],
        compiler_params=_params("arbitrary", "arbitrary"),
        name="mixer_out_proj",
    )(o, w_out, x, g1, n2, shift, scale, wr_hi, wr_lo)


def _exclusive_cumsum_lanes(mask):
    rows, t = mask.shape
    width = min(2 * LANES, t)
    r = lax.broadcasted_iota(I32, (width, width), 0)
    c = lax.broadcasted_iota(I32, (width, width), 1)
    upper = (r < c).astype(BF16)
    carry = jnp.zeros((rows, 1), F32)
    outs = []
    for ch in range(t // width):
        m = mask[:, ch * width:(ch + 1) * width]
        outs.append(_dot(m.astype(BF16), upper) + carry)
        carry = carry + jnp.sum(m, axis=1, keepdims=True)
    return jnp.concatenate(outs, axis=1)


LOG_FLOOR = -160.0
BISECT_LOG_STEPS = 40
BISECT_STEPS = 12


def _affinity_kernel(lg_ref, aff_ref, aff_t_ref):
    t = lg_ref.shape[1]
    lane = lax.broadcasted_iota(I32, (t, LANES), 1)
    lg = jnp.where(lane < N_EXPERTS, lg_ref[0], -jnp.inf)
    e = jnp.exp(lg - jnp.max(lg, axis=-1, keepdims=True))
    aff = e / jnp.sum(e, axis=-1, keepdims=True)
    aff_ref[0] = aff
    aff_t_ref[0] = aff.T[:N_EXPERTS]


def _select_kernel(aff_t_ref, pos_t_ref, starts_ref, *, cap):
    rows = aff_t_ref.shape[0]

    def enough(threshold):
        return jnp.sum((aff_t_ref[...] >= threshold).astype(F32), axis=1, keepdims=True) >= cap

    def bisect_log(_, carry):
        lo_u, hi_u = carry
        mid = lo_u + (hi_u - lo_u) * 0.5
        ok = enough(jnp.exp2(mid))
        return jnp.where(ok, mid, lo_u), jnp.where(ok, hi_u, mid)

    def bisect(_, carry):
        lo, hi = carry
        mid = lo + (hi - lo) * 0.5
        ok = enough(mid)
        return jnp.where(ok, mid, lo), jnp.where(ok, hi, mid)

    lo_u, hi_u = lax.fori_loop(0, BISECT_LOG_STEPS, bisect_log,
                               (jnp.full((rows, 1), LOG_FLOOR, F32), jnp.full((rows, 1), 1.0, F32)))
    lo, hi = lax.fori_loop(0, BISECT_STEPS, bisect, (jnp.exp2(lo_u), jnp.exp2(hi_u)))
    aff_t = aff_t_ref[...]
    above = (aff_t >= hi).astype(F32)
    tied = ((aff_t >= lo) & (aff_t < hi)).astype(F32)
    need = cap - jnp.sum(above, axis=1, keepdims=True)
    sel = above + tied * (_exclusive_cumsum_lanes(tied) < need).astype(F32)
    slot = _exclusive_cumsum_lanes(sel)
    pos_t_ref[...] = jnp.where(sel > 0, slot, -1.0).astype(I32)
    t = aff_t.shape[1]
    tok = lax.broadcasted_iota(I32, (t, LANES), 0)
    edge = lax.broadcasted_iota(I32, (t, LANES), 1) * min(TOKEN_TILE, t)
    before = ((tok < edge) & (edge <= t)).astype(BF16)
    starts_ref[...] = _dot(sel.astype(BF16), before).astype(I32)


def _route(logits, cap):
    b, t, _ = logits.shape
    aff, aff_t = pl.pallas_call(
        _affinity_kernel,
        grid=(b,),
        in_specs=[pl.BlockSpec((1, t, LANES), lambda i: (i, 0, 0))],
        out_specs=[pl.BlockSpec((1, t, LANES), lambda i: (i, 0, 0)),
                   pl.BlockSpec((1, N_EXPERTS, t), lambda i: (i, 0, 0))],
        out_shape=[jax.ShapeDtypeStruct((b, t, LANES), F32),
                   jax.ShapeDtypeStruct((b, N_EXPERTS, t), F32)],
        compiler_params=_params("arbitrary"),
        name="ec_affinity",
    )(logits)
    rows = b * N_EXPERTS
    pos_t, starts = pl.pallas_call(
        functools.partial(_select_kernel, cap=cap),
        grid=(1,),
        in_specs=[pl.BlockSpec((rows, t), lambda i: (0, 0))],
        out_specs=[pl.BlockSpec((rows, t), lambda i: (0, 0)),
                   pl.BlockSpec((rows, LANES), lambda i: (0, 0))],
        out_shape=[jax.ShapeDtypeStruct((rows, t), I32),
                   jax.ShapeDtypeStruct((rows, LANES), I32)],
        compiler_params=_params("arbitrary"),
        name="ec_select",
    )(aff_t.reshape(rows, t))
    return pos_t.reshape(b, N_EXPERTS, t), starts.reshape(b, N_EXPERTS, LANES), aff


def _gather_stacked_kernel(pos_t_ref, h_ref, *rest, cap):
    xs_ref = rest[-1]
    t = h_ref.shape[1]
    pos_t = pos_t_ref[0]
    slot = lax.broadcasted_iota(I32, (cap, t), 0)
    onehot = jnp.concatenate([(slot == pos_t[e:e + 1, :]).astype(BF16) for e in range(N_EXPERTS)], axis=0)
    z = _dot(onehot, h_ref[0]).astype(BF16)
    for e in range(N_EXPERTS):
        xs_ref[e] = z[e * cap:(e + 1) * cap]


def _gather_stacked(pos_t, h, cap, rows, into=None, first_block=0):
    b, t, d = h.shape
    in_specs = [pl.BlockSpec((1, N_EXPERTS, t), lambda i: (i, 0, 0)),
                pl.BlockSpec((1, t, d), lambda i: (i, 0, 0))]
    args = [pos_t, h]
    if into is not None:
        assert into.shape == (N_EXPERTS, rows, d)
        in_specs.append(pl.BlockSpec(memory_space=pl.ANY))
        args.append(into)
    return pl.pallas_call(
        functools.partial(_gather_stacked_kernel, cap=cap),
        grid=(b,),
        in_specs=in_specs,
        out_specs=pl.BlockSpec((N_EXPERTS, cap, d), lambda i: (0, first_block + i, 0)),
        out_shape=jax.ShapeDtypeStruct((N_EXPERTS, rows, d), BF16),
        input_output_aliases={} if into is None else {2: 0},
        compiler_params=_params("arbitrary"),
        name="ec_gather_stacked",
    )(*args)


def _tile_windows(starts_ref, cap, ntp):
    i, j = pl.program_id(0), pl.program_id(1)
    wins, fits = [], None
    for e in range(N_EXPERTS):
        at = (i * N_EXPERTS + e) * ntp + j
        win = jnp.minimum((starts_ref[at] // SLOT_ALIGN) * SLOT_ALIGN, cap - SLOT_WINDOW)
        ok = starts_ref[at + 1] - win <= SLOT_WINDOW
        wins.append(pl.multiple_of(win, SLOT_ALIGN))
        fits = ok if fits is None else jnp.logical_and(fits, ok)
    return wins, fits


def _gather_win_kernel(starts_ref, pos_t_ref, h_ref, xs_ref, *, cap, ntp):
    w = SLOT_WINDOW
    tile = h_ref.shape[1]
    wins, fits = _tile_windows(starts_ref, cap, ntp)
    h = h_ref[0]
    pos_t = pos_t_ref[0]

    @pl.when(pl.program_id(1) == 0)
    def _():
        xs_ref[...] = jnp.zeros_like(xs_ref)

    @pl.when(fits)
    def _():
        row = lax.broadcasted_iota(I32, (w, tile), 0)
        onehot = jnp.concatenate([(row + wins[e] == pos_t[e:e + 1, :]).astype(BF16) for e in range(N_EXPERTS)],
                                 axis=0)
        z = _dot(onehot, h).astype(BF16)
        for e in range(N_EXPERTS):
            xs_ref[e, pl.ds(wins[e], w), :] += z[e * w:(e + 1) * w]

    @pl.when(jnp.logical_not(fits))
    def _():
        slot = lax.broadcasted_iota(I32, (cap, tile), 0)
        for e in range(N_EXPERTS):
            onehot = (slot == pos_t[e:e + 1, :]).astype(BF16)
            xs_ref[e] += _dot(onehot, h).astype(BF16)


def _gather_windowed(pos_t, starts, h, cap, rows):
    b, t, d = h.shape
    tile = min(TOKEN_TILE, t)
    nt = t // tile
    grid_spec = pltpu.PrefetchScalarGridSpec(
        num_scalar_prefetch=1,
        grid=(b, nt),
        in_specs=[pl.BlockSpec((1, N_EXPERTS, tile), lambda i, j, s: (i, 0, j)),
                  pl.BlockSpec((1, tile, d), lambda i, j, s: (i, j, 0))],
        out_specs=pl.BlockSpec((N_EXPERTS, cap, d), lambda i, j, s: (0, i, 0)),
    )
    return pl.pallas_call(
        functools.partial(_gather_win_kernel, cap=cap, ntp=nt + 1),
        grid_spec=grid_spec,
        out_shape=jax.ShapeDtypeStruct((N_EXPERTS, rows, d), BF16),
        compiler_params=_params("arbitrary", "arbitrary"),
        name="ec_gather_win",
    )(starts[:, :, :nt + 1].reshape(-1), pos_t, h)


def _ffn_up_kernel(xs_ref, wg_ref, wu_ref, hm_ref):
    xs = xs_ref[0]
    a = _dot(xs, wg_ref[0, 0].astype(BF16))
    u = _dot(xs, wu_ref[0, 0].astype(BF16))
    hm_ref[0] = (_silu(a) * u).astype(hm_ref.dtype)


def _ffn_down_kernel(hm_ref, wd_ref, y_ref, wd_bf16_ref):
    @pl.when(pl.program_id(1) == 0)
    def _():
        wd_bf16_ref[...] = wd_ref[0, 0].astype(BF16)

    y_ref[0] = _dot(hm_ref[0], wd_bf16_ref[...]).astype(y_ref.dtype)


def _largest_divisor(rows, limit):
    if rows <= limit:
        return rows
    tile = (limit // 256) * 256
    while tile and rows % tile:
        tile -= 256
    assert tile, f"no 256-multiple tile <= {limit} divides {rows} rows"
    return tile


def _expert_ffn(xs, w_gate, w_up, w_down, layer):
    n_exp, rows, d = xs.shape
    ff = w_gate.shape[-1]
    tm = _largest_divisor(rows, MOE_ROW_TILE)
    tf = min(FF_TILE, ff)
    if rows <= FEW_ROWS and ff % FF_TILE_FEW_ROWS == 0:
        tf = FF_TILE_FEW_ROWS
    hm = pl.pallas_call(
        _ffn_up_kernel,
        grid=(n_exp, rows // tm, ff // tf),
        in_specs=[pl.BlockSpec((1, tm, d), lambda e, m, f: (e, m, 0)),
                  pl.BlockSpec((1, 1, d, tf), lambda e, m, f: (layer, e, 0, f)),
                  pl.BlockSpec((1, 1, d, tf), lambda e, m, f: (layer, e, 0, f))],
        out_specs=pl.BlockSpec((1, tm, tf), lambda e, m, f: (e, m, f)),
        out_shape=jax.ShapeDtypeStruct((n_exp, rows, ff), BF16),
        compiler_params=_params("arbitrary", "arbitrary", "arbitrary"),
        name="ec_ffn_up",
    )(xs, w_gate, w_up)
    tm2 = _largest_divisor(rows, MOE_DOWN_ROW_TILE)
    return pl.pallas_call(
        _ffn_down_kernel,
        grid=(n_exp, rows // tm2),
        in_specs=[pl.BlockSpec((1, tm2, ff), lambda e, m: (e, m, 0)),
                  pl.BlockSpec((1, 1, ff, d), lambda e, m: (layer, e, 0, 0))],
        out_specs=pl.BlockSpec((1, tm2, d), lambda e, m: (e, m, 0)),
        out_shape=jax.ShapeDtypeStruct((n_exp, rows, d), BF16),
        scratch_shapes=[pltpu.VMEM((ff, d), BF16)],
        compiler_params=_params("arbitrary", "arbitrary"),
        name="ec_ffn_down",
    )(hm, w_down)


def _combine_kernel(x_ref, g2_ref, pos_t_ref, aff_ref, y_ref, fg_ref, o_ref, *, cap, final_norm):
    tt = x_ref.shape[1]
    pos = _token_major(pos_t_ref)
    aff = aff_ref[0]
    lane = lax.broadcasted_iota(I32, (tt, cap), 1)
    acc = jnp.zeros(x_ref.shape[1:], F32)
    for e in range(N_EXPERTS):
        onehot = (lane == pos[:, e:e + 1]).astype(BF16)
        acc = acc + aff[:, e:e + 1] * _dot(onehot, y_ref[e])
    _finish_combine(x_ref, g2_ref, fg_ref, o_ref, acc, final_norm)


def _combine(x1, g2, pos_t, aff, y, final_g, cap, final_norm, first_block):
    b, t, d = x1.shape
    tt = min(TOKEN_TILE, t)
    row = lambda i, j: (i, j, 0)
    return pl.pallas_call(
        functools.partial(_combine_kernel, cap=cap, final_norm=final_norm),
        grid=(b, t // tt),
        in_specs=[pl.BlockSpec((1, tt, d), row),
                  pl.BlockSpec((1, 1, d), lambda i, j: (i, 0, 0)),
                  pl.BlockSpec((1, N_EXPERTS, tt), lambda i, j: (i, 0, j)),
                  pl.BlockSpec((1, tt, LANES), row),
                  pl.BlockSpec((N_EXPERTS, cap, d), lambda i, j: (0, first_block + i, 0)),
                  pl.BlockSpec((1, d), lambda i, j: (0, 0))],
        out_specs=pl.BlockSpec((1, tt, d), row),
        out_shape=jax.ShapeDtypeStruct((b, t, d), F32),
        compiler_params=_params("arbitrary", "arbitrary"),
        name="ec_combine",
    )(x1, g2, pos_t, aff, y, final_g)


def _token_major(pos_t_ref):
    pos_t = pos_t_ref[0].astype(F32)
    pad = jnp.full((LANES - N_EXPERTS, pos_t.shape[1]), -1.0, F32)
    return jnp.concatenate([pos_t, pad], axis=0).T.astype(I32)


def _finish_combine(x_ref, g2_ref, fg_ref, o_ref, acc, final_norm):
    x2 = x_ref[0] + g2_ref[0] * acc
    if final_norm:
        x2 = (x2 * lax.rsqrt(jnp.mean(x2 * x2, axis=-1, keepdims=True) + NORM_EPS)) * fg_ref[...]
    o_ref[0] = x2


def _combine_win_kernel(starts_ref, x_ref, g2_ref, pos_t_ref, aff_ref, y_ref, fg_ref, o_ref,
                        *, cap, ntp, final_norm):
    w = SLOT_WINDOW
    assert w & (w - 1) == 0, "window column index is taken with a bit mask"
    tile = x_ref.shape[1]
    wins, fits = _tile_windows(starts_ref, cap, ntp)
    pos = _token_major(pos_t_ref)
    aff = aff_ref[0]

    @pl.when(fits)
    def _():
        ywin = jnp.concatenate([y_ref[e, pl.ds(wins[e], w), :] for e in range(N_EXPERTS)], axis=0)
        lane = lax.broadcasted_iota(I32, (1, LANES), 1)
        win_row = jnp.zeros((1, LANES), I32)
        for e in range(N_EXPERTS):
            win_row = jnp.where(lane == e, wins[e], win_row)
        rel = jnp.where(pos >= 0, jnp.clip(pos - win_row, -1, w), -1)
        src = lax.broadcasted_iota(I32, (LANES, N_EXPERTS * w), 0) * w
        col = lax.broadcasted_iota(I32, (LANES, N_EXPERTS * w), 1)
        spread = ((col >= src) & (col < src + w)).astype(BF16)
        wide = _dot(jnp.concatenate([rel.astype(F32).astype(BF16), aff.astype(BF16)], axis=0), spread)
        in_win = lax.broadcasted_iota(I32, (1, N_EXPERTS * w), 1) & (w - 1)
        gates = jnp.where(wide[:tile] == in_win.astype(F32), wide[tile:], 0.0).astype(BF16)
        _finish_combine(x_ref, g2_ref, fg_ref, o_ref, _dot(gates, ywin), final_norm)

    @pl.when(jnp.logical_not(fits))
    def _():
        lane = lax.broadcasted_iota(I32, (tile, cap), 1)
        acc = jnp.zeros(x_ref.shape[1:], F32)
        for e in range(N_EXPERTS):
            onehot = (lane == pos[:, e:e + 1]).astype(BF16)
            acc = acc + aff[:, e:e + 1] * _dot(onehot, y_ref[e])
        _finish_combine(x_ref, g2_ref, fg_ref, o_ref, acc, final_norm)


def _combine_windowed(x1, g2, pos_t, starts, aff, y, final_g, cap, final_norm):
    b, t, d = x1.shape
    tile = min(TOKEN_TILE, t)
    nt = t // tile
    row = lambda i, j, s: (i, j, 0)
    grid_spec = pltpu.PrefetchScalarGridSpec(
        num_scalar_prefetch=1,
        grid=(b, nt),
        in_specs=[pl.BlockSpec((1, tile, d), row),
                  pl.BlockSpec((1, 1, d), lambda i, j, s: (i, 0, 0)),
                  pl.BlockSpec((1, N_EXPERTS, tile), lambda i, j, s: (i, 0, j)),
                  pl.BlockSpec((1, tile, LANES), row),
                  pl.BlockSpec((N_EXPERTS, cap, d), lambda i, j, s: (0, i, 0)),
                  pl.BlockSpec((1, d), lambda i, j, s: (0, 0))],
        out_specs=pl.BlockSpec((1, tile, d), row),
    )
    return pl.pallas_call(
        functools.partial(_combine_win_kernel, cap=cap, ntp=nt + 1, final_norm=final_norm),
        grid_spec=grid_spec,
        out_shape=jax.ShapeDtypeStruct((b, t, d), F32),
        compiler_params=_params("arbitrary", "arbitrary"),
        name="ec_combine_win",
    )(starts[:, :, :nt + 1].reshape(-1), x1, g2, pos_t, aff, y, final_g)


def _ec_moe(streams, w_gate, w_up, w_down, layer, final_g, final_norm):
    b, _, d = streams[0][0].shape
    caps = [CAPACITY_FACTOR * s[0].shape[1] // N_EXPERTS for s in streams]
    rows = sum(b * cap for cap in caps)
    routes, first_blocks, xs = [], [], None
    row0 = 0
    for k, ((x1, h2, logits, g2), cap) in enumerate(zip(streams, caps)):
        t = x1.shape[1]
        assert row0 % cap == 0 and cap % SLOT_ALIGN == 0, "a stream's row blocks must tile the slot buffer"
        pos_t, starts, aff = _route(logits, cap)
        windowed = k == 0 and t > TOKEN_TILE and cap >= SLOT_WINDOW and (cap - SLOT_WINDOW) % SLOT_ALIGN == 0
        if windowed:
            xs = _gather_windowed(pos_t, starts, h2, cap, rows)
        else:
            xs = _gather_stacked(pos_t, h2, cap, rows, into=xs, first_block=row0 // cap)
        routes.append((pos_t, starts, aff, windowed))
        first_blocks.append(row0 // cap)
        row0 += b * cap
    y = _expert_ffn(xs, w_gate, w_up, w_down, layer)
    outs = []
    for (x1, h2, logits, g2), cap, (pos_t, starts, aff, windowed), first in zip(streams, caps, routes, first_blocks):
        if windowed:
            outs.append(_combine_windowed(x1, g2, pos_t, starts, aff, y, final_g, cap, final_norm))
        else:
            outs.append(_combine(x1, g2, pos_t, aff, y, final_g, cap, final_norm, first))
    return outs


def _pre_gla_kernel(x_ref, g_ref, sh_ref, sc_ref, w_ref, wlr_ref, w2_ref, gb_ref,
                    q_ref, r_ref, k_ref, v_ref, lgf_ref, lgb_ref):
    qk = q_ref.shape[-1]
    vd = v_ref.shape[-1]
    h = _norm_mod(x_ref[0], g_ref[...], sh_ref[0], sc_ref[0]).astype(BF16)
    p = _dot(h, w_ref[...])
    q_ref[0] = (p[:, :qk] * (qk // GLA_HEADS) ** -0.5).astype(BF16)
    r_ref[0] = _silu(p[:, qk:qk + vd]).astype(BF16)
    k_ref[0] = p[:, qk + vd:2 * qk + vd].astype(BF16)
    v_ref[0] = p[:, 2 * qk + vd:].astype(BF16)
    lr = _dot(h, wlr_ref[...]).astype(BF16)
    z = _dot(lr, w2_ref[...]) + gb_ref[...]
    lg = (jnp.minimum(z, 0.0) - jnp.log1p(jnp.exp(-jnp.abs(z)))) * (1.0 / GLA_TAU)
    lgf_ref[0] = lg[:, :qk]
    lgb_ref[0] = lg[:, qk:]


def _pre_gla(x, g, shift, scale, w_main, w_lr, w2, gate_b):
    b, t, d = x.shape
    n_main = w_main.shape[1]
    qk = w2.shape[1] // 2
    vd = (n_main - 2 * qk) // 2
    tm = min(ROW_TILE, t)
    row = lambda i, j: (i, j, 0)
    per_b = lambda i, j: (i, 0, 0)
    fixed = lambda i, j: (0, 0)
    out = lambda n, dt: (pl.BlockSpec((1, tm, n), row), jax.ShapeDtypeStruct((b, t, n), dt))
    outs = [out(qk, BF16), out(vd, BF16), out(qk, BF16), out(vd, BF16), out(qk, F32), out(qk, F32)]
    return pl.pallas_call(
        _pre_gla_kernel,
        grid=(b, t // tm),
        in_specs=[pl.BlockSpec((1, tm, d), row),
                  pl.BlockSpec((1, d), fixed),
                  pl.BlockSpec((1, 1, d), per_b),
                  pl.BlockSpec((1, 1, d), per_b),
                  pl.BlockSpec((d, n_main), fixed),
                  pl.BlockSpec((d, LANES), fixed),
                  pl.BlockSpec((LANES, 2 * qk), fixed),
                  pl.BlockSpec((1, 2 * qk), fixed)],
        out_specs=[o[0] for o in outs],
        out_shape=[o[1] for o in outs],
        compiler_params=_params("arbitrary", "arbitrary"),
        name="gla_in_proj",
    )(x, g, shift, scale, w_main, w_lr, w2, gate_b)


def _gla_masks(rows):
    ri = lax.broadcasted_iota(I32, (rows, rows), 0)
    ci = lax.broadcasted_iota(I32, (rows, rows), 1)
    same = (ri // GLA_CHUNK) == (ci // GLA_CHUNK)
    return same & (ri >= ci), same & (ri <= ci)


def _gla_tiles(jobs):
    ch = GLA_CHUNK
    n = len(jobs)
    rows, dk = jobs[0][1].shape
    n_chunks = rows // ch
    last = [[c * ch + (ch - 1 if job[6] else 0) for c in range(n_chunks)] for job in jobs]
    b = []
    for q, k, v, lg, st_ref, mask, forward in jobs:
        lg_hi, lg_lo = _split2(lg)
        m = mask.astype(BF16)
        b.append(_dot(m, lg_hi) + _dot(m, lg_lo))
    qt, kt, kd = [None] * n, [None] * n, [None] * n
    for i, (q, k, v, lg, st_ref, mask, forward) in enumerate(jobs):
        b_last = jnp.concatenate([jnp.broadcast_to(b[i][r:r + 1, :], (ch, dk)) for r in last[i]], axis=0)
        kf = k.astype(F32)
        kd[i] = (kf * jnp.exp(b_last - b[i])).astype(BF16)
        if q is not None:
            qt[i] = (q.astype(F32) * jnp.exp(b[i])).astype(BF16)
            kt[i] = (kf * jnp.exp(-b[i])).astype(BF16)
    a = [None if jobs[i][0] is None else lax.dot_general(qt[i], kt[i], NT_DIMS, preferred_element_type=F32)
         for i in range(n)]
    a = [None if a[i] is None else jnp.where(jobs[i][5], a[i], 0.0).astype(BF16) for i in range(n)]
    o_local = [None if a[i] is None else _dot(a[i], jobs[i][2]) for i in range(n)]
    d_st = [[lax.dot_general(jobs[i][2][c * ch:(c + 1) * ch], kd[i][c * ch:(c + 1) * ch], TN_DIMS,
                             preferred_element_type=F32) for c in range(n_chunks)] for i in range(n)]
    results = []
    for i, (q, k, v, lg, st_ref, mask, forward) in enumerate(jobs):
        st = st_ref[...]
        outs = [None] * n_chunks
        for c in (range(n_chunks) if forward else reversed(range(n_chunks))):
            rs = slice(c * ch, (c + 1) * ch)
            if q is not None:
                outs[c] = o_local[i][rs] + lax.dot_general(qt[i][rs], st.astype(BF16), NT_DIMS,
                                                           preferred_element_type=F32)
            st = jnp.exp(b[i][last[i][c]:last[i][c] + 1, :]) * st + d_st[i][c]
        st_ref[...] = st
        results.append(None if q is None else jnp.concatenate(outs, axis=0))
    return results


def _gla_kernel(q_ref, k_ref, v_ref, lgf_ref, lgb_ref, kc_ref, vc_ref, lgcf_ref, lgcb_ref, r_ref, ng_ref,
                o_ref, of_ref, ob_ref, stf_ref, stb_ref):
    t = q_ref.shape[1]
    t_ctx = kc_ref.shape[1]
    n_heads = stf_ref.shape[0]
    dv, dk = stf_ref.shape[1:]
    tile = min(GLA_TILE, t)
    ctile = min(GLA_TILE, t_ctx)
    mask_f, mask_b = _gla_masks(tile)
    cmask_f, cmask_b = (mask_f, mask_b) if ctile == tile else _gla_masks(ctile)
    stf_ref[...] = jnp.zeros_like(stf_ref)
    stb_ref[...] = jnp.zeros_like(stb_ref)
    kcols = [slice(h * dk, (h + 1) * dk) for h in range(n_heads)]
    vcols = [slice(h * dv, (h + 1) * dv) for h in range(n_heads)]

    n_ctx = t_ctx // ctile
    for i in range(n_ctx):
        rf = slice(i * ctile, (i + 1) * ctile)
        rb = slice((n_ctx - 1 - i) * ctile, (n_ctx - i) * ctile)
        jobs = []
        for h in range(n_heads):
            jobs.append((None, kc_ref[0, rf, kcols[h]], vc_ref[0, rf, vcols[h]], lgcf_ref[0, rf, kcols[h]],
                         stf_ref.at[h], cmask_f, True))
            jobs.append((None, kc_ref[0, rb, kcols[h]], vc_ref[0, rb, vcols[h]], lgcb_ref[0, rb, kcols[h]],
                         stb_ref.at[h], cmask_b, False))
        _gla_tiles(jobs)

    n = t // tile

    def body(i, carry):
        rf = pl.ds(pl.multiple_of(i * tile, tile), tile)
        rb = pl.ds(pl.multiple_of((n - 1 - i) * tile, tile), tile)
        jobs = []
        for h in range(n_heads):
            jobs.append((q_ref[0, rf, kcols[h]], k_ref[0, rf, kcols[h]], v_ref[0, rf, vcols[h]],
                         lgf_ref[0, rf, kcols[h]], stf_ref.at[h], mask_f, True))
            jobs.append((q_ref[0, rb, kcols[h]], k_ref[0, rb, kcols[h]], v_ref[0, rb, vcols[h]],
                         lgb_ref[0, rb, kcols[h]], stb_ref.at[h], mask_b, False))
        outs = _gla_tiles(jobs)
        for h in range(n_heads):
            of_ref[rf, vcols[h]] = outs[2 * h]
            ob_ref[rb, vcols[h]] = outs[2 * h + 1]
        return carry

    lax.fori_loop(0, n, body, 0)

    def readout(i, carry):
        rows = pl.ds(pl.multiple_of(i * tile, tile), tile)
        for h in range(n_heads):
            o = of_ref[rows, vcols[h]] + ob_ref[rows, vcols[h]]
            o = o * lax.rsqrt(jnp.mean(o * o, axis=-1, keepdims=True) + NORM_EPS) * ng_ref[:, vcols[h]]
            o_ref[0, rows, vcols[h]] = (o * r_ref[0, rows, vcols[h]].astype(F32)).astype(o_ref.dtype)
        return carry

    lax.fori_loop(0, n, readout, 0)


def _gla(q, k, v, lgf, lgb, kc, vc, lgcf, lgcb, r, norm_g):
    b, t, qk = q.shape
    vd = v.shape[-1]
    t_ctx = kc.shape[1]
    dk, dv = qk // GLA_HEADS, vd // GLA_HEADS
    hp = GLA_HEADS_PER_STEP
    head = lambda i, h: (i, 0, h)
    lat = lambda n: pl.BlockSpec((1, t, hp * n), head)
    ctx = lambda n: pl.BlockSpec((1, t_ctx, hp * n), head)
    return pl.pallas_call(
        _gla_kernel,
        grid=(b, GLA_HEADS // hp),
        in_specs=[lat(dk), lat(dk), lat(dv), lat(dk), lat(dk),
                  ctx(dk), ctx(dv), ctx(dk), ctx(dk),
                  lat(dv),
                  pl.BlockSpec((1, hp * dv), lambda i, h: (0, h))],
        out_specs=lat(dv),
        out_shape=jax.ShapeDtypeStruct((b, t, vd), BF16),
        scratch_shapes=[pltpu.VMEM((t, hp * dv), F32), pltpu.VMEM((t, hp * dv), F32),
                        pltpu.VMEM((hp, dv, dk), F32), pltpu.VMEM((hp, dv, dk), F32)],
        compiler_params=_params("arbitrary", "arbitrary"),
        name="gla_scan",
    )(q, k, v, lgf, lgb, kc, vc, lgcf, lgcb, r, norm_g)


def _rope_tables(t):
    pos = jnp.arange(t)
    row = (pos // GRID_W).astype(F32)
    col = (pos % GRID_W).astype(F32)
    inv = ROPE_BASE ** (-jnp.arange(ROPE_PAIRS, dtype=F32) / ROPE_PAIRS)
    ar, ac = row[:, None] * inv, col[:, None] * inv
    ang = jnp.concatenate([ar, ar, ac, ac], axis=-1)
    sign = jnp.tile(jnp.repeat(jnp.array([-1.0, 1.0], F32), ROPE_PAIRS), 2)
    reps = LANES // HEAD_DIM
    return jnp.tile(jnp.cos(ang), (1, reps)), jnp.tile(jnp.sin(ang) * sign, (1, reps))


def _router_split(w):
    d, n = w.shape
    w = jnp.pad(w, ((0, 0), (0, LANES - n)))
    hi = w.astype(BF16)
    return hi, (w - hi.astype(F32)).astype(BF16)


def kernel(x, c, ctx, c_ctx, ada_w, ada_b, norm1_g, norm2_g, attn_w_in, attn_w_out, attn_sink, gla_w_in, gla_gate_w2, gla_gate_b, gla_norm_g, gla_w_out, router_w, exp_w_gate, exp_w_up, exp_w_down, final_norm_g):
    b, t, d = x.shape
    depth = ada_w.shape[0]
    assert depth == 2, "layer 0 is windowed attention with context output, layer 1 is GLA and last"
    rows = -(-(b + 1) // 16) * 16
    src = jnp.concatenate([c, c_ctx[None, :], jnp.zeros((rows - b - 1, d), F32)], axis=0)
    cos, sin_signed = _rope_tables(t)
    final_g = final_norm_g.reshape(1, d)

    def chunks(i):
        mod = _modulation(src, ada_w, ada_b, i)
        lat = [mod[:b, j * d:(j + 1) * d].reshape(b, 1, d) for j in range(6)]
        cx = [jnp.broadcast_to(mod[b, j * d:(j + 1) * d].reshape(1, 1, d), (b, 1, d)) for j in range(6)]
        return lat, cx

    (sh1, sc1, g1, sh2, sc2, g2), (sh1c, sc1c, g1c, sh2c, sc2c, g2c) = chunks(0)
    n1, n2 = norm1_g[0].reshape(1, d), norm2_g[0].reshape(1, d)
    w_in = attn_w_in[0].astype(BF16)
    w_out = attn_w_out[0].astype(BF16)
    sink = attn_sink[0].reshape(1, -1)
    wr_hi, wr_lo = _router_split(router_w[0])
    q, k, v = _pre_attn(x, n1, sh1, sc1, w_in, cos, sin_signed, rope=True)
    t_ctx = ctx.shape[1]
    qc, kc, vc = _pre_attn(ctx, n1, sh1c, sc1c, w_in, cos[:t_ctx], sin_signed[:t_ctx], rope=False)
    o = _attn_band(q, k, v, kc, vc, sink)
    oc = _attn_ctx(qc, kc, vc, sink)
    x1, h2, logits = _post_mixer(o, w_out, x, g1, n2, sh2, sc2, wr_hi, wr_lo)
    xc1, hc2, logits_c = _post_mixer(oc, w_out, ctx, g1c, n2, sh2c, sc2c, wr_hi, wr_lo)
    x, xc = _ec_moe([(x1, h2, logits, g2), (xc1, hc2, logits_c, g2c)],
                    exp_w_gate, exp_w_up, exp_w_down, 0, final_g, False)

    (sh1, sc1, g1, sh2, sc2, g2), (sh1c, sc1c, _, _, _, _) = chunks(1)
    n1, n2 = norm1_g[1].reshape(1, d), norm2_g[1].reshape(1, d)
    qk = gla_gate_w2.shape[-1]
    n_main = gla_w_in.shape[-1] - 2 * GLA_RANK
    w_main = gla_w_in[0][:, :n_main].astype(BF16)
    w_lr = jnp.pad(gla_w_in[0][:, n_main:], ((0, 0), (0, LANES - 2 * GLA_RANK))).astype(BF16)
    w2 = jnp.zeros((LANES, 2 * qk), F32)
    w2 = w2.at[:GLA_RANK, :qk].set(gla_gate_w2[0, 0]).at[GLA_RANK:2 * GLA_RANK, qk:].set(gla_gate_w2[0, 1])
    w2 = w2.astype(BF16)
    gate_b = gla_gate_b[0].reshape(1, 2 * qk)
    gq, gr, gk, gv, lgf, lgb = _pre_gla(x, n1, sh1, sc1, w_main, w_lr, w2, gate_b)
    _, _, gkc, gvc, lgcf, lgcb = _pre_gla(xc, n1, sh1c, sc1c, w_main, w_lr, w2, gate_b)
    og = _gla(gq, gk, gv, lgf, lgb, gkc, gvc, lgcf, lgcb, gr, gla_norm_g[0].reshape(1, -1))
    wr_hi, wr_lo = _router_split(router_w[1])
    x1, h2, logits = _post_mixer(og, gla_w_out[0].astype(BF16), x, g1, n2, sh2, sc2, wr_hi, wr_lo)
    return _ec_moe([(x1, h2, logits, g2)], exp_w_gate, exp_w_up, exp_w_down, 1, final_g, True)[0]
```

```python
import functools

import jax
import jax.numpy as jnp
from jax import lax
from jax.experimental import pallas as pl
from jax.experimental.pallas import tpu as pltpu

F32 = jnp.float32
BF16 = jnp.bfloat16
I32 = jnp.int32

LANES = 128
HEAD_DIM = 64
ATT_GROUP = 4
ATT_BLOCK = 128
GRID_W = 64
ROPE_BASE = 10000.0
ROPE_PAIRS = HEAD_DIM // 4
GLA_HEADS = 4
GLA_RANK = 16
GLA_TAU = 16.0
GLA_CHUNK = 64
N_EXPERTS = 16
CAPACITY_FACTOR = 2
NORM_EPS = 1e-6
LOG2E = 1.4426950408889634
ROW_TILE = 512
ATTN_IN_ROW_TILE = 512
ATTN_IN_SUBTILE_ROWS = 256
GLA_IN_ROW_TILE = 512
GLA_IN_SUBTILE_ROWS = 256
POST_SUBTILES = 2
TOKEN_TILE = 256
MOE_ROW_TILE = 4608
MOE_DOWN_ROW_TILE = 1024
FF_TILE = 256
FEW_ROWS = 512
FF_TILE_FEW_ROWS = 1408
GLA_TILE = 256
GLA_HEADS_PER_STEP = 2
SLOT_WINDOW = 64
SLOT_ALIGN = 16
VMEM_LIMIT = 56 * 1024 * 1024

NT_DIMS = (((1,), (1,)), ((), ()))
TN_DIMS = (((0,), (0,)), ((), ()))


def _params(*sem):
    return pltpu.CompilerParams(dimension_semantics=sem, vmem_limit_bytes=VMEM_LIMIT)


def _dot(a, b):
    return jnp.dot(a, b, preferred_element_type=F32)


def _split2(a):
    hi = a.astype(BF16)
    lo = (a - hi.astype(F32)).astype(BF16)
    return hi, lo


def _dot3(a, b_hi, b_lo):
    a_hi, a_lo = _split2(a)
    return _dot(a_hi, b_hi) + _dot(a_hi, b_lo) + _dot(a_lo, b_hi)


def _silu(a):
    return a * jax.nn.sigmoid(a)


def _norm_mod(x, g, shift, scale):
    y = x * lax.rsqrt(jnp.mean(x * x, axis=-1, keepdims=True) + NORM_EPS)
    return (y * g) * (1.0 + scale) + shift


def _mod_kernel(src_ref, w_ref, b_ref, o_ref):
    w_hi, w_lo = _split2(w_ref[0])
    o_ref[...] = _dot3(_silu(src_ref[...]), w_hi, w_lo) + b_ref[0]


def _modulation(src, ada_w, ada_b, layer):
    rows, d = src.shape
    depth, _, n = ada_w.shape
    tn = 512
    return pl.pallas_call(
        _mod_kernel,
        grid=(n // tn,),
        in_specs=[pl.BlockSpec((rows, d), lambda j: (0, 0)),
                  pl.BlockSpec((1, d, tn), lambda j: (layer, 0, j)),
                  pl.BlockSpec((1, 1, tn), lambda j: (layer, 0, j))],
        out_specs=pl.BlockSpec((rows, tn), lambda j: (0, j)),
        out_shape=jax.ShapeDtypeStruct((rows, n), F32),
        compiler_params=_params("arbitrary"),
        name="adaln_mod",
    )(src, ada_w, ada_b.reshape(depth, 1, n))


def _rope(x, cos, sin_signed, lane_lo):
    outs = []
    for j in range(x.shape[1] // LANES):
        xb = x[:, j * LANES:(j + 1) * LANES]
        partner = jnp.where(lane_lo, pltpu.roll(xb, LANES - ROPE_PAIRS, 1), pltpu.roll(xb, ROPE_PAIRS, 1))
        outs.append(xb * cos + partner * sin_signed)
    return jnp.concatenate(outs, axis=1)


def _pre_attn_kernel(x_ref, g_ref, sh_ref, sc_ref, w_ref, cos_ref, sin_ref, q_ref, k_ref, v_ref, *, rope):
    d = x_ref.shape[-1]
    kv = k_ref.shape[-1]
    tm = x_ref.shape[1]
    n_sub = max(1, tm // ATTN_IN_SUBTILE_ROWS)
    subs = [slice(i * tm // n_sub, (i + 1) * tm // n_sub) for i in range(n_sub)]
    hs = [_norm_mod(x_ref[0, rs, :], g_ref[...], sh_ref[0], sc_ref[0]).astype(BF16) for rs in subs]
    ps = [_dot(h, w_ref[...]) for h in hs]
    for rs, p in zip(subs, ps):
        q, k, v = p[:, :d], p[:, d:d + kv], p[:, d + kv:]
        if rope:
            lane = lax.broadcasted_iota(I32, (tm // n_sub, LANES), 1)
            lane_lo = (lane & (2 * ROPE_PAIRS - 1)) < ROPE_PAIRS
            cos, sin_signed = cos_ref[rs, :], sin_ref[rs, :]
            q = _rope(q, cos, sin_signed, lane_lo)
            k = _rope(k, cos, sin_signed, lane_lo)
        q_ref[0, rs, :] = (q * (HEAD_DIM ** -0.5 * LOG2E)).astype(BF16)
        k_ref[0, rs, :] = k.astype(BF16)
        v_ref[0, rs, :] = v.astype(BF16)


def _pre_attn(x, g, shift, scale, w_in, cos, sin_signed, rope):
    b, t, d = x.shape
    n_in = w_in.shape[1]
    kv = (n_in - d) // 2
    tm = min(ATTN_IN_ROW_TILE, t)
    row = lambda i, j: (i, j, 0)
    per_b = lambda i, j: (i, 0, 0)
    fixed = lambda i, j: (0, 0)
    return pl.pallas_call(
        functools.partial(_pre_attn_kernel, rope=rope),
        grid=(b, t // tm),
        in_specs=[pl.BlockSpec((1, tm, d), row),
                  pl.BlockSpec((1, d), fixed),
                  pl.BlockSpec((1, 1, d), per_b),
                  pl.BlockSpec((1, 1, d), per_b),
                  pl.BlockSpec((d, n_in), fixed),
                  pl.BlockSpec((tm, LANES), lambda i, j: (j, 0)),
                  pl.BlockSpec((tm, LANES), lambda i, j: (j, 0))],
        out_specs=[pl.BlockSpec((1, tm, d), row),
                   pl.BlockSpec((1, tm, kv), row),
                   pl.BlockSpec((1, tm, kv), row)],
        out_shape=[jax.ShapeDtypeStruct((b, t, d), BF16),
                   jax.ShapeDtypeStruct((b, t, kv), BF16),
                   jax.ShapeDtypeStruct((b, t, kv), BF16)],
        compiler_params=_params("arbitrary", "arbitrary"),
        name="attn_in_proj",
    )(x, g, shift, scale, w_in, cos, sin_signed)


def _attn_heads(q, k_parts, v_parts, bias_parts, sink_ref, o_ref):
    tq = q.shape[0]
    n_kv = k_parts[0].shape[1] // HEAD_DIM
    lane = lax.broadcasted_iota(I32, (1, ATT_GROUP * tq), 1)
    v_lane = lax.broadcasted_iota(I32, (1, LANES), 1)
    sks, scores, ms, es, both = ([None] * n_kv for _ in range(5))

    def score_stage(h):
        heads = [ATT_GROUP * h + g for g in range(ATT_GROUP)]
        qg = jnp.concatenate([q[:, i * HEAD_DIM:(i + 1) * HEAD_DIM] for i in heads], axis=0)
        sk = jnp.full((1, ATT_GROUP * tq), sink_ref[0, heads[0]], F32)
        for g in range(1, ATT_GROUP):
            sk = jnp.where(lane >= g * tq, sink_ref[0, heads[g]], sk)
        sks[h] = sk * LOG2E
        m = sks[h]
        scores[h] = []
        for kp, bias in zip(k_parts, bias_parts):
            s = lax.dot_general(kp[:, h * HEAD_DIM:(h + 1) * HEAD_DIM], qg, NT_DIMS, preferred_element_type=F32)
            s = s if bias is None else s + bias
            m = jnp.maximum(m, jnp.max(s, axis=0, keepdims=True))
            scores[h].append(s)
        ms[h] = m

    def weight_stage(h):
        es[h] = jnp.concatenate([jnp.exp2(s - ms[h]).astype(BF16) for s in scores[h]], axis=0)

    def value_stage(h):
        block = (h * HEAD_DIM // LANES) * LANES
        upper = (h * HEAD_DIM) % LANES != 0
        v_pair = jnp.concatenate([vp[:, block:block + LANES] for vp in v_parts], axis=0)
        own = (v_lane >= HEAD_DIM) if upper else (v_lane < HEAD_DIM)
        v_aug = jnp.where(own, v_pair, jnp.ones_like(v_pair))
        both[h] = lax.dot_general(v_aug, es[h], TN_DIMS, preferred_element_type=F32)

    stages = (score_stage, weight_stage, value_stage)
    for step in range(n_kv + len(stages) - 1):
        for depth, stage in enumerate(stages):
            if 0 <= step - depth < n_kv:
                stage(step - depth)
    outs = []
    for h in range(n_kv):
        upper = (h * HEAD_DIM) % LANES != 0
        num = both[h][HEAD_DIM:] if upper else both[h][:HEAD_DIM]
        total = both[h][:1] if upper else both[h][HEAD_DIM:HEAD_DIM + 1]
        o_t = num / (total + jnp.exp2(sks[h] - ms[h]))
        outs.extend(o_t[:, g * tq:(g + 1) * tq] for g in range(ATT_GROUP))
    o_ref[0] = jnp.concatenate(outs, axis=0).T.astype(o_ref.dtype)


def _band_bias():
    blk = ATT_BLOCK
    key = jnp.arange(blk)[:, None]
    qry = jnp.arange(ATT_GROUP * blk)[None, :] % blk
    prev_ok = key >= qry
    next_ok = key <= qry
    never = jnp.zeros_like(prev_ok)
    variants = [jnp.stack([never if first else prev_ok, never if last else next_ok])
                for last in (False, True) for first in (False, True)]
    return jnp.where(jnp.stack(variants), 0.0, -jnp.inf).astype(F32)


def _attn_band_kernel(q_ref, kp_ref, ko_ref, kn_ref, vp_ref, vo_ref, vn_ref, kc_ref, vc_ref, bias_ref, sink_ref,
                      o_ref):
    _attn_heads(q_ref[0], [kc_ref[0], kp_ref[0], ko_ref[0], kn_ref[0]], [vc_ref[0], vp_ref[0], vo_ref[0], vn_ref[0]],
                [None, bias_ref[0, 0], None, bias_ref[0, 1]], sink_ref, o_ref)


def _attn_ctx_kernel(q_ref, kc_ref, vc_ref, sink_ref, o_ref):
    _attn_heads(q_ref[0], [kc_ref[0]], [vc_ref[0]], [None], sink_ref, o_ref)


def _attn_band(q, k, v, kc, vc, sink):
    b, t, d = q.shape
    kv = k.shape[-1]
    n_ctx = kc.shape[1]
    nb = t // ATT_BLOCK
    blk = lambda m: pl.BlockSpec((1, ATT_BLOCK, kv), m)
    prev = lambda i, j: (i, jnp.maximum(j - 1, 0), 0)
    own = lambda i, j: (i, j, 0)
    nxt = lambda i, j: (i, jnp.minimum(j + 1, nb - 1), 0)
    ctx_spec = pl.BlockSpec((1, n_ctx, kv), lambda i, j: (i, 0, 0))
    bias = _band_bias()
    edge = lambda i, j: ((j == 0).astype(I32) + 2 * (j == nb - 1).astype(I32), 0, 0, 0)
    return pl.pallas_call(
        _attn_band_kernel,
        grid=(b, nb),
        in_specs=[pl.BlockSpec((1, ATT_BLOCK, d), own),
                  blk(prev), blk(own), blk(nxt), blk(prev), blk(own), blk(nxt),
                  ctx_spec, ctx_spec,
                  pl.BlockSpec((1,) + bias.shape[1:], edge),
                  pl.BlockSpec(memory_space=pltpu.SMEM)],
        out_specs=pl.BlockSpec((1, ATT_BLOCK, d), own),
        out_shape=jax.ShapeDtypeStruct((b, t, d), BF16),
        compiler_params=_params("arbitrary", "arbitrary"),
        name="attn_band",
    )(q, k, k, k, v, v, v, kc, vc, bias, sink)


def _attn_ctx(qc, kc, vc, sink):
    b, n_ctx, d = qc.shape
    kv = kc.shape[-1]
    return pl.pallas_call(
        _attn_ctx_kernel,
        grid=(b,),
        in_specs=[pl.BlockSpec((1, n_ctx, d), lambda i: (i, 0, 0)),
                  pl.BlockSpec((1, n_ctx, kv), lambda i: (i, 0, 0)),
                  pl.BlockSpec((1, n_ctx, kv), lambda i: (i, 0, 0)),
                  pl.BlockSpec(memory_space=pltpu.SMEM)],
        out_specs=pl.BlockSpec((1, n_ctx, d), lambda i: (i, 0, 0)),
        out_shape=jax.ShapeDtypeStruct((b, n_ctx, d), BF16),
        compiler_params=_params("arbitrary"),
        name="attn_ctx",
    )(qc, kc, vc, sink)


def _post_kernel(o_ref, w_ref, x_ref, g1_ref, n2_ref, sh_ref, sc_ref, wr_hi_ref, wr_lo_ref,
                 x1_ref, h2_ref, aff_ref, aff_t_ref):
    tm = x_ref.shape[1]
    subs = [slice(i * tm // POST_SUBTILES, (i + 1) * tm // POST_SUBTILES) for i in range(POST_SUBTILES)]
    ys = [_dot(o_ref[0, rs, :], w_ref[...]) for rs in subs]
    h2s = []
    for rs, y in zip(subs, ys):
        x1 = x_ref[0, rs, :] + g1_ref[0] * y
        x1_ref[0, rs, :] = x1
        h2 = _norm_mod(x1, n2_ref[...], sh_ref[0], sc_ref[0])
        h2_ref[0, rs, :] = h2.astype(BF16)
        h2s.append(h2)
    lane = lax.broadcasted_iota(I32, (tm // POST_SUBTILES, LANES), 1)
    for rs, h2 in zip(subs, h2s):
        lg = jnp.where(lane < N_EXPERTS, _dot3(h2, wr_hi_ref[...], wr_lo_ref[...]), -jnp.inf)
        e = jnp.exp(lg - jnp.max(lg, axis=-1, keepdims=True))
        aff = e / jnp.sum(e, axis=-1, keepdims=True)
        aff_ref[0, rs, :] = aff
        aff_t_ref[0, :, rs] = aff.T[:N_EXPERTS]


def _post_mixer(o, w_out, x, g1, n2, shift, scale, wr_hi, wr_lo):
    b, t, d = x.shape
    k_in = o.shape[-1]
    tm = min(ROW_TILE, t)
    row = lambda i, j: (i, j, 0)
    per_b = lambda i, j: (i, 0, 0)
    fixed = lambda i, j: (0, 0)
    return pl.pallas_call(
        _post_kernel,
        grid=(b, t // tm),
        in_specs=[pl.BlockSpec((1, tm, k_in), row),
                  pl.BlockSpec((k_in, d), fixed),
                  pl.BlockSpec((1, tm, d), row),
                  pl.BlockSpec((1, 1, d), per_b),
                  pl.BlockSpec((1, d), fixed),
                  pl.BlockSpec((1, 1, d), per_b),
                  pl.BlockSpec((1, 1, d), per_b),
                  pl.BlockSpec((d, LANES), fixed),
                  pl.BlockSpec((d, LANES), fixed)],
        out_specs=[pl.BlockSpec((1, tm, d), row),
                   pl.BlockSpec((1, tm, d), row),
                   pl.BlockSpec((1, tm, LANES), row),
                   pl.BlockSpec((1, N_EXPERTS, tm), lambda i, j: (i, 0, j))],
        out_shape=[jax.ShapeDtypeStruct((b, t, d), F32),
                   jax.ShapeDtypeStruct((b, t, d), BF16),
                   jax.ShapeDtypeStruct((b, t, LANES), F32),
                   jax.ShapeDtypeStruct((b, N_EXPERTS, t), F32)],
        compiler_params=_params("arbitrary", "arbitrary"),
        name="mixer_out_proj",
    )(o, w_out, x, g1, n2, shift, scale, wr_hi, wr_lo)


def _exclusive_cumsum_lanes(mask):
    rows, t = mask.shape
    width = min(2 * LANES, t)
    r = lax.broadcasted_iota(I32, (width, width), 0)
    c = lax.broadcasted_iota(I32, (width, width), 1)
    upper = (r < c).astype(BF16)
    carry = jnp.zeros((rows, 1), F32)
    outs = []
    for ch in range(t // width):
        m = mask[:, ch * width:(ch + 1) * width]
        outs.append(_dot(m.astype(BF16), upper) + carry)
        carry = carry + jnp.sum(m, axis=1, keepdims=True)
    return jnp.concatenate(outs, axis=1)


LOG_FLOOR = -160.0
BISECT_LOG_STEPS = 40
BISECT_STEPS = 12


def _select_kernel(aff_t_ref, pos_t_ref, starts_ref, *, cap):
    rows = aff_t_ref.shape[0]

    def enough(threshold):
        return jnp.sum((aff_t_ref[...] >= threshold).astype(F32), axis=1, keepdims=True) >= cap

    def bisect_log(_, carry):
        lo_u, hi_u = carry
        mid = lo_u + (hi_u - lo_u) * 0.5
        ok = enough(jnp.exp2(mid))
        return jnp.where(ok, mid, lo_u), jnp.where(ok, hi_u, mid)

    def bisect(_, carry):
        lo, hi = carry
        mid = lo + (hi - lo) * 0.5
        ok = enough(mid)
        return jnp.where(ok, mid, lo), jnp.where(ok, hi, mid)

    lo_u, hi_u = lax.fori_loop(0, BISECT_LOG_STEPS, bisect_log,
                               (jnp.full((rows, 1), LOG_FLOOR, F32), jnp.full((rows, 1), 1.0, F32)))
    lo, hi = lax.fori_loop(0, BISECT_STEPS, bisect, (jnp.exp2(lo_u), jnp.exp2(hi_u)))
    aff_t = aff_t_ref[...]
    above = (aff_t >= hi).astype(F32)
    tied = ((aff_t >= lo) & (aff_t < hi)).astype(F32)
    need = cap - jnp.sum(above, axis=1, keepdims=True)
    sel = above + tied * (_exclusive_cumsum_lanes(tied) < need).astype(F32)
    slot = _exclusive_cumsum_lanes(sel)
    pos_t_ref[...] = jnp.where(sel > 0, slot, -1.0).astype(I32)
    t = aff_t.shape[1]
    tok = lax.broadcasted_iota(I32, (t, LANES), 0)
    edge = lax.broadcasted_iota(I32, (t, LANES), 1) * min(TOKEN_TILE, t)
    before = ((tok < edge) & (edge <= t)).astype(BF16)
    starts_ref[...] = _dot(sel.astype(BF16), before).astype(I32)


def _route(aff_t, cap):
    b, _, t = aff_t.shape
    rows = b * N_EXPERTS
    pos_t, starts = pl.pallas_call(
        functools.partial(_select_kernel, cap=cap),
        grid=(1,),
        in_specs=[pl.BlockSpec((rows, t), lambda i: (0, 0))],
        out_specs=[pl.BlockSpec((rows, t), lambda i: (0, 0)),
                   pl.BlockSpec((rows, LANES), lambda i: (0, 0))],
        out_shape=[jax.ShapeDtypeStruct((rows, t), I32),
                   jax.ShapeDtypeStruct((rows, LANES), I32)],
        compiler_params=_params("arbitrary"),
        name="ec_select",
    )(aff_t.reshape(rows, t))
    return pos_t.reshape(b, N_EXPERTS, t), starts.reshape(b, N_EXPERTS, LANES)


def _gather_stacked_kernel(pos_t_ref, h_ref, *rest, cap):
    xs_ref = rest[-1]
    t = h_ref.shape[1]
    pos_t = pos_t_ref[0]
    slot = lax.broadcasted_iota(I32, (cap, t), 0)
    onehot = jnp.concatenate([(slot == pos_t[e:e + 1, :]).astype(BF16) for e in range(N_EXPERTS)], axis=0)
    z = _dot(onehot, h_ref[0]).astype(BF16)
    for e in range(N_EXPERTS):
        xs_ref[e] = z[e * cap:(e + 1) * cap]


def _gather_stacked(pos_t, h, cap, rows, into=None, first_block=0):
    b, t, d = h.shape
    in_specs = [pl.BlockSpec((1, N_EXPERTS, t), lambda i: (i, 0, 0)),
                pl.BlockSpec((1, t, d), lambda i: (i, 0, 0))]
    args = [pos_t, h]
    if into is not None:
        assert into.shape == (N_EXPERTS, rows, d)
        in_specs.append(pl.BlockSpec(memory_space=pl.ANY))
        args.append(into)
    return pl.pallas_call(
        functools.partial(_gather_stacked_kernel, cap=cap),
        grid=(b,),
        in_specs=in_specs,
        out_specs=pl.BlockSpec((N_EXPERTS, cap, d), lambda i: (0, first_block + i, 0)),
        out_shape=jax.ShapeDtypeStruct((N_EXPERTS, rows, d), BF16),
        input_output_aliases={} if into is None else {2: 0},
        compiler_params=_params("arbitrary"),
        name="ec_gather_stacked",
    )(*args)


def _tile_windows(starts_ref, cap, ntp):
    i, j = pl.program_id(0), pl.program_id(1)
    wins, fits = [], None
    for e in range(N_EXPERTS):
        at = (i * N_EXPERTS + e) * ntp + j
        win = jnp.minimum((starts_ref[at] // SLOT_ALIGN) * SLOT_ALIGN, cap - SLOT_WINDOW)
        ok = starts_ref[at + 1] - win <= SLOT_WINDOW
        wins.append(pl.multiple_of(win, SLOT_ALIGN))
        fits = ok if fits is None else jnp.logical_and(fits, ok)
    return wins, fits


def _gather_win_kernel(starts_ref, pos_t_ref, h_ref, xs_ref, *, cap, ntp):
    w = SLOT_WINDOW
    tile = h_ref.shape[1]
    wins, fits = _tile_windows(starts_ref, cap, ntp)
    h = h_ref[0]
    pos_t = pos_t_ref[0]

    @pl.when(pl.program_id(1) == 0)
    def _():
        xs_ref[...] = jnp.zeros_like(xs_ref)

    @pl.when(fits)
    def _():
        row = lax.broadcasted_iota(I32, (w, tile), 0)
        onehot = jnp.concatenate([(row + wins[e] == pos_t[e:e + 1, :]).astype(BF16) for e in range(N_EXPERTS)],
                                 axis=0)
        z = _dot(onehot, h).astype(BF16)
        for e in range(N_EXPERTS):
            xs_ref[e, pl.ds(wins[e], w), :] += z[e * w:(e + 1) * w]

    @pl.when(jnp.logical_not(fits))
    def _():
        slot = lax.broadcasted_iota(I32, (cap, tile), 0)
        for e in range(N_EXPERTS):
            onehot = (slot == pos_t[e:e + 1, :]).astype(BF16)
            xs_ref[e] += _dot(onehot, h).astype(BF16)


def _gather_windowed(pos_t, starts, h, cap, rows):
    b, t, d = h.shape
    tile = min(TOKEN_TILE, t)
    nt = t // tile
    grid_spec = pltpu.PrefetchScalarGridSpec(
        num_scalar_prefetch=1,
        grid=(b, nt),
        in_specs=[pl.BlockSpec((1, N_EXPERTS, tile), lambda i, j, s: (i, 0, j)),
                  pl.BlockSpec((1, tile, d), lambda i, j, s: (i, j, 0))],
        out_specs=pl.BlockSpec((N_EXPERTS, cap, d), lambda i, j, s: (0, i, 0)),
    )
    return pl.pallas_call(
        functools.partial(_gather_win_kernel, cap=cap, ntp=nt + 1),
        grid_spec=grid_spec,
        out_shape=jax.ShapeDtypeStruct((N_EXPERTS, rows, d), BF16),
        compiler_params=_params("arbitrary", "arbitrary"),
        name="ec_gather_win",
    )(starts[:, :, :nt + 1].reshape(-1), pos_t, h)


def _ffn_up_kernel(xs_ref, wg_ref, wu_ref, hm_ref):
    xs = xs_ref[0]
    a = _dot(xs, wg_ref[0, 0].astype(BF16))
    u = _dot(xs, wu_ref[0, 0].astype(BF16))
    hm_ref[0] = (_silu(a) * u).astype(hm_ref.dtype)


def _ffn_down_kernel(hm_ref, wd_ref, y_ref, wd_bf16_ref):
    @pl.when(pl.program_id(1) == 0)
    def _():
        wd_bf16_ref[...] = wd_ref[0, 0].astype(BF16)

    y_ref[0] = _dot(hm_ref[0], wd_bf16_ref[...]).astype(y_ref.dtype)


def _largest_divisor(rows, limit):
    if rows <= limit:
        return rows
    tile = (limit // 256) * 256
    while tile and rows % tile:
        tile -= 256
    assert tile, f"no 256-multiple tile <= {limit} divides {rows} rows"
    return tile


def _expert_ffn(xs, w_gate, w_up, w_down, layer):
    n_exp, rows, d = xs.shape
    ff = w_gate.shape[-1]
    tm = _largest_divisor(rows, MOE_ROW_TILE)
    tf = min(FF_TILE, ff)
    if rows <= FEW_ROWS and ff % FF_TILE_FEW_ROWS == 0:
        tf = FF_TILE_FEW_ROWS
    hm = pl.pallas_call(
        _ffn_up_kernel,
        grid=(n_exp, rows // tm, ff // tf),
        in_specs=[pl.BlockSpec((1, tm, d), lambda e, m, f: (e, m, 0)),
                  pl.BlockSpec((1, 1, d, tf), lambda e, m, f: (layer, e, 0, f)),
                  pl.BlockSpec((1, 1, d, tf), lambda e, m, f: (layer, e, 0, f))],
        out_specs=pl.BlockSpec((1, tm, tf), lambda e, m, f: (e, m, f)),
        out_shape=jax.ShapeDtypeStruct((n_exp, rows, ff), BF16),
        compiler_params=_params("arbitrary", "arbitrary", "arbitrary"),
        name="ec_ffn_up",
    )(xs, w_gate, w_up)
    tm2 = _largest_divisor(rows, MOE_DOWN_ROW_TILE)
    return pl.pallas_call(
        _ffn_down_kernel,
        grid=(n_exp, rows // tm2),
        in_specs=[pl.BlockSpec((1, tm2, ff), lambda e, m: (e, m, 0)),
                  pl.BlockSpec((1, 1, ff, d), lambda e, m: (layer, e, 0, 0))],
        out_specs=pl.BlockSpec((1, tm2, d), lambda e, m: (e, m, 0)),
        out_shape=jax.ShapeDtypeStruct((n_exp, rows, d), BF16),
        scratch_shapes=[pltpu.VMEM((ff, d), BF16)],
        compiler_params=_params("arbitrary", "arbitrary"),
        name="ec_ffn_down",
    )(hm, w_down)


def _combine_kernel(x_ref, g2_ref, pos_t_ref, aff_ref, y_ref, fg_ref, o_ref, *, cap, final_norm):
    tt = x_ref.shape[1]
    pos = _token_major(pos_t_ref)
    aff = aff_ref[0]
    lane = lax.broadcasted_iota(I32, (tt, cap), 1)
    acc = jnp.zeros(x_ref.shape[1:], F32)
    for e in range(N_EXPERTS):
        onehot = (lane == pos[:, e:e + 1]).astype(BF16)
        acc = acc + aff[:, e:e + 1] * _dot(onehot, y_ref[e])
    _finish_combine(x_ref, g2_ref, fg_ref, o_ref, acc, final_norm)


def _combine(x1, g2, pos_t, aff, y, final_g, cap, final_norm, first_block):
    b, t, d = x1.shape
    tt = min(TOKEN_TILE, t)
    row = lambda i, j: (i, j, 0)
    return pl.pallas_call(
        functools.partial(_combine_kernel, cap=cap, final_norm=final_norm),
        grid=(b, t // tt),
        in_specs=[pl.BlockSpec((1, tt, d), row),
                  pl.BlockSpec((1, 1, d), lambda i, j: (i, 0, 0)),
                  pl.BlockSpec((1, N_EXPERTS, tt), lambda i, j: (i, 0, j)),
                  pl.BlockSpec((1, tt, LANES), row),
                  pl.BlockSpec((N_EXPERTS, cap, d), lambda i, j: (0, first_block + i, 0)),
                  pl.BlockSpec((1, d), lambda i, j: (0, 0))],
        out_specs=pl.BlockSpec((1, tt, d), row),
        out_shape=jax.ShapeDtypeStruct((b, t, d), F32),
        compiler_params=_params("arbitrary", "arbitrary"),
        name="ec_combine",
    )(x1, g2, pos_t, aff, y, final_g)


def _token_major(pos_t_ref):
    pos_t = pos_t_ref[0].astype(F32)
    pad = jnp.full((LANES - N_EXPERTS, pos_t.shape[1]), -1.0, F32)
    return jnp.concatenate([pos_t, pad], axis=0).T.astype(I32)


def _finish_combine(x_ref, g2_ref, fg_ref, o_ref, acc, final_norm):
    x2 = x_ref[0] + g2_ref[0] * acc
    if final_norm:
        x2 = (x2 * lax.rsqrt(jnp.mean(x2 * x2, axis=-1, keepdims=True) + NORM_EPS)) * fg_ref[...]
    o_ref[0] = x2


def _combine_win_kernel(starts_ref, x_ref, g2_ref, pos_t_ref, aff_ref, y_ref, fg_ref, o_ref,
                        *, cap, ntp, final_norm):
    w = SLOT_WINDOW
    assert w & (w - 1) == 0, "window column index is taken with a bit mask"
    tile = x_ref.shape[1]
    wins, fits = _tile_windows(starts_ref, cap, ntp)
    pos = _token_major(pos_t_ref)
    aff = aff_ref[0]

    @pl.when(fits)
    def _():
        ywin = jnp.concatenate([y_ref[e, pl.ds(wins[e], w), :] for e in range(N_EXPERTS)], axis=0)
        lane = lax.broadcasted_iota(I32, (1, LANES), 1)
        win_row = jnp.zeros((1, LANES), I32)
        for e in range(N_EXPERTS):
            win_row = jnp.where(lane == e, wins[e], win_row)
        rel = jnp.where(pos >= 0, jnp.clip(pos - win_row, -1, w), -1)
        src = lax.broadcasted_iota(I32, (LANES, N_EXPERTS * w), 0) * w
        col = lax.broadcasted_iota(I32, (LANES, N_EXPERTS * w), 1)
        spread = ((col >= src) & (col < src + w)).astype(BF16)
        wide = _dot(jnp.concatenate([rel.astype(F32).astype(BF16), aff.astype(BF16)], axis=0), spread)
        in_win = lax.broadcasted_iota(I32, (1, N_EXPERTS * w), 1) & (w - 1)
        gates = jnp.where(wide[:tile] == in_win.astype(F32), wide[tile:], 0.0).astype(BF16)
        _finish_combine(x_ref, g2_ref, fg_ref, o_ref, _dot(gates, ywin), final_norm)

    @pl.when(jnp.logical_not(fits))
    def _():
        lane = lax.broadcasted_iota(I32, (tile, cap), 1)
        acc = jnp.zeros(x_ref.shape[1:], F32)
        for e in range(N_EXPERTS):
            onehot = (lane == pos[:, e:e + 1]).astype(BF16)
            acc = acc + aff[:, e:e + 1] * _dot(onehot, y_ref[e])
        _finish_combine(x_ref, g2_ref, fg_ref, o_ref, acc, final_norm)


def _combine_windowed(x1, g2, pos_t, starts, aff, y, final_g, cap, final_norm):
    b, t, d = x1.shape
    tile = min(TOKEN_TILE, t)
    nt = t // tile
    row = lambda i, j, s: (i, j, 0)
    grid_spec = pltpu.PrefetchScalarGridSpec(
        num_scalar_prefetch=1,
        grid=(b, nt),
        in_specs=[pl.BlockSpec((1, tile, d), row),
                  pl.BlockSpec((1, 1, d), lambda i, j, s: (i, 0, 0)),
                  pl.BlockSpec((1, N_EXPERTS, tile), lambda i, j, s: (i, 0, j)),
                  pl.BlockSpec((1, tile, LANES), row),
                  pl.BlockSpec((N_EXPERTS, cap, d), lambda i, j, s: (0, i, 0)),
                  pl.BlockSpec((1, d), lambda i, j, s: (0, 0))],
        out_specs=pl.BlockSpec((1, tile, d), row),
    )
    return pl.pallas_call(
        functools.partial(_combine_win_kernel, cap=cap, ntp=nt + 1, final_norm=final_norm),
        grid_spec=grid_spec,
        out_shape=jax.ShapeDtypeStruct((b, t, d), F32),
        compiler_params=_params("arbitrary", "arbitrary"),
        name="ec_combine_win",
    )(starts[:, :, :nt + 1].reshape(-1), x1, g2, pos_t, aff, y, final_g)


def _ec_moe(streams, w_gate, w_up, w_down, layer, final_g, final_norm):
    b, _, d = streams[0][0].shape
    caps = [CAPACITY_FACTOR * s[0].shape[1] // N_EXPERTS for s in streams]
    rows = sum(b * cap for cap in caps)
    routes, first_blocks, xs = [], [], None
    row0 = 0
    for k, ((x1, h2, aff, aff_t, g2), cap) in enumerate(zip(streams, caps)):
        t = x1.shape[1]
        assert row0 % cap == 0 and cap % SLOT_ALIGN == 0, "a stream's row blocks must tile the slot buffer"
        pos_t, starts = _route(aff_t, cap)
        windowed = k == 0 and t > TOKEN_TILE and cap >= SLOT_WINDOW and (cap - SLOT_WINDOW) % SLOT_ALIGN == 0
        if windowed:
            xs = _gather_windowed(pos_t, starts, h2, cap, rows)
        else:
            xs = _gather_stacked(pos_t, h2, cap, rows, into=xs, first_block=row0 // cap)
        routes.append((pos_t, starts, aff, windowed))
        first_blocks.append(row0 // cap)
        row0 += b * cap
    y = _expert_ffn(xs, w_gate, w_up, w_down, layer)
    outs = []
    for (x1, h2, _, _, g2), cap, (pos_t, starts, aff, windowed), first in zip(streams, caps, routes, first_blocks):
        if windowed:
            outs.append(_combine_windowed(x1, g2, pos_t, starts, aff, y, final_g, cap, final_norm))
        else:
            outs.append(_combine(x1, g2, pos_t, aff, y, final_g, cap, final_norm, first))
    return outs


def _pre_gla_kernel(x_ref, g_ref, sh_ref, sc_ref, w_ref, wlr_ref, w2_ref, gb_ref,
                    q_ref, r_ref, k_ref, v_ref, lgf_ref, lgb_ref):
    qk = q_ref.shape[-1]
    vd = v_ref.shape[-1]
    tm = x_ref.shape[1]
    n_sub = max(1, tm // GLA_IN_SUBTILE_ROWS)
    subs = [slice(i * tm // n_sub, (i + 1) * tm // n_sub) for i in range(n_sub)]
    hs = [_norm_mod(x_ref[0, rs, :], g_ref[...], sh_ref[0], sc_ref[0]).astype(BF16) for rs in subs]
    ps = [_dot(h, w_ref[...]) for h in hs]
    lrs = [_dot(h, wlr_ref[...]).astype(BF16) for h in hs]
    zs = [_dot(lr, w2_ref[...]) + gb_ref[...] for lr in lrs]
    for rs, p, z in zip(subs, ps, zs):
        q_ref[0, rs, :] = (p[:, :qk] * (qk // GLA_HEADS) ** -0.5).astype(BF16)
        r_ref[0, rs, :] = _silu(p[:, qk:qk + vd]).astype(BF16)
        k_ref[0, rs, :] = p[:, qk + vd:2 * qk + vd].astype(BF16)
        v_ref[0, rs, :] = p[:, 2 * qk + vd:].astype(BF16)
        lg = (jnp.minimum(z, 0.0) - jnp.log1p(jnp.exp(-jnp.abs(z)))) * (1.0 / GLA_TAU)
        lgf_ref[0, rs, :] = lg[:, :qk]
        lgb_ref[0, rs, :] = lg[:, qk:]


def _pre_gla(x, g, shift, scale, w_main, w_lr, w2, gate_b):
    b, t, d = x.shape
    n_main = w_main.shape[1]
    qk = w2.shape[1] // 2
    vd = (n_main - 2 * qk) // 2
    tm = min(GLA_IN_ROW_TILE, t)
    row = lambda i, j: (i, j, 0)
    per_b = lambda i, j: (i, 0, 0)
    fixed = lambda i, j: (0, 0)
    out = lambda n, dt: (pl.BlockSpec((1, tm, n), row), jax.ShapeDtypeStruct((b, t, n), dt))
    outs = [out(qk, BF16), out(vd, BF16), out(qk, BF16), out(vd, BF16), out(qk, F32), out(qk, F32)]
    return pl.pallas_call(
        _pre_gla_kernel,
        grid=(b, t // tm),
        in_specs=[pl.BlockSpec((1, tm, d), row),
                  pl.BlockSpec((1, d), fixed),
                  pl.BlockSpec((1, 1, d), per_b),
                  pl.BlockSpec((1, 1, d), per_b),
                  pl.BlockSpec((d, n_main), fixed),
                  pl.BlockSpec((d, LANES), fixed),
                  pl.BlockSpec((LANES, 2 * qk), fixed),
                  pl.BlockSpec((1, 2 * qk), fixed)],
        out_specs=[o[0] for o in outs],
        out_shape=[o[1] for o in outs],
        compiler_params=_params("arbitrary", "arbitrary"),
        name="gla_in_proj",
    )(x, g, shift, scale, w_main, w_lr, w2, gate_b)


def _gla_masks(rows):
    ri = lax.broadcasted_iota(I32, (rows, rows), 0)
    ci = lax.broadcasted_iota(I32, (rows, rows), 1)
    same = (ri // GLA_CHUNK) == (ci // GLA_CHUNK)
    return same & (ri >= ci), same & (ri <= ci)


def _gla_tiles(jobs):
    ch = GLA_CHUNK
    n = len(jobs)
    rows, dk = jobs[0][1].shape
    n_chunks = rows // ch
    last = [[c * ch + (ch - 1 if job[6] else 0) for c in range(n_chunks)] for job in jobs]
    b = []
    for q, k, v, lg, st_ref, mask, forward in jobs:
        lg_hi, lg_lo = _split2(lg)
        m = mask.astype(BF16)
        b.append(_dot(m, lg_hi) + _dot(m, lg_lo))
    qt, kt, kd = [None] * n, [None] * n, [None] * n
    for i, (q, k, v, lg, st_ref, mask, forward) in enumerate(jobs):
        b_last = jnp.concatenate([jnp.broadcast_to(b[i][r:r + 1, :], (ch, dk)) for r in last[i]], axis=0)
        kf = k.astype(F32)
        kd[i] = (kf * jnp.exp(b_last - b[i])).astype(BF16)
        if q is not None:
            qt[i] = (q.astype(F32) * jnp.exp(b[i])).astype(BF16)
            kt[i] = (kf * jnp.exp(-b[i])).astype(BF16)
    a = [None if jobs[i][0] is None else lax.dot_general(qt[i], kt[i], NT_DIMS, preferred_element_type=F32)
         for i in range(n)]
    a = [None if a[i] is None else jnp.where(jobs[i][5], a[i], 0.0).astype(BF16) for i in range(n)]
    o_local = [None if a[i] is None else _dot(a[i], jobs[i][2]) for i in range(n)]
    d_st = [[lax.dot_general(jobs[i][2][c * ch:(c + 1) * ch], kd[i][c * ch:(c + 1) * ch], TN_DIMS,
                             preferred_element_type=F32) for c in range(n_chunks)] for i in range(n)]
    results = []
    for i, (q, k, v, lg, st_ref, mask, forward) in enumerate(jobs):
        st = st_ref[...]
        outs = [None] * n_chunks
        for c in (range(n_chunks) if forward else reversed(range(n_chunks))):
            rs = slice(c * ch, (c + 1) * ch)
            if q is not None:
                outs[c] = o_local[i][rs] + lax.dot_general(qt[i][rs], st.astype(BF16), NT_DIMS,
                                                           preferred_element_type=F32)
            st = jnp.exp(b[i][last[i][c]:last[i][c] + 1, :]) * st + d_st[i][c]
        st_ref[...] = st
        results.append(None if q is None else jnp.concatenate(outs, axis=0))
    return results


def _gla_kernel(q_ref, k_ref, v_ref, lgf_ref, lgb_ref, kc_ref, vc_ref, lgcf_ref, lgcb_ref, r_ref, ng_ref,
                o_ref, of_ref, ob_ref, stf_ref, stb_ref):
    t = q_ref.shape[1]
    t_ctx = kc_ref.shape[1]
    n_heads = stf_ref.shape[0]
    dv, dk = stf_ref.shape[1:]
    tile = min(GLA_TILE, t)
    ctile = min(GLA_TILE, t_ctx)
    mask_f, mask_b = _gla_masks(tile)
    cmask_f, cmask_b = (mask_f, mask_b) if ctile == tile else _gla_masks(ctile)
    stf_ref[...] = jnp.zeros_like(stf_ref)
    stb_ref[...] = jnp.zeros_like(stb_ref)
    kcols = [slice(h * dk, (h + 1) * dk) for h in range(n_heads)]
    vcols = [slice(h * dv, (h + 1) * dv) for h in range(n_heads)]

    n_ctx = t_ctx // ctile
    for i in range(n_ctx):
        rf = slice(i * ctile, (i + 1) * ctile)
        rb = slice((n_ctx - 1 - i) * ctile, (n_ctx - i) * ctile)
        jobs = []
        for h in range(n_heads):
            jobs.append((None, kc_ref[0, rf, kcols[h]], vc_ref[0, rf, vcols[h]], lgcf_ref[0, rf, kcols[h]],
                         stf_ref.at[h], cmask_f, True))
            jobs.append((None, kc_ref[0, rb, kcols[h]], vc_ref[0, rb, vcols[h]], lgcb_ref[0, rb, kcols[h]],
                         stb_ref.at[h], cmask_b, False))
        _gla_tiles(jobs)

    n = t // tile

    def body(i, carry):
        rf = pl.ds(pl.multiple_of(i * tile, tile), tile)
        rb = pl.ds(pl.multiple_of((n - 1 - i) * tile, tile), tile)
        jobs = []
        for h in range(n_heads):
            jobs.append((q_ref[0, rf, kcols[h]], k_ref[0, rf, kcols[h]], v_ref[0, rf, vcols[h]],
                         lgf_ref[0, rf, kcols[h]], stf_ref.at[h], mask_f, True))
            jobs.append((q_ref[0, rb, kcols[h]], k_ref[0, rb, kcols[h]], v_ref[0, rb, vcols[h]],
                         lgb_ref[0, rb, kcols[h]], stb_ref.at[h], mask_b, False))
        outs = _gla_tiles(jobs)
        for h in range(n_heads):
            of_ref[rf, vcols[h]] = outs[2 * h]
            ob_ref[rb, vcols[h]] = outs[2 * h + 1]
        return carry

    lax.fori_loop(0, n, body, 0)

    def readout(i, carry):
        rows = pl.ds(pl.multiple_of(i * tile, tile), tile)
        for h in range(n_heads):
            o = of_ref[rows, vcols[h]] + ob_ref[rows, vcols[h]]
            o = o * lax.rsqrt(jnp.mean(o * o, axis=-1, keepdims=True) + NORM_EPS) * ng_ref[:, vcols[h]]
            o_ref[0, rows, vcols[h]] = (o * r_ref[0, rows, vcols[h]].astype(F32)).astype(o_ref.dtype)
        return carry

    lax.fori_loop(0, n, readout, 0)


def _gla(q, k, v, lgf, lgb, kc, vc, lgcf, lgcb, r, norm_g):
    b, t, qk = q.shape
    vd = v.shape[-1]
    t_ctx = kc.shape[1]
    dk, dv = qk // GLA_HEADS, vd // GLA_HEADS
    hp = GLA_HEADS_PER_STEP
    head = lambda i, h: (i, 0, h)
    lat = lambda n: pl.BlockSpec((1, t, hp * n), head)
    ctx = lambda n: pl.BlockSpec((1, t_ctx, hp * n), head)
    return pl.pallas_call(
        _gla_kernel,
        grid=(b, GLA_HEADS // hp),
        in_specs=[lat(dk), lat(dk), lat(dv), lat(dk), lat(dk),
                  ctx(dk), ctx(dv), ctx(dk), ctx(dk),
                  lat(dv),
                  pl.BlockSpec((1, hp * dv), lambda i, h: (0, h))],
        out_specs=lat(dv),
        out_shape=jax.ShapeDtypeStruct((b, t, vd), BF16),
        scratch_shapes=[pltpu.VMEM((t, hp * dv), F32), pltpu.VMEM((t, hp * dv), F32),
                        pltpu.VMEM((hp, dv, dk), F32), pltpu.VMEM((hp, dv, dk), F32)],
        compiler_params=_params("arbitrary", "arbitrary"),
        name="gla_scan",
    )(q, k, v, lgf, lgb, kc, vc, lgcf, lgcb, r, norm_g)


def _rope_tables(t):
    pos = jnp.arange(t)
    row = (pos // GRID_W).astype(F32)
    col = (pos % GRID_W).astype(F32)
    inv = ROPE_BASE ** (-jnp.arange(ROPE_PAIRS, dtype=F32) / ROPE_PAIRS)
    ar, ac = row[:, None] * inv, col[:, None] * inv
    ang = jnp.concatenate([ar, ar, ac, ac], axis=-1)
    sign = jnp.tile(jnp.repeat(jnp.array([-1.0, 1.0], F32), ROPE_PAIRS), 2)
    reps = LANES // HEAD_DIM
    return jnp.tile(jnp.cos(ang), (1, reps)), jnp.tile(jnp.sin(ang) * sign, (1, reps))


def _router_split(w):
    d, n = w.shape
    w = jnp.pad(w, ((0, 0), (0, LANES - n)))
    hi = w.astype(BF16)
    return hi, (w - hi.astype(F32)).astype(BF16)


def kernel(x, c, ctx, c_ctx, ada_w, ada_b, norm1_g, norm2_g, attn_w_in, attn_w_out, attn_sink, gla_w_in, gla_gate_w2, gla_gate_b, gla_norm_g, gla_w_out, router_w, exp_w_gate, exp_w_up, exp_w_down, final_norm_g):
    b, t, d = x.shape
    depth = ada_w.shape[0]
    assert depth == 2, "layer 0 is windowed attention with context output, layer 1 is GLA and last"
    rows = -(-(b + 1) // 16) * 16
    src = jnp.concatenate([c, c_ctx[None, :], jnp.zeros((rows - b - 1, d), F32)], axis=0)
    cos, sin_signed = _rope_tables(t)
    final_g = final_norm_g.reshape(1, d)

    def chunks(i):
        mod = _modulation(src, ada_w, ada_b, i)
        lat = [mod[:b, j * d:(j + 1) * d].reshape(b, 1, d) for j in range(6)]
        cx = [jnp.broadcast_to(mod[b, j * d:(j + 1) * d].reshape(1, 1, d), (b, 1, d)) for j in range(6)]
        return lat, cx

    (sh1, sc1, g1, sh2, sc2, g2), (sh1c, sc1c, g1c, sh2c, sc2c, g2c) = chunks(0)
    n1, n2 = norm1_g[0].reshape(1, d), norm2_g[0].reshape(1, d)
    w_in = attn_w_in[0].astype(BF16)
    w_out = attn_w_out[0].astype(BF16)
    sink = attn_sink[0].reshape(1, -1)
    wr_hi, wr_lo = _router_split(router_w[0])
    q, k, v = _pre_attn(x, n1, sh1, sc1, w_in, cos, sin_signed, rope=True)
    t_ctx = ctx.shape[1]
    qc, kc, vc = _pre_attn(ctx, n1, sh1c, sc1c, w_in, cos[:t_ctx], sin_signed[:t_ctx], rope=False)
    o = _attn_band(q, k, v, kc, vc, sink)
    oc = _attn_ctx(qc, kc, vc, sink)
    lat = _post_mixer(o, w_out, x, g1, n2, sh2, sc2, wr_hi, wr_lo)
    cxt = _post_mixer(oc, w_out, ctx, g1c, n2, sh2c, sc2c, wr_hi, wr_lo)
    x, xc = _ec_moe([(*lat, g2), (*cxt, g2c)], exp_w_gate, exp_w_up, exp_w_down, 0, final_g, False)

    (sh1, sc1, g1, sh2, sc2, g2), (sh1c, sc1c, _, _, _, _) = chunks(1)
    n1, n2 = norm1_g[1].reshape(1, d), norm2_g[1].reshape(1, d)
    qk = gla_gate_w2.shape[-1]
    n_main = gla_w_in.shape[-1] - 2 * GLA_RANK
    w_main = gla_w_in[0][:, :n_main].astype(BF16)
    w_lr = jnp.pad(gla_w_in[0][:, n_main:], ((0, 0), (0, LANES - 2 * GLA_RANK))).astype(BF16)
    w2 = jnp.zeros((LANES, 2 * qk), F32)
    w2 = w2.at[:GLA_RANK, :qk].set(gla_gate_w2[0, 0]).at[GLA_RANK:2 * GLA_RANK, qk:].set(gla_gate_w2[0, 1])
    w2 = w2.astype(BF16)
    gate_b = gla_gate_b[0].reshape(1, 2 * qk)
    gq, gr, gk, gv, lgf, lgb = _pre_gla(x, n1, sh1, sc1, w_main, w_lr, w2, gate_b)
    _, _, gkc, gvc, lgcf, lgcb = _pre_gla(xc, n1, sh1c, sc1c, w_main, w_lr, w2, gate_b)
    og = _gla(gq, gk, gv, lgf, lgb, gkc, gvc, lgcf, lgcb, gr, gla_norm_g[0].reshape(1, -1))
    wr_hi, wr_lo = _router_split(router_w[1])
    lat = _post_mixer(og, gla_w_out[0].astype(BF16), x, g1, n2, sh2, sc2, wr_hi, wr_lo)
    return _ec_moe([(*lat, g2)], exp_w_gate, exp_w_up, exp_w_down, 1, final_g, True)[0]
```

```python
import functools

import jax
import jax.numpy as jnp
from jax import lax
from jax.experimental import pallas as pl
from jax.experimental.pallas import tpu as pltpu

F32 = jnp.float32
BF16 = jnp.bfloat16
I32 = jnp.int32

LANES = 128
HEAD_DIM = 64
ATT_GROUP = 4
ATT_BLOCK = 128
ATT_BLOCKS_PER_STEP = 2
GRID_W = 64
ROPE_BASE = 10000.0
ROPE_PAIRS = HEAD_DIM // 4
GLA_HEADS = 4
GLA_RANK = 16
GLA_TAU = 16.0
GLA_CHUNK = 64
N_EXPERTS = 16
CAPACITY_FACTOR = 2
NORM_EPS = 1e-6
LOG2E = 1.4426950408889634
ROW_TILE = 512
ATTN_IN_ROW_TILE = 512
ATTN_IN_SUBTILE_ROWS = 256
GLA_IN_ROW_TILE = 512
GLA_IN_SUBTILE_ROWS = 256
POST_SUBTILES = 2
TOKEN_TILE = 256
MOE_ROW_TILE = 4608
MOE_DOWN_ROW_TILE = 1024
FF_TILE = 256
FEW_ROWS = 512
FF_TILE_FEW_ROWS = 1408
GLA_TILE = 256
GLA_HEADS_PER_STEP = 2
SLOT_WINDOW = 64
SLOT_ALIGN = 16
VMEM_LIMIT = 56 * 1024 * 1024

NT_DIMS = (((1,), (1,)), ((), ()))
TN_DIMS = (((0,), (0,)), ((), ()))


def _params(*sem):
    return pltpu.CompilerParams(dimension_semantics=sem, vmem_limit_bytes=VMEM_LIMIT)


def _dot(a, b):
    return jnp.dot(a, b, preferred_element_type=F32)


def _split2(a):
    hi = a.astype(BF16)
    lo = (a - hi.astype(F32)).astype(BF16)
    return hi, lo


def _dot3(a, b_hi, b_lo):
    a_hi, a_lo = _split2(a)
    return _dot(a_hi, b_hi) + _dot(a_hi, b_lo) + _dot(a_lo, b_hi)


def _silu(a):
    return a * jax.nn.sigmoid(a)


def _norm_mod(x, g, shift, scale):
    y = x * lax.rsqrt(jnp.mean(x * x, axis=-1, keepdims=True) + NORM_EPS)
    return (y * g) * (1.0 + scale) + shift


def _mod_kernel(src_ref, w_ref, b_ref, o_ref):
    w_hi, w_lo = _split2(w_ref[0])
    o_ref[...] = _dot3(_silu(src_ref[...]), w_hi, w_lo) + b_ref[0]


def _modulation(src, ada_w, ada_b, layer):
    rows, d = src.shape
    depth, _, n = ada_w.shape
    tn = 512
    return pl.pallas_call(
        _mod_kernel,
        grid=(n // tn,),
        in_specs=[pl.BlockSpec((rows, d), lambda j: (0, 0)),
                  pl.BlockSpec((1, d, tn), lambda j: (layer, 0, j)),
                  pl.BlockSpec((1, 1, tn), lambda j: (layer, 0, j))],
        out_specs=pl.BlockSpec((rows, tn), lambda j: (0, j)),
        out_shape=jax.ShapeDtypeStruct((rows, n), F32),
        compiler_params=_params("arbitrary"),
        name="adaln_mod",
    )(src, ada_w, ada_b.reshape(depth, 1, n))


def _rope(x, cos, sin_signed, lane_lo):
    outs = []
    for j in range(x.shape[1] // LANES):
        xb = x[:, j * LANES:(j + 1) * LANES]
        partner = jnp.where(lane_lo, pltpu.roll(xb, LANES - ROPE_PAIRS, 1), pltpu.roll(xb, ROPE_PAIRS, 1))
        outs.append(xb * cos + partner * sin_signed)
    return jnp.concatenate(outs, axis=1)


def _pre_attn_kernel(x_ref, g_ref, sh_ref, sc_ref, w_ref, cos_ref, sin_ref, q_ref, k_ref, v_ref, *, rope):
    d = x_ref.shape[-1]
    kv = k_ref.shape[-1]
    tm = x_ref.shape[1]
    n_sub = max(1, tm // ATTN_IN_SUBTILE_ROWS)
    subs = [slice(i * tm // n_sub, (i + 1) * tm // n_sub) for i in range(n_sub)]
    hs = [_norm_mod(x_ref[0, rs, :], g_ref[...], sh_ref[0], sc_ref[0]).astype(BF16) for rs in subs]
    ps = [_dot(h, w_ref[...]) for h in hs]
    for rs, p in zip(subs, ps):
        q, k, v = p[:, :d], p[:, d:d + kv], p[:, d + kv:]
        if rope:
            lane = lax.broadcasted_iota(I32, (tm // n_sub, LANES), 1)
            lane_lo = (lane & (2 * ROPE_PAIRS - 1)) < ROPE_PAIRS
            cos, sin_signed = cos_ref[rs, :], sin_ref[rs, :]
            q = _rope(q, cos, sin_signed, lane_lo)
            k = _rope(k, cos, sin_signed, lane_lo)
        q_ref[0, rs, :] = (q * (HEAD_DIM ** -0.5 * LOG2E)).astype(BF16)
        k_ref[0, rs, :] = k.astype(BF16)
        v_ref[0, rs, :] = v.astype(BF16)


def _pre_attn(x, g, shift, scale, w_in, cos, sin_signed, rope):
    b, t, d = x.shape
    n_in = w_in.shape[1]
    kv = (n_in - d) // 2
    tm = min(ATTN_IN_ROW_TILE, t)
    row = lambda i, j: (i, j, 0)
    per_b = lambda i, j: (i, 0, 0)
    fixed = lambda i, j: (0, 0)
    return pl.pallas_call(
        functools.partial(_pre_attn_kernel, rope=rope),
        grid=(b, t // tm),
        in_specs=[pl.BlockSpec((1, tm, d), row),
                  pl.BlockSpec((1, d), fixed),
                  pl.BlockSpec((1, 1, d), per_b),
                  pl.BlockSpec((1, 1, d), per_b),
                  pl.BlockSpec((d, n_in), fixed),
                  pl.BlockSpec((tm, LANES), lambda i, j: (j, 0)),
                  pl.BlockSpec((tm, LANES), lambda i, j: (j, 0))],
        out_specs=[pl.BlockSpec((1, tm, d), row),
                   pl.BlockSpec((1, tm, kv), row),
                   pl.BlockSpec((1, tm, kv), row)],
        out_shape=[jax.ShapeDtypeStruct((b, t, d), BF16),
                   jax.ShapeDtypeStruct((b, t, kv), BF16),
                   jax.ShapeDtypeStruct((b, t, kv), BF16)],
        compiler_params=_params("arbitrary", "arbitrary"),
        name="attn_in_proj",
    )(x, g, shift, scale, w_in, cos, sin_signed)


def _attn_heads(groups, sink_ref, o_ref):
    tq = groups[0][0].shape[0]
    n_kv = groups[0][1][0].shape[1] // HEAD_DIM
    lane = lax.broadcasted_iota(I32, (1, ATT_GROUP * tq), 1)
    v_lane = lax.broadcasted_iota(I32, (1, LANES), 1)
    jobs = [(group, h) for h in range(n_kv) for group in groups]
    n_jobs = len(jobs)
    sks, scores, ms, es, both = ([None] * n_jobs for _ in range(5))

    def score_stage(job):
        (q, k_parts, v_parts, bias_parts, rows), h = jobs[job]
        heads = [ATT_GROUP * h + g for g in range(ATT_GROUP)]
        qg = jnp.concatenate([q[:, i * HEAD_DIM:(i + 1) * HEAD_DIM] for i in heads], axis=0)
        sk = jnp.full((1, ATT_GROUP * tq), sink_ref[0, heads[0]], F32)
        for g in range(1, ATT_GROUP):
            sk = jnp.where(lane >= g * tq, sink_ref[0, heads[g]], sk)
        sks[job] = sk * LOG2E
        m = sks[job]
        scores[job] = []
        for kp, bias in zip(k_parts, bias_parts):
            s = lax.dot_general(kp[:, h * HEAD_DIM:(h + 1) * HEAD_DIM], qg, NT_DIMS, preferred_element_type=F32)
            s = s if bias is None else s + bias
            m = jnp.maximum(m, jnp.max(s, axis=0, keepdims=True))
            scores[job].append(s)
        ms[job] = m

    def weight_stage(job):
        es[job] = jnp.concatenate([jnp.exp2(s - ms[job]).astype(BF16) for s in scores[job]], axis=0)

    def value_stage(job):
        (q, k_parts, v_parts, bias_parts, rows), h = jobs[job]
        block = (h * HEAD_DIM // LANES) * LANES
        upper = (h * HEAD_DIM) % LANES != 0
        v_pair = jnp.concatenate([vp[:, block:block + LANES] for vp in v_parts], axis=0)
        own = (v_lane >= HEAD_DIM) if upper else (v_lane < HEAD_DIM)
        v_aug = jnp.where(own, v_pair, jnp.ones_like(v_pair))
        both[job] = lax.dot_general(v_aug, es[job], TN_DIMS, preferred_element_type=F32)

    stages = (score_stage, weight_stage, value_stage)
    for step in range(n_jobs + len(stages) - 1):
        for depth, stage in enumerate(stages):
            if 0 <= step - depth < n_jobs:
                stage(step - depth)
    for gi, group in enumerate(groups):
        outs = []
        for h in range(n_kv):
            job = h * len(groups) + gi
            upper = (h * HEAD_DIM) % LANES != 0
            num = both[job][HEAD_DIM:] if upper else both[job][:HEAD_DIM]
            total = both[job][:1] if upper else both[job][HEAD_DIM:HEAD_DIM + 1]
            o_t = num / (total + jnp.exp2(sks[job] - ms[job]))
            outs.extend(o_t[:, g * tq:(g + 1) * tq] for g in range(ATT_GROUP))
        o_ref[0, group[4], :] = jnp.concatenate(outs, axis=0).T.astype(o_ref.dtype)


def _band_bias():
    blk = ATT_BLOCK
    key = jnp.arange(blk)[:, None]
    qry = jnp.arange(ATT_GROUP * blk)[None, :] % blk
    prev_ok = key >= qry
    next_ok = key <= qry
    never = jnp.zeros_like(prev_ok)
    variants = [jnp.stack([never if first else prev_ok, never if last else next_ok])
                for last in (False, True) for first in (False, True)]
    return jnp.where(jnp.stack(variants), 0.0, -jnp.inf).astype(F32)


def _attn_band_kernel(*refs, qb):
    q_ref, o_ref, sink_ref = refs[0], refs[-1], refs[-2]
    k_refs = refs[1:qb + 3]
    v_refs = refs[qb + 3:2 * qb + 5]
    kc_ref, vc_ref = refs[2 * qb + 5], refs[2 * qb + 6]
    bias_refs = refs[2 * qb + 7:3 * qb + 7]
    blk = ATT_BLOCK
    groups = []
    for g in range(qb):
        rows = slice(g * blk, (g + 1) * blk)
        groups.append((q_ref[0, rows, :],
                       [kc_ref[0]] + [r[0] for r in k_refs[g:g + 3]],
                       [vc_ref[0]] + [r[0] for r in v_refs[g:g + 3]],
                       [None, bias_refs[g][0, 0], None, bias_refs[g][0, 1]], rows))
    _attn_heads(groups, sink_ref, o_ref)


def _attn_ctx_kernel(q_ref, kc_ref, vc_ref, sink_ref, o_ref):
    _attn_heads([(q_ref[0], [kc_ref[0]], [vc_ref[0]], [None], slice(None))], sink_ref, o_ref)


def _attn_band(q, k, v, kc, vc, sink):
    b, t, d = q.shape
    kv = k.shape[-1]
    n_ctx = kc.shape[1]
    nb = t // ATT_BLOCK
    qb = ATT_BLOCKS_PER_STEP if nb % ATT_BLOCKS_PER_STEP == 0 else 1
    steps = nb // qb
    kv_spec = lambda m: pl.BlockSpec((1, ATT_BLOCK, kv), lambda i, j: (i, jnp.clip(qb * j - 1 + m, 0, nb - 1), 0))
    kv_specs = [kv_spec(m) for m in range(qb + 2)]
    ctx_spec = pl.BlockSpec((1, n_ctx, kv), lambda i, j: (i, 0, 0))
    bias = _band_bias()

    def bias_spec(g):
        first = (lambda j: (j == 0).astype(I32)) if g == 0 else (lambda j: 0)
        last = (lambda j: 2 * (j == steps - 1).astype(I32)) if g == qb - 1 else (lambda j: 0)
        return pl.BlockSpec((1,) + bias.shape[1:], lambda i, j: (first(j) + last(j), 0, 0, 0))

    step_rows = pl.BlockSpec((1, qb * ATT_BLOCK, d), lambda i, j: (i, j, 0))
    return pl.pallas_call(
        functools.partial(_attn_band_kernel, qb=qb),
        grid=(b, steps),
        in_specs=[step_rows] + kv_specs + kv_specs + [ctx_spec, ctx_spec] + [bias_spec(g) for g in range(qb)]
                 + [pl.BlockSpec(memory_space=pltpu.SMEM)],
        out_specs=step_rows,
        out_shape=jax.ShapeDtypeStruct((b, t, d), BF16),
        compiler_params=_params("arbitrary", "arbitrary"),
        name="attn_band",
    )(q, *([k] * (qb + 2)), *([v] * (qb + 2)), kc, vc, *([bias] * qb), sink)


def _attn_ctx(qc, kc, vc, sink):
    b, n_ctx, d = qc.shape
    kv = kc.shape[-1]
    return pl.pallas_call(
        _attn_ctx_kernel,
        grid=(b,),
        in_specs=[pl.BlockSpec((1, n_ctx, d), lambda i: (i, 0, 0)),
                  pl.BlockSpec((1, n_ctx, kv), lambda i: (i, 0, 0)),
                  pl.BlockSpec((1, n_ctx, kv), lambda i: (i, 0, 0)),
                  pl.BlockSpec(memory_space=pltpu.SMEM)],
        out_specs=pl.BlockSpec((1, n_ctx, d), lambda i: (i, 0, 0)),
        out_shape=jax.ShapeDtypeStruct((b, n_ctx, d), BF16),
        compiler_params=_params("arbitrary"),
        name="attn_ctx",
    )(qc, kc, vc, sink)


def _post_kernel(o_ref, w_ref, x_ref, g1_ref, n2_ref, sh_ref, sc_ref, wr_hi_ref, wr_lo_ref,
                 x1_ref, h2_ref, aff_ref, aff_t_ref):
    tm = x_ref.shape[1]
    subs = [slice(i * tm // POST_SUBTILES, (i + 1) * tm // POST_SUBTILES) for i in range(POST_SUBTILES)]
    ys = [_dot(o_ref[0, rs, :], w_ref[...]) for rs in subs]
    h2s = []
    for rs, y in zip(subs, ys):
        x1 = x_ref[0, rs, :] + g1_ref[0] * y
        x1_ref[0, rs, :] = x1
        h2 = _norm_mod(x1, n2_ref[...], sh_ref[0], sc_ref[0])
        h2_ref[0, rs, :] = h2.astype(BF16)
        h2s.append(h2)
    lane = lax.broadcasted_iota(I32, (tm // POST_SUBTILES, LANES), 1)
    for rs, h2 in zip(subs, h2s):
        lg = jnp.where(lane < N_EXPERTS, _dot3(h2, wr_hi_ref[...], wr_lo_ref[...]), -jnp.inf)
        e = jnp.exp(lg - jnp.max(lg, axis=-1, keepdims=True))
        aff = e / jnp.sum(e, axis=-1, keepdims=True)
        aff_ref[0, rs, :] = aff
        aff_t_ref[0, :, rs] = aff.T[:N_EXPERTS]


def _post_mixer(o, w_out, x, g1, n2, shift, scale, wr_hi, wr_lo):
    b, t, d = x.shape
    k_in = o.shape[-1]
    tm = min(ROW_TILE, t)
    row = lambda i, j: (i, j, 0)
    per_b = lambda i, j: (i, 0, 0)
    fixed = lambda i, j: (0, 0)
    return pl.pallas_call(
        _post_kernel,
        grid=(b, t // tm),
        in_specs=[pl.BlockSpec((1, tm, k_in), row),
                  pl.BlockSpec((k_in, d), fixed),
                  pl.BlockSpec((1, tm, d), row),
                  pl.BlockSpec((1, 1, d), per_b),
                  pl.BlockSpec((1, d), fixed),
                  pl.BlockSpec((1, 1, d), per_b),
                  pl.BlockSpec((1, 1, d), per_b),
                  pl.BlockSpec((d, LANES), fixed),
                  pl.BlockSpec((d, LANES), fixed)],
        out_specs=[pl.BlockSpec((1, tm, d), row),
                   pl.BlockSpec((1, tm, d), row),
                   pl.BlockSpec((1, tm, LANES), row),
                   pl.BlockSpec((1, N_EXPERTS, tm), lambda i, j: (i, 0, j))],
        out_shape=[jax.ShapeDtypeStruct((b, t, d), F32),
                   jax.ShapeDtypeStruct((b, t, d), BF16),
                   jax.ShapeDtypeStruct((b, t, LANES), F32),
                   jax.ShapeDtypeStruct((b, N_EXPERTS, t), F32)],
        compiler_params=_params("arbitrary", "arbitrary"),
        name="mixer_out_proj",
    )(o, w_out, x, g1, n2, shift, scale, wr_hi, wr_lo)


def _exclusive_cumsum_lanes(mask):
    rows, t = mask.shape
    width = min(2 * LANES, t)
    r = lax.broadcasted_iota(I32, (width, width), 0)
    c = lax.broadcasted_iota(I32, (width, width), 1)
    upper = (r < c).astype(BF16)
    carry = jnp.zeros((rows, 1), F32)
    outs = []
    for ch in range(t // width):
        m = mask[:, ch * width:(ch + 1) * width]
        outs.append(_dot(m.astype(BF16), upper) + carry)
        carry = carry + jnp.sum(m, axis=1, keepdims=True)
    return jnp.concatenate(outs, axis=1)


LOG_FLOOR = -160.0
BISECT_LOG_STEPS = 40
BISECT_STEPS = 12


def _select_kernel(aff_t_ref, pos_t_ref, starts_ref, *, cap):
    rows = aff_t_ref.shape[0]

    def enough(threshold):
        return jnp.sum((aff_t_ref[...] >= threshold).astype(F32), axis=1, keepdims=True) >= cap

    def bisect_log(_, carry):
        lo_u, hi_u = carry
        mid = lo_u + (hi_u - lo_u) * 0.5
        ok = enough(jnp.exp2(mid))
        return jnp.where(ok, mid, lo_u), jnp.where(ok, hi_u, mid)

    def bisect(_, carry):
        lo, hi = carry
        mid = lo + (hi - lo) * 0.5
        ok = enough(mid)
        return jnp.where(ok, mid, lo), jnp.where(ok, hi, mid)

    lo_u, hi_u = lax.fori_loop(0, BISECT_LOG_STEPS, bisect_log,
                               (jnp.full((rows, 1), LOG_FLOOR, F32), jnp.full((rows, 1), 1.0, F32)))
    lo, hi = lax.fori_loop(0, BISECT_STEPS, bisect, (jnp.exp2(lo_u), jnp.exp2(hi_u)))
    aff_t = aff_t_ref[...]
    above = (aff_t >= hi).astype(F32)
    tied = ((aff_t >= lo) & (aff_t < hi)).astype(F32)
    need = cap - jnp.sum(above, axis=1, keepdims=True)
    sel = above + tied * (_exclusive_cumsum_lanes(tied) < need).astype(F32)
    slot = _exclusive_cumsum_lanes(sel)
    pos_t_ref[...] = jnp.where(sel > 0, slot, -1.0).astype(I32)
    t = aff_t.shape[1]
    tok = lax.broadcasted_iota(I32, (t, LANES), 0)
    edge = lax.broadcasted_iota(I32, (t, LANES), 1) * min(TOKEN_TILE, t)
    before = ((tok < edge) & (edge <= t)).astype(BF16)
    starts_ref[...] = _dot(sel.astype(BF16), before).astype(I32)


def _route(aff_t, cap):
    b, _, t = aff_t.shape
    rows = b * N_EXPERTS
    pos_t, starts = pl.pallas_call(
        functools.partial(_select_kernel, cap=cap),
        grid=(1,),
        in_specs=[pl.BlockSpec((rows, t), lambda i: (0, 0))],
        out_specs=[pl.BlockSpec((rows, t), lambda i: (0, 0)),
                   pl.BlockSpec((rows, LANES), lambda i: (0, 0))],
        out_shape=[jax.ShapeDtypeStruct((rows, t), I32),
                   jax.ShapeDtypeStruct((rows, LANES), I32)],
        compiler_params=_params("arbitrary"),
        name="ec_select",
    )(aff_t.reshape(rows, t))
    return pos_t.reshape(b, N_EXPERTS, t), starts.reshape(b, N_EXPERTS, LANES)


def _gather_stacked_kernel(pos_t_ref, h_ref, *rest, cap):
    xs_ref = rest[-1]
    t = h_ref.shape[1]
    pos_t = pos_t_ref[0]
    slot = lax.broadcasted_iota(I32, (cap, t), 0)
    onehot = jnp.concatenate([(slot == pos_t[e:e + 1, :]).astype(BF16) for e in range(N_EXPERTS)], axis=0)
    z = _dot(onehot, h_ref[0]).astype(BF16)
    for e in range(N_EXPERTS):
        xs_ref[e] = z[e * cap:(e + 1) * cap]


def _gather_stacked(pos_t, h, cap, rows, into=None, first_block=0):
    b, t, d = h.shape
    in_specs = [pl.BlockSpec((1, N_EXPERTS, t), lambda i: (i, 0, 0)),
                pl.BlockSpec((1, t, d), lambda i: (i, 0, 0))]
    args = [pos_t, h]
    if into is not None:
        assert into.shape == (N_EXPERTS, rows, d)
        in_specs.append(pl.BlockSpec(memory_space=pl.ANY))
        args.append(into)
    return pl.pallas_call(
        functools.partial(_gather_stacked_kernel, cap=cap),
        grid=(b,),
        in_specs=in_specs,
        out_specs=pl.BlockSpec((N_EXPERTS, cap, d), lambda i: (0, first_block + i, 0)),
        out_shape=jax.ShapeDtypeStruct((N_EXPERTS, rows, d), BF16),
        input_output_aliases={} if into is None else {2: 0},
        compiler_params=_params("arbitrary"),
        name="ec_gather_stacked",
    )(*args)


def _tile_windows(starts_ref, cap, ntp):
    i, j = pl.program_id(0), pl.program_id(1)
    wins, fits = [], None
    for e in range(N_EXPERTS):
        at = (i * N_EXPERTS + e) * ntp + j
        win = jnp.minimum((starts_ref[at] // SLOT_ALIGN) * SLOT_ALIGN, cap - SLOT_WINDOW)
        ok = starts_ref[at + 1] - win <= SLOT_WINDOW
        wins.append(pl.multiple_of(win, SLOT_ALIGN))
        fits = ok if fits is None else jnp.logical_and(fits, ok)
    return wins, fits


def _gather_win_kernel(starts_ref, pos_t_ref, h_ref, xs_ref, *, cap, ntp):
    w = SLOT_WINDOW
    tile = h_ref.shape[1]
    wins, fits = _tile_windows(starts_ref, cap, ntp)
    h = h_ref[0]
    pos_t = pos_t_ref[0]

    @pl.when(pl.program_id(1) == 0)
    def _():
        xs_ref[...] = jnp.zeros_like(xs_ref)

    @pl.when(fits)
    def _():
        row = lax.broadcasted_iota(I32, (w, tile), 0)
        onehot = jnp.concatenate([(row + wins[e] == pos_t[e:e + 1, :]).astype(BF16) for e in range(N_EXPERTS)],
                                 axis=0)
        z = _dot(onehot, h).astype(BF16)
        for e in range(N_EXPERTS):
            xs_ref[e, pl.ds(wins[e], w), :] += z[e * w:(e + 1) * w]

    @pl.when(jnp.logical_not(fits))
    def _():
        slot = lax.broadcasted_iota(I32, (cap, tile), 0)
        for e in range(N_EXPERTS):
            onehot = (slot == pos_t[e:e + 1, :]).astype(BF16)
            xs_ref[e] += _dot(onehot, h).astype(BF16)


def _gather_windowed(pos_t, starts, h, cap, rows):
    b, t, d = h.shape
    tile = min(TOKEN_TILE, t)
    nt = t // tile
    grid_spec = pltpu.PrefetchScalarGridSpec(
        num_scalar_prefetch=1,
        grid=(b, nt),
        in_specs=[pl.BlockSpec((1, N_EXPERTS, tile), lambda i, j, s: (i, 0, j)),
                  pl.BlockSpec((1, tile, d), lambda i, j, s: (i, j, 0))],
        out_specs=pl.BlockSpec((N_EXPERTS, cap, d), lambda i, j, s: (0, i, 0)),
    )
    return pl.pallas_call(
        functools.partial(_gather_win_kernel, cap=cap, ntp=nt + 1),
        grid_spec=grid_spec,
        out_shape=jax.ShapeDtypeStruct((N_EXPERTS, rows, d), BF16),
        compiler_params=_params("arbitrary", "arbitrary"),
        name="ec_gather_win",
    )(starts[:, :, :nt + 1].reshape(-1), pos_t, h)


def _ffn_up_kernel(xs_ref, wg_ref, wu_ref, hm_ref):
    xs = xs_ref[0]
    a = _dot(xs, wg_ref[0, 0].astype(BF16))
    u = _dot(xs, wu_ref[0, 0].astype(BF16))
    hm_ref[0] = (_silu(a) * u).astype(hm_ref.dtype)


def _ffn_down_kernel(hm_ref, wd_ref, y_ref, wd_bf16_ref):
    @pl.when(pl.program_id(1) == 0)
    def _():
        wd_bf16_ref[...] = wd_ref[0, 0].astype(BF16)

    y_ref[0] = _dot(hm_ref[0], wd_bf16_ref[...]).astype(y_ref.dtype)


def _largest_divisor(rows, limit):
    if rows <= limit:
        return rows
    tile = (limit // 256) * 256
    while tile and rows % tile:
        tile -= 256
    assert tile, f"no 256-multiple tile <= {limit} divides {rows} rows"
    return tile


def _expert_ffn(xs, w_gate, w_up, w_down, layer):
    n_exp, rows, d = xs.shape
    ff = w_gate.shape[-1]
    tm = _largest_divisor(rows, MOE_ROW_TILE)
    tf = min(FF_TILE, ff)
    if rows <= FEW_ROWS and ff % FF_TILE_FEW_ROWS == 0:
        tf = FF_TILE_FEW_ROWS
    hm = pl.pallas_call(
        _ffn_up_kernel,
        grid=(n_exp, rows // tm, ff // tf),
        in_specs=[pl.BlockSpec((1, tm, d), lambda e, m, f: (e, m, 0)),
                  pl.BlockSpec((1, 1, d, tf), lambda e, m, f: (layer, e, 0, f)),
                  pl.BlockSpec((1, 1, d, tf), lambda e, m, f: (layer, e, 0, f))],
        out_specs=pl.BlockSpec((1, tm, tf), lambda e, m, f: (e, m, f)),
        out_shape=jax.ShapeDtypeStruct((n_exp, rows, ff), BF16),
        compiler_params=_params("arbitrary", "arbitrary", "arbitrary"),
        name="ec_ffn_up",
    )(xs, w_gate, w_up)
    tm2 = _largest_divisor(rows, MOE_DOWN_ROW_TILE)
    return pl.pallas_call(
        _ffn_down_kernel,
        grid=(n_exp, rows // tm2),
        in_specs=[pl.BlockSpec((1, tm2, ff), lambda e, m: (e, m, 0)),
                  pl.BlockSpec((1, 1, ff, d), lambda e, m: (layer, e, 0, 0))],
        out_specs=pl.BlockSpec((1, tm2, d), lambda e, m: (e, m, 0)),
        out_shape=jax.ShapeDtypeStruct((n_exp, rows, d), BF16),
        scratch_shapes=[pltpu.VMEM((ff, d), BF16)],
        compiler_params=_params("arbitrary", "arbitrary"),
        name="ec_ffn_down",
    )(hm, w_down)


def _combine_kernel(x_ref, g2_ref, pos_t_ref, aff_ref, y_ref, fg_ref, o_ref, *, cap, final_norm):
    tt = x_ref.shape[1]
    pos = _token_major(pos_t_ref)
    aff = aff_ref[0]
    lane = lax.broadcasted_iota(I32, (tt, cap), 1)
    acc = jnp.zeros(x_ref.shape[1:], F32)
    for e in range(N_EXPERTS):
        onehot = (lane == pos[:, e:e + 1]).astype(BF16)
        acc = acc + aff[:, e:e + 1] * _dot(onehot, y_ref[e])
    _finish_combine(x_ref, g2_ref, fg_ref, o_ref, acc, final_norm)


def _combine(x1, g2, pos_t, aff, y, final_g, cap, final_norm, first_block):
    b, t, d = x1.shape
    tt = min(TOKEN_TILE, t)
    row = lambda i, j: (i, j, 0)
    return pl.pallas_call(
        functools.partial(_combine_kernel, cap=cap, final_norm=final_norm),
        grid=(b, t // tt),
        in_specs=[pl.BlockSpec((1, tt, d), row),
                  pl.BlockSpec((1, 1, d), lambda i, j: (i, 0, 0)),
                  pl.BlockSpec((1, N_EXPERTS, tt), lambda i, j: (i, 0, j)),
                  pl.BlockSpec((1, tt, LANES), row),
                  pl.BlockSpec((N_EXPERTS, cap, d), lambda i, j: (0, first_block + i, 0)),
                  pl.BlockSpec((1, d), lambda i, j: (0, 0))],
        out_specs=pl.BlockSpec((1, tt, d), row),
        out_shape=jax.ShapeDtypeStruct((b, t, d), F32),
        compiler_params=_params("arbitrary", "arbitrary"),
        name="ec_combine",
    )(x1, g2, pos_t, aff, y, final_g)


def _token_major(pos_t_ref):
    pos_t = pos_t_ref[0].astype(F32)
    pad = jnp.full((LANES - N_EXPERTS, pos_t.shape[1]), -1.0, F32)
    return jnp.concatenate([pos_t, pad], axis=0).T.astype(I32)


def _finish_combine(x_ref, g2_ref, fg_ref, o_ref, acc, final_norm):
    x2 = x_ref[0] + g2_ref[0] * acc
    if final_norm:
        x2 = (x2 * lax.rsqrt(jnp.mean(x2 * x2, axis=-1, keepdims=True) + NORM_EPS)) * fg_ref[...]
    o_ref[0] = x2


def _combine_win_kernel(starts_ref, x_ref, g2_ref, pos_t_ref, aff_ref, y_ref, fg_ref, o_ref,
                        *, cap, ntp, final_norm):
    w = SLOT_WINDOW
    assert w & (w - 1) == 0, "window column index is taken with a bit mask"
    tile = x_ref.shape[1]
    wins, fits = _tile_windows(starts_ref, cap, ntp)
    pos = _token_major(pos_t_ref)
    aff = aff_ref[0]

    @pl.when(fits)
    def _():
        ywin = jnp.concatenate([y_ref[e, pl.ds(wins[e], w), :] for e in range(N_EXPERTS)], axis=0)
        lane = lax.broadcasted_iota(I32, (1, LANES), 1)
        win_row = jnp.zeros((1, LANES), I32)
        for e in range(N_EXPERTS):
            win_row = jnp.where(lane == e, wins[e], win_row)
        rel = jnp.where(pos >= 0, jnp.clip(pos - win_row, -1, w), -1)
        src = lax.broadcasted_iota(I32, (LANES, N_EXPERTS * w), 0) * w
        col = lax.broadcasted_iota(I32, (LANES, N_EXPERTS * w), 1)
        spread = ((col >= src) & (col < src + w)).astype(BF16)
        wide = _dot(jnp.concatenate([rel.astype(F32).astype(BF16), aff.astype(BF16)], axis=0), spread)
        in_win = lax.broadcasted_iota(I32, (1, N_EXPERTS * w), 1) & (w - 1)
        gates = jnp.where(wide[:tile] == in_win.astype(F32), wide[tile:], 0.0).astype(BF16)
        _finish_combine(x_ref, g2_ref, fg_ref, o_ref, _dot(gates, ywin), final_norm)

    @pl.when(jnp.logical_not(fits))
    def _():
        lane = lax.broadcasted_iota(I32, (tile, cap), 1)
        acc = jnp.zeros(x_ref.shape[1:], F32)
        for e in range(N_EXPERTS):
            onehot = (lane == pos[:, e:e + 1]).astype(BF16)
            acc = acc + aff[:, e:e + 1] * _dot(onehot, y_ref[e])
        _finish_combine(x_ref, g2_ref, fg_ref, o_ref, acc, final_norm)


def _combine_windowed(x1, g2, pos_t, starts, aff, y, final_g, cap, final_norm):
    b, t, d = x1.shape
    tile = min(TOKEN_TILE, t)
    nt = t // tile
    row = lambda i, j, s: (i, j, 0)
    grid_spec = pltpu.PrefetchScalarGridSpec(
        num_scalar_prefetch=1,
        grid=(b, nt),
        in_specs=[pl.BlockSpec((1, tile, d), row),
                  pl.BlockSpec((1, 1, d), lambda i, j, s: (i, 0, 0)),
                  pl.BlockSpec((1, N_EXPERTS, tile), lambda i, j, s: (i, 0, j)),
                  pl.BlockSpec((1, tile, LANES), row),
                  pl.BlockSpec((N_EXPERTS, cap, d), lambda i, j, s: (0, i, 0)),
                  pl.BlockSpec((1, d), lambda i, j, s: (0, 0))],
        out_specs=pl.BlockSpec((1, tile, d), row),
    )
    return pl.pallas_call(
        functools.partial(_combine_win_kernel, cap=cap, ntp=nt + 1, final_norm=final_norm),
        grid_spec=grid_spec,
        out_shape=jax.ShapeDtypeStruct((b, t, d), F32),
        compiler_params=_params("arbitrary", "arbitrary"),
        name="ec_combine_win",
    )(starts[:, :, :nt + 1].reshape(-1), x1, g2, pos_t, aff, y, final_g)


def _ec_moe(streams, w_gate, w_up, w_down, layer, final_g, final_norm):
    b, _, d = streams[0][0].shape
    caps = [CAPACITY_FACTOR * s[0].shape[1] // N_EXPERTS for s in streams]
    rows = sum(b * cap for cap in caps)
    routes, first_blocks, xs = [], [], None
    row0 = 0
    for k, ((x1, h2, aff, aff_t, g2), cap) in enumerate(zip(streams, caps)):
        t = x1.shape[1]
        assert row0 % cap == 0 and cap % SLOT_ALIGN == 0, "a stream's row blocks must tile the slot buffer"
        pos_t, starts = _route(aff_t, cap)
        windowed = k == 0 and t > TOKEN_TILE and cap >= SLOT_WINDOW and (cap - SLOT_WINDOW) % SLOT_ALIGN == 0
        if windowed:
            xs = _gather_windowed(pos_t, starts, h2, cap, rows)
        else:
            xs = _gather_stacked(pos_t, h2, cap, rows, into=xs, first_block=row0 // cap)
        routes.append((pos_t, starts, aff, windowed))
        first_blocks.append(row0 // cap)
        row0 += b * cap
    y = _expert_ffn(xs, w_gate, w_up, w_down, layer)
    outs = []
    for (x1, h2, _, _, g2), cap, (pos_t, starts, aff, windowed), first in zip(streams, caps, routes, first_blocks):
        if windowed:
            outs.append(_combine_windowed(x1, g2, pos_t, starts, aff, y, final_g, cap, final_norm))
        else:
            outs.append(_combine(x1, g2, pos_t, aff, y, final_g, cap, final_norm, first))
    return outs


def _pre_gla_kernel(x_ref, g_ref, sh_ref, sc_ref, w_ref, wlr_ref, w2_ref, gb_ref,
                    q_ref, r_ref, k_ref, v_ref, lgf_ref, lgb_ref):
    qk = q_ref.shape[-1]
    vd = v_ref.shape[-1]
    tm = x_ref.shape[1]
    n_sub = max(1, tm // GLA_IN_SUBTILE_ROWS)
    subs = [slice(i * tm // n_sub, (i + 1) * tm // n_sub) for i in range(n_sub)]
    hs = [_norm_mod(x_ref[0, rs, :], g_ref[...], sh_ref[0], sc_ref[0]).astype(BF16) for rs in subs]
    ps = [_dot(h, w_ref[...]) for h in hs]
    lrs = [_dot(h, wlr_ref[...]).astype(BF16) for h in hs]
    zs = [_dot(lr, w2_ref[...]) + gb_ref[...] for lr in lrs]
    for rs, p, z in zip(subs, ps, zs):
        q_ref[0, rs, :] = (p[:, :qk] * (qk // GLA_HEADS) ** -0.5).astype(BF16)
        r_ref[0, rs, :] = _silu(p[:, qk:qk + vd]).astype(BF16)
        k_ref[0, rs, :] = p[:, qk + vd:2 * qk + vd].astype(BF16)
        v_ref[0, rs, :] = p[:, 2 * qk + vd:].astype(BF16)
        lg = (jnp.minimum(z, 0.0) - jnp.log1p(jnp.exp(-jnp.abs(z)))) * (1.0 / GLA_TAU)
        lgf_ref[0, rs, :] = lg[:, :qk]
        lgb_ref[0, rs, :] = lg[:, qk:]


def _pre_gla(x, g, shift, scale, w_main, w_lr, w2, gate_b):
    b, t, d = x.shape
    n_main = w_main.shape[1]
    qk = w2.shape[1] // 2
    vd = (n_main - 2 * qk) // 2
    tm = min(GLA_IN_ROW_TILE, t)
    row = lambda i, j: (i, j, 0)
    per_b = lambda i, j: (i, 0, 0)
    fixed = lambda i, j: (0, 0)
    out = lambda n, dt: (pl.BlockSpec((1, tm, n), row), jax.ShapeDtypeStruct((b, t, n), dt))
    outs = [out(qk, BF16), out(vd, BF16), out(qk, BF16), out(vd, BF16), out(qk, F32), out(qk, F32)]
    return pl.pallas_call(
        _pre_gla_kernel,
        grid=(b, t // tm),
        in_specs=[pl.BlockSpec((1, tm, d), row),
                  pl.BlockSpec((1, d), fixed),
                  pl.BlockSpec((1, 1, d), per_b),
                  pl.BlockSpec((1, 1, d), per_b),
                  pl.BlockSpec((d, n_main), fixed),
                  pl.BlockSpec((d, LANES), fixed),
                  pl.BlockSpec((LANES, 2 * qk), fixed),
                  pl.BlockSpec((1, 2 * qk), fixed)],
        out_specs=[o[0] for o in outs],
        out_shape=[o[1] for o in outs],
        compiler_params=_params("arbitrary", "arbitrary"),
        name="gla_in_proj",
    )(x, g, shift, scale, w_main, w_lr, w2, gate_b)


def _gla_masks(rows):
    ri = lax.broadcasted_iota(I32, (rows, rows), 0)
    ci = lax.broadcasted_iota(I32, (rows, rows), 1)
    same = (ri // GLA_CHUNK) == (ci // GLA_CHUNK)
    return same & (ri >= ci), same & (ri <= ci)


def _gla_tiles(jobs):
    ch = GLA_CHUNK
    n = len(jobs)
    rows, dk = jobs[0][1].shape
    n_chunks = rows // ch
    last = [[c * ch + (ch - 1 if job[6] else 0) for c in range(n_chunks)] for job in jobs]
    b = []
    for q, k, v, lg, st_ref, mask, forward in jobs:
        lg_hi, lg_lo = _split2(lg)
        m = mask.astype(BF16)
        b.append(_dot(m, lg_hi) + _dot(m, lg_lo))
    qt, kt, kd = [None] * n, [None] * n, [None] * n
    for i, (q, k, v, lg, st_ref, mask, forward) in enumerate(jobs):
        b_last = jnp.concatenate([jnp.broadcast_to(b[i][r:r + 1, :], (ch, dk)) for r in last[i]], axis=0)
        kf = k.astype(F32)
        kd[i] = (kf * jnp.exp(b_last - b[i])).astype(BF16)
        if q is not None:
            qt[i] = (q.astype(F32) * jnp.exp(b[i])).astype(BF16)
            kt[i] = (kf * jnp.exp(-b[i])).astype(BF16)
    a = [None if jobs[i][0] is None else lax.dot_general(qt[i], kt[i], NT_DIMS, preferred_element_type=F32)
         for i in range(n)]
    a = [None if a[i] is None else jnp.where(jobs[i][5], a[i], 0.0).astype(BF16) for i in range(n)]
    o_local = [None if a[i] is None else _dot(a[i], jobs[i][2]) for i in range(n)]
    d_st = [[lax.dot_general(jobs[i][2][c * ch:(c + 1) * ch], kd[i][c * ch:(c + 1) * ch], TN_DIMS,
                             preferred_element_type=F32) for c in range(n_chunks)] for i in range(n)]
    results = []
    for i, (q, k, v, lg, st_ref, mask, forward) in enumerate(jobs):
        st = st_ref[...]
        outs = [None] * n_chunks
        for c in (range(n_chunks) if forward else reversed(range(n_chunks))):
            rs = slice(c * ch, (c + 1) * ch)
            if q is not None:
                outs[c] = o_local[i][rs] + lax.dot_general(qt[i][rs], st.astype(BF16), NT_DIMS,
                                                           preferred_element_type=F32)
            st = jnp.exp(b[i][last[i][c]:last[i][c] + 1, :]) * st + d_st[i][c]
        st_ref[...] = st
        results.append(None if q is None else jnp.concatenate(outs, axis=0))
    return results


def _gla_kernel(q_ref, k_ref, v_ref, lgf_ref, lgb_ref, kc_ref, vc_ref, lgcf_ref, lgcb_ref, r_ref, ng_ref,
                o_ref, of_ref, ob_ref, stf_ref, stb_ref):
    t = q_ref.shape[1]
    t_ctx = kc_ref.shape[1]
    n_heads = stf_ref.shape[0]
    dv, dk = stf_ref.shape[1:]
    tile = min(GLA_TILE, t)
    ctile = min(GLA_TILE, t_ctx)
    mask_f, mask_b = _gla_masks(tile)
    cmask_f, cmask_b = (mask_f, mask_b) if ctile == tile else _gla_masks(ctile)
    stf_ref[...] = jnp.zeros_like(stf_ref)
    stb_ref[...] = jnp.zeros_like(stb_ref)
    kcols = [slice(h * dk, (h + 1) * dk) for h in range(n_heads)]
    vcols = [slice(h * dv, (h + 1) * dv) for h in range(n_heads)]

    n_ctx = t_ctx // ctile
    for i in range(n_ctx):
        rf = slice(i * ctile, (i + 1) * ctile)
        rb = slice((n_ctx - 1 - i) * ctile, (n_ctx - i) * ctile)
        jobs = []
        for h in range(n_heads):
            jobs.append((None, kc_ref[0, rf, kcols[h]], vc_ref[0, rf, vcols[h]], lgcf_ref[0, rf, kcols[h]],
                         stf_ref.at[h], cmask_f, True))
            jobs.append((None, kc_ref[0, rb, kcols[h]], vc_ref[0, rb, vcols[h]], lgcb_ref[0, rb, kcols[h]],
                         stb_ref.at[h], cmask_b, False))
        _gla_tiles(jobs)

    n = t // tile

    def body(i, carry):
        rf = pl.ds(pl.multiple_of(i * tile, tile), tile)
        rb = pl.ds(pl.multiple_of((n - 1 - i) * tile, tile), tile)
        jobs = []
        for h in range(n_heads):
            jobs.append((q_ref[0, rf, kcols[h]], k_ref[0, rf, kcols[h]], v_ref[0, rf, vcols[h]],
                         lgf_ref[0, rf, kcols[h]], stf_ref.at[h], mask_f, True))
            jobs.append((q_ref[0, rb, kcols[h]], k_ref[0, rb, kcols[h]], v_ref[0, rb, vcols[h]],
                         lgb_ref[0, rb, kcols[h]], stb_ref.at[h], mask_b, False))
        outs = _gla_tiles(jobs)
        for h in range(n_heads):
            of_ref[rf, vcols[h]] = outs[2 * h]
            ob_ref[rb, vcols[h]] = outs[2 * h + 1]
        return carry

    lax.fori_loop(0, n, body, 0)

    def readout(i, carry):
        rows = pl.ds(pl.multiple_of(i * tile, tile), tile)
        for h in range(n_heads):
            o = of_ref[rows, vcols[h]] + ob_ref[rows, vcols[h]]
            o = o * lax.rsqrt(jnp.mean(o * o, axis=-1, keepdims=True) + NORM_EPS) * ng_ref[:, vcols[h]]
            o_ref[0, rows, vcols[h]] = (o * r_ref[0, rows, vcols[h]].astype(F32)).astype(o_ref.dtype)
        return carry

    lax.fori_loop(0, n, readout, 0)


def _gla(q, k, v, lgf, lgb, kc, vc, lgcf, lgcb, r, norm_g):
    b, t, qk = q.shape
    vd = v.shape[-1]
    t_ctx = kc.shape[1]
    dk, dv = qk // GLA_HEADS, vd // GLA_HEADS
    hp = GLA_HEADS_PER_STEP
    head = lambda i, h: (i, 0, h)
    lat = lambda n: pl.BlockSpec((1, t, hp * n), head)
    ctx = lambda n: pl.BlockSpec((1, t_ctx, hp * n), head)
    return pl.pallas_call(
        _gla_kernel,
        grid=(b, GLA_HEADS // hp),
        in_specs=[lat(dk), lat(dk), lat(dv), lat(dk), lat(dk),
                  ctx(dk), ctx(dv), ctx(dk), ctx(dk),
                  lat(dv),
                  pl.BlockSpec((1, hp * dv), lambda i, h: (0, h))],
        out_specs=lat(dv),
        out_shape=jax.ShapeDtypeStruct((b, t, vd), BF16),
        scratch_shapes=[pltpu.VMEM((t, hp * dv), F32), pltpu.VMEM((t, hp * dv), F32),
                        pltpu.VMEM((hp, dv, dk), F32), pltpu.VMEM((hp, dv, dk), F32)],
        compiler_params=_params("arbitrary", "arbitrary"),
        name="gla_scan",
    )(q, k, v, lgf, lgb, kc, vc, lgcf, lgcb, r, norm_g)


def _rope_tables(t):
    pos = jnp.arange(t)
    row = (pos // GRID_W).astype(F32)
    col = (pos % GRID_W).astype(F32)
    inv = ROPE_BASE ** (-jnp.arange(ROPE_PAIRS, dtype=F32) / ROPE_PAIRS)
    ar, ac = row[:, None] * inv, col[:, None] * inv
    ang = jnp.concatenate([ar, ar, ac, ac], axis=-1)
    sign = jnp.tile(jnp.repeat(jnp.array([-1.0, 1.0], F32), ROPE_PAIRS), 2)
    reps = LANES // HEAD_DIM
    return jnp.tile(jnp.cos(ang), (1, reps)), jnp.tile(jnp.sin(ang) * sign, (1, reps))


def _router_split(w):
    d, n = w.shape
    w = jnp.pad(w, ((0, 0), (0, LANES - n)))
    hi = w.astype(BF16)
    return hi, (w - hi.astype(F32)).astype(BF16)


def kernel(x, c, ctx, c_ctx, ada_w, ada_b, norm1_g, norm2_g, attn_w_in, attn_w_out, attn_sink, gla_w_in, gla_gate_w2, gla_gate_b, gla_norm_g, gla_w_out, router_w, exp_w_gate, exp_w_up, exp_w_down, final_norm_g):
    b, t, d = x.shape
    depth = ada_w.shape[0]
    assert depth == 2, "layer 0 is windowed attention with context output, layer 1 is GLA and last"
    rows = -(-(b + 1) // 16) * 16
    src = jnp.concatenate([c, c_ctx[None, :], jnp.zeros((rows - b - 1, d), F32)], axis=0)
    cos, sin_signed = _rope_tables(t)
    final_g = final_norm_g.reshape(1, d)

    def chunks(i):
        mod = _modulation(src, ada_w, ada_b, i)
        lat = [mod[:b, j * d:(j + 1) * d].reshape(b, 1, d) for j in range(6)]
        cx = [jnp.broadcast_to(mod[b, j * d:(j + 1) * d].reshape(1, 1, d), (b, 1, d)) for j in range(6)]
        return lat, cx

    (sh1, sc1, g1, sh2, sc2, g2), (sh1c, sc1c, g1c, sh2c, sc2c, g2c) = chunks(0)
    n1, n2 = norm1_g[0].reshape(1, d), norm2_g[0].reshape(1, d)
    w_in = attn_w_in[0].astype(BF16)
    w_out = attn_w_out[0].astype(BF16)
    sink = attn_sink[0].reshape(1, -1)
    wr_hi, wr_lo = _router_split(router_w[0])
    q, k, v = _pre_attn(x, n1, sh1, sc1, w_in, cos, sin_signed, rope=True)
    t_ctx = ctx.shape[1]
    qc, kc, vc = _pre_attn(ctx, n1, sh1c, sc1c, w_in, cos[:t_ctx], sin_signed[:t_ctx], rope=False)
    o = _attn_band(q, k, v, kc, vc, sink)
    oc = _attn_ctx(qc, kc, vc, sink)
    lat = _post_mixer(o, w_out, x, g1, n2, sh2, sc2, wr_hi, wr_lo)
    cxt = _post_mixer(oc, w_out, ctx, g1c, n2, sh2c, sc2c, wr_hi, wr_lo)
    x, xc = _ec_moe([(*lat, g2), (*cxt, g2c)], exp_w_gate, exp_w_up, exp_w_down, 0, final_g, False)

    (sh1, sc1, g1, sh2, sc2, g2), (sh1c, sc1c, _, _, _, _) = chunks(1)
    n1, n2 = norm1_g[1].reshape(1, d), norm2_g[1].reshape(1, d)
    qk = gla_gate_w2.shape[-1]
    n_main = gla_w_in.shape[-1] - 2 * GLA_RANK
    w_main = gla_w_in[0][:, :n_main].astype(BF16)
    w_lr = jnp.pad(gla_w_in[0][:, n_main:], ((0, 0), (0, LANES - 2 * GLA_RANK))).astype(BF16)
    w2 = jnp.zeros((LANES, 2 * qk), F32)
    w2 = w2.at[:GLA_RANK, :qk].set(gla_gate_w2[0, 0]).at[GLA_RANK:2 * GLA_RANK, qk:].set(gla_gate_w2[0, 1])
    w2 = w2.astype(BF16)
    gate_b = gla_gate_b[0].reshape(1, 2 * qk)
    gq, gr, gk, gv, lgf, lgb = _pre_gla(x, n1, sh1, sc1, w_main, w_lr, w2, gate_b)
    _, _, gkc, gvc, lgcf, lgcb = _pre_gla(xc, n1, sh1c, sc1c, w_main, w_lr, w2, gate_b)
    og = _gla(gq, gk, gv, lgf, lgb, gkc, gvc, lgcf, lgcb, gr, gla_norm_g[0].reshape(1, -1))
    wr_hi, wr_lo = _router_split(router_w[1])
    lat = _post_mixer(og, gla_w_out[0].astype(BF16), x, g1, n2, sh2, sc2, wr_hi, wr_lo)
    return _ec_moe([(*lat, g2)], exp_w_gate, exp_w_up, exp_w_down, 1, final_g, True)[0]
```

```python
import functools

import jax
import jax.numpy as jnp
from jax import lax
from jax.experimental import pallas as pl
from jax.experimental.pallas import tpu as pltpu

F32 = jnp.float32
BF16 = jnp.bfloat16
I32 = jnp.int32

LANES = 128
HEAD_DIM = 64
ATT_GROUP = 4
ATT_BLOCK = 128
ATT_BLOCKS_PER_STEP = 4
GRID_W = 64
ROPE_BASE = 10000.0
ROPE_PAIRS = HEAD_DIM // 4
GLA_HEADS = 4
GLA_RANK = 16
GLA_TAU = 16.0
GLA_CHUNK = 64
N_EXPERTS = 16
CAPACITY_FACTOR = 2
NORM_EPS = 1e-6
LOG2E = 1.4426950408889634
ROW_TILE = 512
ATTN_IN_ROW_TILE = 512
ATTN_IN_SUBTILE_ROWS = 256
GLA_IN_ROW_TILE = 512
GLA_IN_SUBTILE_ROWS = 256
POST_SUBTILES = 2
TOKEN_TILE = 256
MOE_ROW_TILE = 4608
MOE_DOWN_ROW_TILE = 1024
FF_TILE = 256
FEW_ROWS = 512
FF_TILE_FEW_ROWS = 1408
GLA_TILE = 256
GLA_HEADS_PER_STEP = 2
SLOT_WINDOW = 64
SLOT_ALIGN = 16
VMEM_LIMIT = 56 * 1024 * 1024

NT_DIMS = (((1,), (1,)), ((), ()))
TN_DIMS = (((0,), (0,)), ((), ()))


def _params(*sem):
    return pltpu.CompilerParams(dimension_semantics=sem, vmem_limit_bytes=VMEM_LIMIT)


def _dot(a, b):
    return jnp.dot(a, b, preferred_element_type=F32)


def _split2(a):
    hi = a.astype(BF16)
    lo = (a - hi.astype(F32)).astype(BF16)
    return hi, lo


def _dot3(a, b_hi, b_lo):
    a_hi, a_lo = _split2(a)
    return _dot(a_hi, b_hi) + _dot(a_hi, b_lo) + _dot(a_lo, b_hi)


def _silu(a):
    return a * jax.nn.sigmoid(a)


def _norm_mod(x, g, shift, scale):
    y = x * lax.rsqrt(jnp.mean(x * x, axis=-1, keepdims=True) + NORM_EPS)
    return (y * g) * (1.0 + scale) + shift


def _mod_kernel(src_ref, w_ref, b_ref, o_ref):
    w_hi, w_lo = _split2(w_ref[0])
    o_ref[...] = _dot3(_silu(src_ref[...]), w_hi, w_lo) + b_ref[0]


def _modulation(src, ada_w, ada_b, layer):
    rows, d = src.shape
    depth, _, n = ada_w.shape
    tn = 512
    return pl.pallas_call(
        _mod_kernel,
        grid=(n // tn,),
        in_specs=[pl.BlockSpec((rows, d), lambda j: (0, 0)),
                  pl.BlockSpec((1, d, tn), lambda j: (layer, 0, j)),
                  pl.BlockSpec((1, 1, tn), lambda j: (layer, 0, j))],
        out_specs=pl.BlockSpec((rows, tn), lambda j: (0, j)),
        out_shape=jax.ShapeDtypeStruct((rows, n), F32),
        compiler_params=_params("arbitrary"),
        name="adaln_mod",
    )(src, ada_w, ada_b.reshape(depth, 1, n))


def _rope(x, cos, sin_signed, lane_lo):
    outs = []
    for j in range(x.shape[1] // LANES):
        xb = x[:, j * LANES:(j + 1) * LANES]
        partner = jnp.where(lane_lo, pltpu.roll(xb, LANES - ROPE_PAIRS, 1), pltpu.roll(xb, ROPE_PAIRS, 1))
        outs.append(xb * cos + partner * sin_signed)
    return jnp.concatenate(outs, axis=1)


def _pre_attn_kernel(x_ref, g_ref, sh_ref, sc_ref, w_ref, cos_ref, sin_ref, q_ref, k_ref, v_ref, *, rope):
    d = x_ref.shape[-1]
    kv = k_ref.shape[-1]
    tm = x_ref.shape[1]
    n_sub = max(1, tm // ATTN_IN_SUBTILE_ROWS)
    subs = [slice(i * tm // n_sub, (i + 1) * tm // n_sub) for i in range(n_sub)]
    hs = [_norm_mod(x_ref[0, rs, :], g_ref[...], sh_ref[0], sc_ref[0]).astype(BF16) for rs in subs]
    ps = [_dot(h, w_ref[...]) for h in hs]
    for rs, p in zip(subs, ps):
        q, k, v = p[:, :d], p[:, d:d + kv], p[:, d + kv:]
        if rope:
            lane = lax.broadcasted_iota(I32, (tm // n_sub, LANES), 1)
            lane_lo = (lane & (2 * ROPE_PAIRS - 1)) < ROPE_PAIRS
            cos, sin_signed = cos_ref[rs, :], sin_ref[rs, :]
            q = _rope(q, cos, sin_signed, lane_lo)
            k = _rope(k, cos, sin_signed, lane_lo)
        q_ref[0, rs, :] = (q * (HEAD_DIM ** -0.5 * LOG2E)).astype(BF16)
        k_ref[0, rs, :] = k.astype(BF16)
        v_ref[0, rs, :] = v.astype(BF16)


def _pre_attn(x, g, shift, scale, w_in, cos, sin_signed, rope):
    b, t, d = x.shape
    n_in = w_in.shape[1]
    kv = (n_in - d) // 2
    tm = min(ATTN_IN_ROW_TILE, t)
    row = lambda i, j: (i, j, 0)
    per_b = lambda i, j: (i, 0, 0)
    fixed = lambda i, j: (0, 0)
    return pl.pallas_call(
        functools.partial(_pre_attn_kernel, rope=rope),
        grid=(b, t // tm),
        in_specs=[pl.BlockSpec((1, tm, d), row),
                  pl.BlockSpec((1, d), fixed),
                  pl.BlockSpec((1, 1, d), per_b),
                  pl.BlockSpec((1, 1, d), per_b),
                  pl.BlockSpec((d, n_in), fixed),
                  pl.BlockSpec((tm, LANES), lambda i, j: (j, 0)),
                  pl.BlockSpec((tm, LANES), lambda i, j: (j, 0))],
        out_specs=[pl.BlockSpec((1, tm, d), row),
                   pl.BlockSpec((1, tm, kv), row),
                   pl.BlockSpec((1, tm, kv), row)],
        out_shape=[jax.ShapeDtypeStruct((b, t, d), BF16),
                   jax.ShapeDtypeStruct((b, t, kv), BF16),
                   jax.ShapeDtypeStruct((b, t, kv), BF16)],
        compiler_params=_params("arbitrary", "arbitrary"),
        name="attn_in_proj",
    )(x, g, shift, scale, w_in, cos, sin_signed)


def _attn_heads(groups, sink_ref, o_ref):
    tq = groups[0][0].shape[0]
    n_kv = groups[0][1][0].shape[1] // HEAD_DIM
    lane = lax.broadcasted_iota(I32, (1, ATT_GROUP * tq), 1)
    v_lane = lax.broadcasted_iota(I32, (1, LANES), 1)
    jobs = [(group, h) for h in range(n_kv) for group in groups]
    n_jobs = len(jobs)
    sks, scores, ms, es, both = ([None] * n_jobs for _ in range(5))

    def score_stage(job):
        (q, k_parts, v_parts, bias_parts, rows), h = jobs[job]
        heads = [ATT_GROUP * h + g for g in range(ATT_GROUP)]
        qg = jnp.concatenate([q[:, i * HEAD_DIM:(i + 1) * HEAD_DIM] for i in heads], axis=0)
        sk = jnp.full((1, ATT_GROUP * tq), sink_ref[0, heads[0]], F32)
        for g in range(1, ATT_GROUP):
            sk = jnp.where(lane >= g * tq, sink_ref[0, heads[g]], sk)
        sks[job] = sk * LOG2E
        m = sks[job]
        scores[job] = []
        for kp, bias in zip(k_parts, bias_parts):
            s = lax.dot_general(kp[:, h * HEAD_DIM:(h + 1) * HEAD_DIM], qg, NT_DIMS, preferred_element_type=F32)
            s = s if bias is None else s + bias
            m = jnp.maximum(m, jnp.max(s, axis=0, keepdims=True))
            scores[job].append(s)
        ms[job] = m

    def weight_stage(job):
        es[job] = jnp.concatenate([jnp.exp2(s - ms[job]).astype(BF16) for s in scores[job]], axis=0)

    def value_stage(job):
        (q, k_parts, v_parts, bias_parts, rows), h = jobs[job]
        block = (h * HEAD_DIM // LANES) * LANES
        upper = (h * HEAD_DIM) % LANES != 0
        v_pair = jnp.concatenate([vp[:, block:block + LANES] for vp in v_parts], axis=0)
        own = (v_lane >= HEAD_DIM) if upper else (v_lane < HEAD_DIM)
        v_aug = jnp.where(own, v_pair, jnp.ones_like(v_pair))
        both[job] = lax.dot_general(v_aug, es[job], TN_DIMS, preferred_element_type=F32)

    stages = (score_stage, weight_stage, value_stage)
    for step in range(n_jobs + len(stages) - 1):
        for depth, stage in enumerate(stages):
            if 0 <= step - depth < n_jobs:
                stage(step - depth)
    for gi, group in enumerate(groups):
        outs = []
        for h in range(n_kv):
            job = h * len(groups) + gi
            upper = (h * HEAD_DIM) % LANES != 0
            num = both[job][HEAD_DIM:] if upper else both[job][:HEAD_DIM]
            total = both[job][:1] if upper else both[job][HEAD_DIM:HEAD_DIM + 1]
            o_t = num / (total + jnp.exp2(sks[job] - ms[job]))
            outs.extend(o_t[:, g * tq:(g + 1) * tq] for g in range(ATT_GROUP))
        o_ref[0, group[4], :] = jnp.concatenate(outs, axis=0).T.astype(o_ref.dtype)


def _band_bias():
    blk = ATT_BLOCK
    key = jnp.arange(blk)[:, None]
    qry = jnp.arange(ATT_GROUP * blk)[None, :] % blk
    prev_ok = key >= qry
    next_ok = key <= qry
    never = jnp.zeros_like(prev_ok)
    variants = [jnp.stack([never if first else prev_ok, never if last else next_ok])
                for last in (False, True) for first in (False, True)]
    return jnp.where(jnp.stack(variants), 0.0, -jnp.inf).astype(F32)


def _attn_band_kernel(*refs, qb):
    q_ref, o_ref, sink_ref = refs[0], refs[-1], refs[-2]
    k_refs = refs[1:qb + 3]
    v_refs = refs[qb + 3:2 * qb + 5]
    kc_ref, vc_ref = refs[2 * qb + 5], refs[2 * qb + 6]
    bias_refs = refs[2 * qb + 7:3 * qb + 7]
    blk = ATT_BLOCK
    groups = []
    for g in range(qb):
        rows = slice(g * blk, (g + 1) * blk)
        groups.append((q_ref[0, rows, :],
                       [kc_ref[0]] + [r[0] for r in k_refs[g:g + 3]],
                       [vc_ref[0]] + [r[0] for r in v_refs[g:g + 3]],
                       [None, bias_refs[g][0, 0], None, bias_refs[g][0, 1]], rows))
    _attn_heads(groups, sink_ref, o_ref)


def _attn_ctx_kernel(q_ref, kc_ref, vc_ref, sink_ref, o_ref):
    _attn_heads([(q_ref[0], [kc_ref[0]], [vc_ref[0]], [None], slice(None))], sink_ref, o_ref)


def _attn_band(q, k, v, kc, vc, sink):
    b, t, d = q.shape
    kv = k.shape[-1]
    n_ctx = kc.shape[1]
    nb = t // ATT_BLOCK
    qb = ATT_BLOCKS_PER_STEP if nb % ATT_BLOCKS_PER_STEP == 0 else 1
    steps = nb // qb
    kv_spec = lambda m: pl.BlockSpec((1, ATT_BLOCK, kv), lambda i, j: (i, jnp.clip(qb * j - 1 + m, 0, nb - 1), 0))
    kv_specs = [kv_spec(m) for m in range(qb + 2)]
    ctx_spec = pl.BlockSpec((1, n_ctx, kv), lambda i, j: (i, 0, 0))
    bias = _band_bias()

    def bias_spec(g):
        first = (lambda j: (j == 0).astype(I32)) if g == 0 else (lambda j: 0)
        last = (lambda j: 2 * (j == steps - 1).astype(I32)) if g == qb - 1 else (lambda j: 0)
        return pl.BlockSpec((1,) + bias.shape[1:], lambda i, j: (first(j) + last(j), 0, 0, 0))

    step_rows = pl.BlockSpec((1, qb * ATT_BLOCK, d), lambda i, j: (i, j, 0))
    return pl.pallas_call(
        functools.partial(_attn_band_kernel, qb=qb),
        grid=(b, steps),
        in_specs=[step_rows] + kv_specs + kv_specs + [ctx_spec, ctx_spec] + [bias_spec(g) for g in range(qb)]
                 + [pl.BlockSpec(memory_space=pltpu.SMEM)],
        out_specs=step_rows,
        out_shape=jax.ShapeDtypeStruct((b, t, d), BF16),
        compiler_params=_params("arbitrary", "arbitrary"),
        name="attn_band",
    )(q, *([k] * (qb + 2)), *([v] * (qb + 2)), kc, vc, *([bias] * qb), sink)


def _attn_ctx(qc, kc, vc, sink):
    b, n_ctx, d = qc.shape
    kv = kc.shape[-1]
    return pl.pallas_call(
        _attn_ctx_kernel,
        grid=(b,),
        in_specs=[pl.BlockSpec((1, n_ctx, d), lambda i: (i, 0, 0)),
                  pl.BlockSpec((1, n_ctx, kv), lambda i: (i, 0, 0)),
                  pl.BlockSpec((1, n_ctx, kv), lambda i: (i, 0, 0)),
                  pl.BlockSpec(memory_space=pltpu.SMEM)],
        out_specs=pl.BlockSpec((1, n_ctx, d), lambda i: (i, 0, 0)),
        out_shape=jax.ShapeDtypeStruct((b, n_ctx, d), BF16),
        compiler_params=_params("arbitrary"),
        name="attn_ctx",
    )(qc, kc, vc, sink)


def _post_kernel(o_ref, w_ref, x_ref, g1_ref, n2_ref, sh_ref, sc_ref, wr_hi_ref, wr_lo_ref,
                 x1_ref, h2_ref, aff_ref, aff_t_ref):
    tm = x_ref.shape[1]
    subs = [slice(i * tm // POST_SUBTILES, (i + 1) * tm // POST_SUBTILES) for i in range(POST_SUBTILES)]
    ys = [_dot(o_ref[0, rs, :], w_ref[...]) for rs in subs]
    h2s = []
    for rs, y in zip(subs, ys):
        x1 = x_ref[0, rs, :] + g1_ref[0] * y
        x1_ref[0, rs, :] = x1
        h2 = _norm_mod(x1, n2_ref[...], sh_ref[0], sc_ref[0])
        h2_ref[0, rs, :] = h2.astype(BF16)
        h2s.append(h2)
    lane = lax.broadcasted_iota(I32, (tm // POST_SUBTILES, LANES), 1)
    for rs, h2 in zip(subs, h2s):
        lg = jnp.where(lane < N_EXPERTS, _dot3(h2, wr_hi_ref[...], wr_lo_ref[...]), -jnp.inf)
        e = jnp.exp(lg - jnp.max(lg, axis=-1, keepdims=True))
        aff = e / jnp.sum(e, axis=-1, keepdims=True)
        aff_ref[0, rs, :] = aff
        aff_t_ref[0, :, rs] = aff.T[:N_EXPERTS]


def _post_mixer(o, w_out, x, g1, n2, shift, scale, wr_hi, wr_lo):
    b, t, d = x.shape
    k_in = o.shape[-1]
    tm = min(ROW_TILE, t)
    row = lambda i, j: (i, j, 0)
    per_b = lambda i, j: (i, 0, 0)
    fixed = lambda i, j: (0, 0)
    return pl.pallas_call(
        _post_kernel,
        grid=(b, t // tm),
        in_specs=[pl.BlockSpec((1, tm, k_in), row),
                  pl.BlockSpec((k_in, d), fixed),
                  pl.BlockSpec((1, tm, d), row),
                  pl.BlockSpec((1, 1, d), per_b),
                  pl.BlockSpec((1, d), fixed),
                  pl.BlockSpec((1, 1, d), per_b),
                  pl.BlockSpec((1, 1, d), per_b),
                  pl.BlockSpec((d, LANES), fixed),
                  pl.BlockSpec((d, LANES), fixed)],
        out_specs=[pl.BlockSpec((1, tm, d), row),
                   pl.BlockSpec((1, tm, d), row),
                   pl.BlockSpec((1, tm, LANES), row),
                   pl.BlockSpec((1, N_EXPERTS, tm), lambda i, j: (i, 0, j))],
        out_shape=[jax.ShapeDtypeStruct((b, t, d), F32),
                   jax.ShapeDtypeStruct((b, t, d), BF16),
                   jax.ShapeDtypeStruct((b, t, LANES), F32),
                   jax.ShapeDtypeStruct((b, N_EXPERTS, t), F32)],
        compiler_params=_params("arbitrary", "arbitrary"),
        name="mixer_out_proj",
    )(o, w_out, x, g1, n2, shift, scale, wr_hi, wr_lo)


def _exclusive_cumsum_lanes(mask):
    rows, t = mask.shape
    width = min(2 * LANES, t)
    r = lax.broadcasted_iota(I32, (width, width), 0)
    c = lax.broadcasted_iota(I32, (width, width), 1)
    upper = (r < c).astype(BF16)
    carry = jnp.zeros((rows, 1), F32)
    outs = []
    for ch in range(t // width):
        m = mask[:, ch * width:(ch + 1) * width]
        outs.append(_dot(m.astype(BF16), upper) + carry)
        carry = carry + jnp.sum(m, axis=1, keepdims=True)
    return jnp.concatenate(outs, axis=1)


LOG_FLOOR = -160.0
BISECT_LOG_STEPS = 40
BISECT_STEPS = 12


def _select_kernel(aff_t_ref, pos_t_ref, starts_ref, *, cap):
    rows = aff_t_ref.shape[0]

    def enough(threshold):
        return jnp.sum((aff_t_ref[...] >= threshold).astype(F32), axis=1, keepdims=True) >= cap

    def bisect_log(_, carry):
        lo_u, hi_u = carry
        mid = lo_u + (hi_u - lo_u) * 0.5
        ok = enough(jnp.exp2(mid))
        return jnp.where(ok, mid, lo_u), jnp.where(ok, hi_u, mid)

    def bisect(_, carry):
        lo, hi = carry
        mid = lo + (hi - lo) * 0.5
        ok = enough(mid)
        return jnp.where(ok, mid, lo), jnp.where(ok, hi, mid)

    lo_u, hi_u = lax.fori_loop(0, BISECT_LOG_STEPS, bisect_log,
                               (jnp.full((rows, 1), LOG_FLOOR, F32), jnp.full((rows, 1), 1.0, F32)))
    lo, hi = lax.fori_loop(0, BISECT_STEPS, bisect, (jnp.exp2(lo_u), jnp.exp2(hi_u)))
    aff_t = aff_t_ref[...]
    above = (aff_t >= hi).astype(F32)
    tied = ((aff_t >= lo) & (aff_t < hi)).astype(F32)
    need = cap - jnp.sum(above, axis=1, keepdims=True)
    sel = above + tied * (_exclusive_cumsum_lanes(tied) < need).astype(F32)
    slot = _exclusive_cumsum_lanes(sel)
    pos_t_ref[...] = jnp.where(sel > 0, slot, -1.0).astype(I32)
    t = aff_t.shape[1]
    tok = lax.broadcasted_iota(I32, (t, LANES), 0)
    edge = lax.broadcasted_iota(I32, (t, LANES), 1) * min(TOKEN_TILE, t)
    before = ((tok < edge) & (edge <= t)).astype(BF16)
    starts_ref[...] = _dot(sel.astype(BF16), before).astype(I32)


def _route(aff_t, cap):
    b, _, t = aff_t.shape
    rows = b * N_EXPERTS
    pos_t, starts = pl.pallas_call(
        functools.partial(_select_kernel, cap=cap),
        grid=(1,),
        in_specs=[pl.BlockSpec((rows, t), lambda i: (0, 0))],
        out_specs=[pl.BlockSpec((rows, t), lambda i: (0, 0)),
                   pl.BlockSpec((rows, LANES), lambda i: (0, 0))],
        out_shape=[jax.ShapeDtypeStruct((rows, t), I32),
                   jax.ShapeDtypeStruct((rows, LANES), I32)],
        compiler_params=_params("arbitrary"),
        name="ec_select",
    )(aff_t.reshape(rows, t))
    return pos_t.reshape(b, N_EXPERTS, t), starts.reshape(b, N_EXPERTS, LANES)


def _gather_stacked_kernel(pos_t_ref, h_ref, *rest, cap):
    xs_ref = rest[-1]
    t = h_ref.shape[1]
    pos_t = pos_t_ref[0]
    slot = lax.broadcasted_iota(I32, (cap, t), 0)
    onehot = jnp.concatenate([(slot == pos_t[e:e + 1, :]).astype(BF16) for e in range(N_EXPERTS)], axis=0)
    z = _dot(onehot, h_ref[0]).astype(BF16)
    for e in range(N_EXPERTS):
        xs_ref[e] = z[e * cap:(e + 1) * cap]


def _gather_stacked(pos_t, h, cap, rows, into=None, first_block=0):
    b, t, d = h.shape
    in_specs = [pl.BlockSpec((1, N_EXPERTS, t), lambda i: (i, 0, 0)),
                pl.BlockSpec((1, t, d), lambda i: (i, 0, 0))]
    args = [pos_t, h]
    if into is not None:
        assert into.shape == (N_EXPERTS, rows, d)
        in_specs.append(pl.BlockSpec(memory_space=pl.ANY))
        args.append(into)
    return pl.pallas_call(
        functools.partial(_gather_stacked_kernel, cap=cap),
        grid=(b,),
        in_specs=in_specs,
        out_specs=pl.BlockSpec((N_EXPERTS, cap, d), lambda i: (0, first_block + i, 0)),
        out_shape=jax.ShapeDtypeStruct((N_EXPERTS, rows, d), BF16),
        input_output_aliases={} if into is None else {2: 0},
        compiler_params=_params("arbitrary"),
        name="ec_gather_stacked",
    )(*args)


def _tile_windows(starts_ref, cap, ntp, sample):
    j = pl.program_id(1)
    wins, fits = [], None
    for e in range(N_EXPERTS):
        at = (sample * N_EXPERTS + e) * ntp + j
        win = jnp.minimum((starts_ref[at] // SLOT_ALIGN) * SLOT_ALIGN, cap - SLOT_WINDOW)
        ok = starts_ref[at + 1] - win <= SLOT_WINDOW
        wins.append(pl.multiple_of(win, SLOT_ALIGN))
        fits = ok if fits is None else jnp.logical_and(fits, ok)
    return wins, fits


def _gather_win_kernel(starts_ref, pos_t_ref, h_ref, xs_ref, *, cap, ntp, n_samples):
    w = SLOT_WINDOW
    tile = h_ref.shape[1]
    real = pl.program_id(0) < n_samples
    wins, fits = _tile_windows(starts_ref, cap, ntp, jnp.minimum(pl.program_id(0), n_samples - 1))
    h = h_ref[0]
    pos_t = pos_t_ref[0]

    @pl.when(pl.program_id(1) == 0)
    def _():
        xs_ref[...] = jnp.zeros_like(xs_ref)

    @pl.when(jnp.logical_and(real, fits))
    def _():
        row = lax.broadcasted_iota(I32, (w, tile), 0)
        onehot = jnp.concatenate([(row + wins[e] == pos_t[e:e + 1, :]).astype(BF16) for e in range(N_EXPERTS)],
                                 axis=0)
        z = _dot(onehot, h).astype(BF16)
        for e in range(N_EXPERTS):
            xs_ref[e, pl.ds(wins[e], w), :] += z[e * w:(e + 1) * w]

    @pl.when(jnp.logical_and(real, jnp.logical_not(fits)))
    def _():
        slot = lax.broadcasted_iota(I32, (cap, tile), 0)
        for e in range(N_EXPERTS):
            onehot = (slot == pos_t[e:e + 1, :]).astype(BF16)
            xs_ref[e] += _dot(onehot, h).astype(BF16)


def _gather_windowed(pos_t, starts, h, cap, rows):
    b, t, d = h.shape
    tile = min(TOKEN_TILE, t)
    nt = t // tile
    spare_blocks = -(-(rows - b * cap) // cap)
    sample = lambda i: jnp.minimum(i, b - 1)
    grid_spec = pltpu.PrefetchScalarGridSpec(
        num_scalar_prefetch=1,
        grid=(b + spare_blocks, nt),
        in_specs=[pl.BlockSpec((1, N_EXPERTS, tile), lambda i, j, s: (sample(i), 0, j)),
                  pl.BlockSpec((1, tile, d), lambda i, j, s: (sample(i), j, 0))],
        out_specs=pl.BlockSpec((N_EXPERTS, cap, d), lambda i, j, s: (0, i, 0)),
    )
    return pl.pallas_call(
        functools.partial(_gather_win_kernel, cap=cap, ntp=nt + 1, n_samples=b),
        grid_spec=grid_spec,
        out_shape=jax.ShapeDtypeStruct((N_EXPERTS, rows, d), BF16),
        compiler_params=_params("arbitrary", "arbitrary"),
        name="ec_gather_win",
    )(starts[:, :, :nt + 1].reshape(-1), pos_t, h)


def _ffn_up_kernel(xs_ref, wg_ref, wu_ref, hm_ref):
    xs = xs_ref[0]
    a = _dot(xs, wg_ref[0, 0].astype(BF16))
    u = _dot(xs, wu_ref[0, 0].astype(BF16))
    hm_ref[0] = (_silu(a) * u).astype(hm_ref.dtype)


def _ffn_down_kernel(hm_ref, wd_ref, y_ref, wd_bf16_ref):
    @pl.when(pl.program_id(1) == 0)
    def _():
        wd_bf16_ref[...] = wd_ref[0, 0].astype(BF16)

    y_ref[0] = _dot(hm_ref[0], wd_bf16_ref[...]).astype(y_ref.dtype)


def _largest_divisor(rows, limit):
    if rows <= limit:
        return rows
    tile = (limit // 256) * 256
    while tile and rows % tile:
        tile -= 256
    assert tile, f"no 256-multiple tile <= {limit} divides {rows} rows"
    return tile


def _expert_ffn(xs, w_gate, w_up, w_down, layer):
    n_exp, rows, d = xs.shape
    ff = w_gate.shape[-1]
    tm = _largest_divisor(rows, MOE_ROW_TILE)
    tf = min(FF_TILE, ff)
    if rows <= FEW_ROWS and ff % FF_TILE_FEW_ROWS == 0:
        tf = FF_TILE_FEW_ROWS
    hm = pl.pallas_call(
        _ffn_up_kernel,
        grid=(n_exp, rows // tm, ff // tf),
        in_specs=[pl.BlockSpec((1, tm, d), lambda e, m, f: (e, m, 0)),
                  pl.BlockSpec((1, 1, d, tf), lambda e, m, f: (layer, e, 0, f)),
                  pl.BlockSpec((1, 1, d, tf), lambda e, m, f: (layer, e, 0, f))],
        out_specs=pl.BlockSpec((1, tm, tf), lambda e, m, f: (e, m, f)),
        out_shape=jax.ShapeDtypeStruct((n_exp, rows, ff), BF16),
        compiler_params=_params("arbitrary", "arbitrary", "arbitrary"),
        name="ec_ffn_up",
    )(xs, w_gate, w_up)
    tm2 = _largest_divisor(rows, MOE_DOWN_ROW_TILE)
    return pl.pallas_call(
        _ffn_down_kernel,
        grid=(n_exp, rows // tm2),
        in_specs=[pl.BlockSpec((1, tm2, ff), lambda e, m: (e, m, 0)),
                  pl.BlockSpec((1, 1, ff, d), lambda e, m: (layer, e, 0, 0))],
        out_specs=pl.BlockSpec((1, tm2, d), lambda e, m: (e, m, 0)),
        out_shape=jax.ShapeDtypeStruct((n_exp, rows, d), BF16),
        scratch_shapes=[pltpu.VMEM((ff, d), BF16)],
        compiler_params=_params("arbitrary", "arbitrary"),
        name="ec_ffn_down",
    )(hm, w_down)


def _combine_kernel(x_ref, g2_ref, pos_t_ref, aff_ref, y_ref, fg_ref, o_ref, *, cap, final_norm):
    tt = x_ref.shape[1]
    pos = _token_major(pos_t_ref)
    aff = aff_ref[0]
    lane = lax.broadcasted_iota(I32, (tt, cap), 1)
    acc = jnp.zeros(x_ref.shape[1:], F32)
    for e in range(N_EXPERTS):
        onehot = (lane == pos[:, e:e + 1]).astype(BF16)
        acc = acc + aff[:, e:e + 1] * _dot(onehot, y_ref[e])
    _finish_combine(x_ref, g2_ref, fg_ref, o_ref, acc, final_norm)


def _combine(x1, g2, pos_t, aff, y, final_g, cap, final_norm, first_block):
    b, t, d = x1.shape
    tt = min(TOKEN_TILE, t)
    row = lambda i, j: (i, j, 0)
    return pl.pallas_call(
        functools.partial(_combine_kernel, cap=cap, final_norm=final_norm),
        grid=(b, t // tt),
        in_specs=[pl.BlockSpec((1, tt, d), row),
                  pl.BlockSpec((1, 1, d), lambda i, j: (i, 0, 0)),
                  pl.BlockSpec((1, N_EXPERTS, tt), lambda i, j: (i, 0, j)),
                  pl.BlockSpec((1, tt, LANES), row),
                  pl.BlockSpec((N_EXPERTS, cap, d), lambda i, j: (0, first_block + i, 0)),
                  pl.BlockSpec((1, d), lambda i, j: (0, 0))],
        out_specs=pl.BlockSpec((1, tt, d), row),
        out_shape=jax.ShapeDtypeStruct((b, t, d), F32),
        compiler_params=_params("arbitrary", "arbitrary"),
        name="ec_combine",
    )(x1, g2, pos_t, aff, y, final_g)


def _token_major(pos_t_ref):
    pos_t = pos_t_ref[0].astype(F32)
    pad = jnp.full((LANES - N_EXPERTS, pos_t.shape[1]), -1.0, F32)
    return jnp.concatenate([pos_t, pad], axis=0).T.astype(I32)


def _finish_combine(x_ref, g2_ref, fg_ref, o_ref, acc, final_norm):
    x2 = x_ref[0] + g2_ref[0] * acc
    if final_norm:
        x2 = (x2 * lax.rsqrt(jnp.mean(x2 * x2, axis=-1, keepdims=True) + NORM_EPS)) * fg_ref[...]
    o_ref[0] = x2


def _combine_win_kernel(starts_ref, x_ref, g2_ref, pos_t_ref, aff_ref, y_ref, fg_ref, o_ref,
                        *, cap, ntp, final_norm):
    w = SLOT_WINDOW
    assert w & (w - 1) == 0, "window column index is taken with a bit mask"
    tile = x_ref.shape[1]
    wins, fits = _tile_windows(starts_ref, cap, ntp, pl.program_id(0))
    pos = _token_major(pos_t_ref)
    aff = aff_ref[0]

    @pl.when(fits)
    def _():
        ywin = jnp.concatenate([y_ref[e, pl.ds(wins[e], w), :] for e in range(N_EXPERTS)], axis=0)
        lane = lax.broadcasted_iota(I32, (1, LANES), 1)
        win_row = jnp.zeros((1, LANES), I32)
        for e in range(N_EXPERTS):
            win_row = jnp.where(lane == e, wins[e], win_row)
        rel = jnp.where(pos >= 0, jnp.clip(pos - win_row, -1, w), -1)
        src = lax.broadcasted_iota(I32, (LANES, N_EXPERTS * w), 0) * w
        col = lax.broadcasted_iota(I32, (LANES, N_EXPERTS * w), 1)
        spread = ((col >= src) & (col < src + w)).astype(BF16)
        wide = _dot(jnp.concatenate([rel.astype(F32).astype(BF16), aff.astype(BF16)], axis=0), spread)
        in_win = lax.broadcasted_iota(I32, (1, N_EXPERTS * w), 1) & (w - 1)
        gates = jnp.where(wide[:tile] == in_win.astype(F32), wide[tile:], 0.0).astype(BF16)
        _finish_combine(x_ref, g2_ref, fg_ref, o_ref, _dot(gates, ywin), final_norm)

    @pl.when(jnp.logical_not(fits))
    def _():
        lane = lax.broadcasted_iota(I32, (tile, cap), 1)
        acc = jnp.zeros(x_ref.shape[1:], F32)
        for e in range(N_EXPERTS):
            onehot = (lane == pos[:, e:e + 1]).astype(BF16)
            acc = acc + aff[:, e:e + 1] * _dot(onehot, y_ref[e])
        _finish_combine(x_ref, g2_ref, fg_ref, o_ref, acc, final_norm)


def _combine_windowed(x1, g2, pos_t, starts, aff, y, final_g, cap, final_norm):
    b, t, d = x1.shape
    tile = min(TOKEN_TILE, t)
    nt = t // tile
    row = lambda i, j, s: (i, j, 0)
    grid_spec = pltpu.PrefetchScalarGridSpec(
        num_scalar_prefetch=1,
        grid=(b, nt),
        in_specs=[pl.BlockSpec((1, tile, d), row),
                  pl.BlockSpec((1, 1, d), lambda i, j, s: (i, 0, 0)),
                  pl.BlockSpec((1, N_EXPERTS, tile), lambda i, j, s: (i, 0, j)),
                  pl.BlockSpec((1, tile, LANES), row),
                  pl.BlockSpec((N_EXPERTS, cap, d), lambda i, j, s: (0, i, 0)),
                  pl.BlockSpec((1, d), lambda i, j, s: (0, 0))],
        out_specs=pl.BlockSpec((1, tile, d), row),
    )
    return pl.pallas_call(
        functools.partial(_combine_win_kernel, cap=cap, ntp=nt + 1, final_norm=final_norm),
        grid_spec=grid_spec,
        out_shape=jax.ShapeDtypeStruct((b, t, d), F32),
        compiler_params=_params("arbitrary", "arbitrary"),
        name="ec_combine_win",
    )(starts[:, :, :nt + 1].reshape(-1), x1, g2, pos_t, aff, y, final_g)


def _ec_moe(streams, w_gate, w_up, w_down, layer, final_g, final_norm):
    b, _, d = streams[0][0].shape
    caps = [CAPACITY_FACTOR * s[0].shape[1] // N_EXPERTS for s in streams]
    rows = sum(b * cap for cap in caps)
    routes, first_blocks, xs = [], [], None
    row0 = 0
    for k, ((x1, h2, aff, aff_t, g2), cap) in enumerate(zip(streams, caps)):
        t = x1.shape[1]
        assert row0 % cap == 0 and cap % SLOT_ALIGN == 0, "a stream's row blocks must tile the slot buffer"
        pos_t, starts = _route(aff_t, cap)
        windowed = k == 0 and t > TOKEN_TILE and cap >= SLOT_WINDOW and (cap - SLOT_WINDOW) % SLOT_ALIGN == 0
        if windowed:
            xs = _gather_windowed(pos_t, starts, h2, cap, rows)
        else:
            if xs is None and rows > b * cap:
                xs = jnp.zeros((N_EXPERTS, rows, d), BF16)
            xs = _gather_stacked(pos_t, h2, cap, rows, into=xs, first_block=row0 // cap)
        routes.append((pos_t, starts, aff, windowed))
        first_blocks.append(row0 // cap)
        row0 += b * cap
    y = _expert_ffn(xs, w_gate, w_up, w_down, layer)
    outs = []
    for (x1, h2, _, _, g2), cap, (pos_t, starts, aff, windowed), first in zip(streams, caps, routes, first_blocks):
        if windowed:
            outs.append(_combine_windowed(x1, g2, pos_t, starts, aff, y, final_g, cap, final_norm))
        else:
            outs.append(_combine(x1, g2, pos_t, aff, y, final_g, cap, final_norm, first))
    return outs


def _pre_gla_kernel(x_ref, g_ref, sh_ref, sc_ref, w_ref, wlr_ref, w2_ref, gb_ref,
                    q_ref, r_ref, k_ref, v_ref, lgf_ref, lgb_ref):
    qk = q_ref.shape[-1]
    vd = v_ref.shape[-1]
    tm = x_ref.shape[1]
    n_sub = max(1, tm // GLA_IN_SUBTILE_ROWS)
    subs = [slice(i * tm // n_sub, (i + 1) * tm // n_sub) for i in range(n_sub)]
    hs = [_norm_mod(x_ref[0, rs, :], g_ref[...], sh_ref[0], sc_ref[0]).astype(BF16) for rs in subs]
    ps = [_dot(h, w_ref[...]) for h in hs]
    lrs = [_dot(h, wlr_ref[...]).astype(BF16) for h in hs]
    zs = [_dot(lr, w2_ref[...]) + gb_ref[...] for lr in lrs]
    for rs, p, z in zip(subs, ps, zs):
        q_ref[0, rs, :] = (p[:, :qk] * (qk // GLA_HEADS) ** -0.5).astype(BF16)
        r_ref[0, rs, :] = _silu(p[:, qk:qk + vd]).astype(BF16)
        k_ref[0, rs, :] = p[:, qk + vd:2 * qk + vd].astype(BF16)
        v_ref[0, rs, :] = p[:, 2 * qk + vd:].astype(BF16)
        lg = (jnp.minimum(z, 0.0) - jnp.log1p(jnp.exp(-jnp.abs(z)))) * (1.0 / GLA_TAU)
        lgf_ref[0, rs, :] = lg[:, :qk]
        lgb_ref[0, rs, :] = lg[:, qk:]


def _pre_gla(x, g, shift, scale, w_main, w_lr, w2, gate_b):
    b, t, d = x.shape
    n_main = w_main.shape[1]
    qk = w2.shape[1] // 2
    vd = (n_main - 2 * qk) // 2
    tm = min(GLA_IN_ROW_TILE, t)
    row = lambda i, j: (i, j, 0)
    per_b = lambda i, j: (i, 0, 0)
    fixed = lambda i, j: (0, 0)
    out = lambda n, dt: (pl.BlockSpec((1, tm, n), row), jax.ShapeDtypeStruct((b, t, n), dt))
    outs = [out(qk, BF16), out(vd, BF16), out(qk, BF16), out(vd, BF16), out(qk, F32), out(qk, F32)]
    return pl.pallas_call(
        _pre_gla_kernel,
        grid=(b, t // tm),
        in_specs=[pl.BlockSpec((1, tm, d), row),
                  pl.BlockSpec((1, d), fixed),
                  pl.BlockSpec((1, 1, d), per_b),
                  pl.BlockSpec((1, 1, d), per_b),
                  pl.BlockSpec((d, n_main), fixed),
                  pl.BlockSpec((d, LANES), fixed),
                  pl.BlockSpec((LANES, 2 * qk), fixed),
                  pl.BlockSpec((1, 2 * qk), fixed)],
        out_specs=[o[0] for o in outs],
        out_shape=[o[1] for o in outs],
        compiler_params=_params("arbitrary", "arbitrary"),
        name="gla_in_proj",
    )(x, g, shift, scale, w_main, w_lr, w2, gate_b)


def _gla_masks(rows):
    ri = lax.broadcasted_iota(I32, (rows, rows), 0)
    ci = lax.broadcasted_iota(I32, (rows, rows), 1)
    same = (ri // GLA_CHUNK) == (ci // GLA_CHUNK)
    return same & (ri >= ci), same & (ri <= ci)


def _gla_tiles(jobs):
    ch = GLA_CHUNK
    n = len(jobs)
    rows, dk = jobs[0][1].shape
    n_chunks = rows // ch
    last = [[c * ch + (ch - 1 if job[6] else 0) for c in range(n_chunks)] for job in jobs]
    b = []
    for q, k, v, lg, st_ref, mask, forward in jobs:
        lg_hi, lg_lo = _split2(lg)
        m = mask.astype(BF16)
        b.append(_dot(m, lg_hi) + _dot(m, lg_lo))
    qt, kt, kd = [None] * n, [None] * n, [None] * n
    for i, (q, k, v, lg, st_ref, mask, forward) in enumerate(jobs):
        b_last = jnp.concatenate([jnp.broadcast_to(b[i][r:r + 1, :], (ch, dk)) for r in last[i]], axis=0)
        kf = k.astype(F32)
        kd[i] = (kf * jnp.exp(b_last - b[i])).astype(BF16)
        if q is not None:
            qt[i] = (q.astype(F32) * jnp.exp(b[i])).astype(BF16)
            kt[i] = (kf * jnp.exp(-b[i])).astype(BF16)
    a = [None if jobs[i][0] is None else lax.dot_general(qt[i], kt[i], NT_DIMS, preferred_element_type=F32)
         for i in range(n)]
    a = [None if a[i] is None else jnp.where(jobs[i][5], a[i], 0.0).astype(BF16) for i in range(n)]
    o_local = [None if a[i] is None else _dot(a[i], jobs[i][2]) for i in range(n)]
    d_st = [[lax.dot_general(jobs[i][2][c * ch:(c + 1) * ch], kd[i][c * ch:(c + 1) * ch], TN_DIMS,
                             preferred_element_type=F32) for c in range(n_chunks)] for i in range(n)]
    results = []
    for i, (q, k, v, lg, st_ref, mask, forward) in enumerate(jobs):
        st = st_ref[...]
        outs = [None] * n_chunks
        for c in (range(n_chunks) if forward else reversed(range(n_chunks))):
            rs = slice(c * ch, (c + 1) * ch)
            if q is not None:
                outs[c] = o_local[i][rs] + lax.dot_general(qt[i][rs], st.astype(BF16), NT_DIMS,
                                                           preferred_element_type=F32)
            st = jnp.exp(b[i][last[i][c]:last[i][c] + 1, :]) * st + d_st[i][c]
        st_ref[...] = st
        results.append(None if q is None else jnp.concatenate(outs, axis=0))
    return results


def _gla_kernel(q_ref, k_ref, v_ref, lgf_ref, lgb_ref, kc_ref, vc_ref, lgcf_ref, lgcb_ref, r_ref, ng_ref,
                o_ref, of_ref, ob_ref, stf_ref, stb_ref):
    t = q_ref.shape[1]
    t_ctx = kc_ref.shape[1]
    n_heads = stf_ref.shape[0]
    dv, dk = stf_ref.shape[1:]
    tile = min(GLA_TILE, t)
    ctile = min(GLA_TILE, t_ctx)
    mask_f, mask_b = _gla_masks(tile)
    cmask_f, cmask_b = (mask_f, mask_b) if ctile == tile else _gla_masks(ctile)
    stf_ref[...] = jnp.zeros_like(stf_ref)
    stb_ref[...] = jnp.zeros_like(stb_ref)
    kcols = [slice(h * dk, (h + 1) * dk) for h in range(n_heads)]
    vcols = [slice(h * dv, (h + 1) * dv) for h in range(n_heads)]

    n_ctx = t_ctx // ctile
    for i in range(n_ctx):
        rf = slice(i * ctile, (i + 1) * ctile)
        rb = slice((n_ctx - 1 - i) * ctile, (n_ctx - i) * ctile)
        jobs = []
        for h in range(n_heads):
            jobs.append((None, kc_ref[0, rf, kcols[h]], vc_ref[0, rf, vcols[h]], lgcf_ref[0, rf, kcols[h]],
                         stf_ref.at[h], cmask_f, True))
            jobs.append((None, kc_ref[0, rb, kcols[h]], vc_ref[0, rb, vcols[h]], lgcb_ref[0, rb, kcols[h]],
                         stb_ref.at[h], cmask_b, False))
        _gla_tiles(jobs)

    n = t // tile

    def body(i, carry):
        rf = pl.ds(pl.multiple_of(i * tile, tile), tile)
        rb = pl.ds(pl.multiple_of((n - 1 - i) * tile, tile), tile)
        jobs = []
        for h in range(n_heads):
            jobs.append((q_ref[0, rf, kcols[h]], k_ref[0, rf, kcols[h]], v_ref[0, rf, vcols[h]],
                         lgf_ref[0, rf, kcols[h]], stf_ref.at[h], mask_f, True))
            jobs.append((q_ref[0, rb, kcols[h]], k_ref[0, rb, kcols[h]], v_ref[0, rb, vcols[h]],
                         lgb_ref[0, rb, kcols[h]], stb_ref.at[h], mask_b, False))
        outs = _gla_tiles(jobs)
        for h in range(n_heads):
            of_ref[rf, vcols[h]] = outs[2 * h]
            ob_ref[rb, vcols[h]] = outs[2 * h + 1]
        return carry

    lax.fori_loop(0, n, body, 0)

    def readout(i, carry):
        rows = pl.ds(pl.multiple_of(i * tile, tile), tile)
        for h in range(n_heads):
            o = of_ref[rows, vcols[h]] + ob_ref[rows, vcols[h]]
            o = o * lax.rsqrt(jnp.mean(o * o, axis=-1, keepdims=True) + NORM_EPS) * ng_ref[:, vcols[h]]
            o_ref[0, rows, vcols[h]] = (o * r_ref[0, rows, vcols[h]].astype(F32)).astype(o_ref.dtype)
        return carry

    lax.fori_loop(0, n, readout, 0)


def _gla(q, k, v, lgf, lgb, kc, vc, lgcf, lgcb, r, norm_g):
    b, t, qk = q.shape
    vd = v.shape[-1]
    t_ctx = kc.shape[1]
    dk, dv = qk // GLA_HEADS, vd // GLA_HEADS
    hp = GLA_HEADS_PER_STEP
    head = lambda i, h: (i, 0, h)
    lat = lambda n: pl.BlockSpec((1, t, hp * n), head)
    ctx = lambda n: pl.BlockSpec((1, t_ctx, hp * n), head)
    return pl.pallas_call(
        _gla_kernel,
        grid=(b, GLA_HEADS // hp),
        in_specs=[lat(dk), lat(dk), lat(dv), lat(dk), lat(dk),
                  ctx(dk), ctx(dv), ctx(dk), ctx(dk),
                  lat(dv),
                  pl.BlockSpec((1, hp * dv), lambda i, h: (0, h))],
        out_specs=lat(dv),
        out_shape=jax.ShapeDtypeStruct((b, t, vd), BF16),
        scratch_shapes=[pltpu.VMEM((t, hp * dv), F32), pltpu.VMEM((t, hp * dv), F32),
                        pltpu.VMEM((hp, dv, dk), F32), pltpu.VMEM((hp, dv, dk), F32)],
        compiler_params=_params("arbitrary", "arbitrary"),
        name="gla_scan",
    )(q, k, v, lgf, lgb, kc, vc, lgcf, lgcb, r, norm_g)


def _rope_tables(t):
    pos = jnp.arange(t)
    row = (pos // GRID_W).astype(F32)
    col = (pos % GRID_W).astype(F32)
    inv = ROPE_BASE ** (-jnp.arange(ROPE_PAIRS, dtype=F32) / ROPE_PAIRS)
    ar, ac = row[:, None] * inv, col[:, None] * inv
    ang = jnp.concatenate([ar, ar, ac, ac], axis=-1)
    sign = jnp.tile(jnp.repeat(jnp.array([-1.0, 1.0], F32), ROPE_PAIRS), 2)
    reps = LANES // HEAD_DIM
    return jnp.tile(jnp.cos(ang), (1, reps)), jnp.tile(jnp.sin(ang) * sign, (1, reps))


def _router_split(w):
    d, n = w.shape
    w = jnp.pad(w, ((0, 0), (0, LANES - n)))
    hi = w.astype(BF16)
    return hi, (w - hi.astype(F32)).astype(BF16)


def kernel(x, c, ctx, c_ctx, ada_w, ada_b, norm1_g, norm2_g, attn_w_in, attn_w_out, attn_sink, gla_w_in, gla_gate_w2, gla_gate_b, gla_norm_g, gla_w_out, router_w, exp_w_gate, exp_w_up, exp_w_down, final_norm_g):
    b, t, d = x.shape
    depth = ada_w.shape[0]
    assert depth == 2, "layer 0 is windowed attention with context output, layer 1 is GLA and last"
    rows = -(-(b + 1) // 16) * 16
    src = jnp.concatenate([c, c_ctx[None, :], jnp.zeros((rows - b - 1, d), F32)], axis=0)
    cos, sin_signed = _rope_tables(t)
    final_g = final_norm_g.reshape(1, d)

    def chunks(i):
        mod = _modulation(src, ada_w, ada_b, i)
        lat = [mod[:b, j * d:(j + 1) * d].reshape(b, 1, d) for j in range(6)]
        cx = [jnp.broadcast_to(mod[b, j * d:(j + 1) * d].reshape(1, 1, d), (b, 1, d)) for j in range(6)]
        return lat, cx

    (sh1, sc1, g1, sh2, sc2, g2), (sh1c, sc1c, g1c, sh2c, sc2c, g2c) = chunks(0)
    n1, n2 = norm1_g[0].reshape(1, d), norm2_g[0].reshape(1, d)
    w_in = attn_w_in[0].astype(BF16)
    w_out = attn_w_out[0].astype(BF16)
    sink = attn_sink[0].reshape(1, -1)
    wr_hi, wr_lo = _router_split(router_w[0])
    q, k, v = _pre_attn(x, n1, sh1, sc1, w_in, cos, sin_signed, rope=True)
    t_ctx = ctx.shape[1]
    qc, kc, vc = _pre_attn(ctx, n1, sh1c, sc1c, w_in, cos[:t_ctx], sin_signed[:t_ctx], rope=False)
    o = _attn_band(q, k, v, kc, vc, sink)
    oc = _attn_ctx(qc, kc, vc, sink)
    lat = _post_mixer(o, w_out, x, g1, n2, sh2, sc2, wr_hi, wr_lo)
    cxt = _post_mixer(oc, w_out, ctx, g1c, n2, sh2c, sc2c, wr_hi, wr_lo)
    x, xc = _ec_moe([(*lat, g2), (*cxt, g2c)], exp_w_gate, exp_w_up, exp_w_down, 0, final_g, False)

    (sh1, sc1, g1, sh2, sc2, g2), (sh1c, sc1c, _, _, _, _) = chunks(1)
    n1, n2 = norm1_g[1].reshape(1, d), norm2_g[1].reshape(1, d)
    qk = gla_gate_w2.shape[-1]
    n_main = gla_w_in.shape[-1] - 2 * GLA_RANK
    w_main = gla_w_in[0][:, :n_main].astype(BF16)
    w_lr = jnp.pad(gla_w_in[0][:, n_main:], ((0, 0), (0, LANES - 2 * GLA_RANK))).astype(BF16)
    w2 = jnp.zeros((LANES, 2 * qk), F32)
    w2 = w2.at[:GLA_RANK, :qk].set(gla_gate_w2[0, 0]).at[GLA_RANK:2 * GLA_RANK, qk:].set(gla_gate_w2[0, 1])
    w2 = w2.astype(BF16)
    gate_b = gla_gate_b[0].reshape(1, 2 * qk)
    gq, gr, gk, gv, lgf, lgb = _pre_gla(x, n1, sh1, sc1, w_main, w_lr, w2, gate_b)
    _, _, gkc, gvc, lgcf, lgcb = _pre_gla(xc, n1, sh1c, sc1c, w_main, w_lr, w2, gate_b)
    og = _gla(gq, gk, gv, lgf, lgb, gkc, gvc, lgcf, lgcb, gr, gla_norm_g[0].reshape(1, -1))
    wr_hi, wr_lo = _router_split(router_w[1])
    lat = _post_mixer(og, gla_w_out[0].astype(BF16), x, g1, n2, sh2, sc2, wr_hi, wr_lo)
    return _ec_moe([(*lat, g2)], exp_w_gate, exp_w_up, exp_w_down, 1, final_g, True)[0]
```

```python
import functools

import jax
import jax.numpy as jnp
from jax import lax
from jax.experimental import pallas as pl
from jax.experimental.pallas import tpu as pltpu

F32 = jnp.float32
BF16 = jnp.bfloat16
I32 = jnp.int32

LANES = 128
HEAD_DIM = 64
ATT_GROUP = 4
ATT_BLOCK = 128
ATT_BLOCKS_PER_STEP = 4
GRID_W = 64
ROPE_BASE = 10000.0
ROPE_PAIRS = HEAD_DIM // 4
GLA_HEADS = 4
GLA_RANK = 16
GLA_TAU = 16.0
GLA_CHUNK = 64
N_EXPERTS = 16
CAPACITY_FACTOR = 2
NORM_EPS = 1e-6
LOG2E = 1.4426950408889634
ROW_TILE = 1024
ATTN_IN_ROW_TILE = 1024
ATTN_IN_SUBTILE_ROWS = 256
GLA_IN_ROW_TILE = 512
GLA_IN_SUBTILE_ROWS = 256
POST_SUBTILE_ROWS = 256
TOKEN_TILE = 256
MOE_ROW_TILE = 4608
MOE_DOWN_ROW_TILE = 1024
FF_TILE = 256
FEW_ROWS = 512
FF_TILE_FEW_ROWS = 1408
GLA_TILE = 256
GLA_HEADS_PER_STEP = 2
SLOT_WINDOW = 64
SLOT_ALIGN = 16
VMEM_LIMIT = 56 * 1024 * 1024

NT_DIMS = (((1,), (1,)), ((), ()))
TN_DIMS = (((0,), (0,)), ((), ()))


def _params(*sem):
    return pltpu.CompilerParams(dimension_semantics=sem, vmem_limit_bytes=VMEM_LIMIT)


def _dot(a, b):
    return jnp.dot(a, b, preferred_element_type=F32)


def _split2(a):
    hi = a.astype(BF16)
    lo = (a - hi.astype(F32)).astype(BF16)
    return hi, lo


def _dot3(a, b_hi, b_lo):
    a_hi, a_lo = _split2(a)
    return _dot(a_hi, b_hi) + _dot(a_hi, b_lo) + _dot(a_lo, b_hi)


def _silu(a):
    return a * jax.nn.sigmoid(a)


def _norm_mod(x, g, shift, scale):
    y = x * lax.rsqrt(jnp.mean(x * x, axis=-1, keepdims=True) + NORM_EPS)
    return (y * g) * (1.0 + scale) + shift


def _mod_kernel(src_ref, w_ref, b_ref, o_ref):
    w_hi, w_lo = _split2(w_ref[0])
    o_ref[...] = _dot3(_silu(src_ref[...]), w_hi, w_lo) + b_ref[0]


def _modulation(src, ada_w, ada_b, layer):
    rows, d = src.shape
    depth, _, n = ada_w.shape
    tn = 512
    return pl.pallas_call(
        _mod_kernel,
        grid=(n // tn,),
        in_specs=[pl.BlockSpec((rows, d), lambda j: (0, 0)),
                  pl.BlockSpec((1, d, tn), lambda j: (layer, 0, j)),
                  pl.BlockSpec((1, 1, tn), lambda j: (layer, 0, j))],
        out_specs=pl.BlockSpec((rows, tn), lambda j: (0, j)),
        out_shape=jax.ShapeDtypeStruct((rows, n), F32),
        compiler_params=_params("arbitrary"),
        name="adaln_mod",
    )(src, ada_w, ada_b.reshape(depth, 1, n))


def _rope(x, cos, sin_signed, lane_lo):
    outs = []
    for j in range(x.shape[1] // LANES):
        xb = x[:, j * LANES:(j + 1) * LANES]
        partner = jnp.where(lane_lo, pltpu.roll(xb, LANES - ROPE_PAIRS, 1), pltpu.roll(xb, ROPE_PAIRS, 1))
        outs.append(xb * cos + partner * sin_signed)
    return jnp.concatenate(outs, axis=1)


def _pre_attn_kernel(x_ref, g_ref, sh_ref, sc_ref, w_ref, cos_ref, sin_ref, q_ref, k_ref, v_ref, *, rope):
    d = x_ref.shape[-1]
    kv = k_ref.shape[-1]
    tm = x_ref.shape[1]
    n_sub = max(1, tm // ATTN_IN_SUBTILE_ROWS)
    subs = [slice(i * tm // n_sub, (i + 1) * tm // n_sub) for i in range(n_sub)]
    hs = [_norm_mod(x_ref[0, rs, :], g_ref[...], sh_ref[0], sc_ref[0]).astype(BF16) for rs in subs]
    ps = [_dot(h, w_ref[...]) for h in hs]
    for rs, p in zip(subs, ps):
        q, k, v = p[:, :d], p[:, d:d + kv], p[:, d + kv:]
        if rope:
            lane = lax.broadcasted_iota(I32, (tm // n_sub, LANES), 1)
            lane_lo = (lane & (2 * ROPE_PAIRS - 1)) < ROPE_PAIRS
            cos, sin_signed = cos_ref[rs, :], sin_ref[rs, :]
            q = _rope(q, cos, sin_signed, lane_lo)
            k = _rope(k, cos, sin_signed, lane_lo)
        q_ref[0, rs, :] = (q * (HEAD_DIM ** -0.5 * LOG2E)).astype(BF16)
        k_ref[0, rs, :] = k.astype(BF16)
        v_ref[0, rs, :] = v.astype(BF16)


def _pre_attn(x, g, shift, scale, w_in, cos, sin_signed, rope):
    b, t, d = x.shape
    n_in = w_in.shape[1]
    kv = (n_in - d) // 2
    tm = min(ATTN_IN_ROW_TILE, t)
    row = lambda i, j: (i, j, 0)
    per_b = lambda i, j: (i, 0, 0)
    fixed = lambda i, j: (0, 0)
    return pl.pallas_call(
        functools.partial(_pre_attn_kernel, rope=rope),
        grid=(b, t // tm),
        in_specs=[pl.BlockSpec((1, tm, d), row),
                  pl.BlockSpec((1, d), fixed),
                  pl.BlockSpec((1, 1, d), per_b),
                  pl.BlockSpec((1, 1, d), per_b),
                  pl.BlockSpec((d, n_in), fixed),
                  pl.BlockSpec((tm, LANES), lambda i, j: (j, 0)),
                  pl.BlockSpec((tm, LANES), lambda i, j: (j, 0))],
        out_specs=[pl.BlockSpec((1, tm, d), row),
                   pl.BlockSpec((1, tm, kv), row),
                   pl.BlockSpec((1, tm, kv), row)],
        out_shape=[jax.ShapeDtypeStruct((b, t, d), BF16),
                   jax.ShapeDtypeStruct((b, t, kv), BF16),
                   jax.ShapeDtypeStruct((b, t, kv), BF16)],
        compiler_params=_params("arbitrary", "arbitrary"),
        name="attn_in_proj",
    )(x, g, shift, scale, w_in, cos, sin_signed)


def _attn_heads(groups, sink_ref, o_ref):
    tq = groups[0][0].shape[0]
    n_kv = groups[0][1][0].shape[1] // HEAD_DIM
    lane = lax.broadcasted_iota(I32, (1, ATT_GROUP * tq), 1)
    v_lane = lax.broadcasted_iota(I32, (1, LANES), 1)
    jobs = [(group, h) for h in range(n_kv) for group in groups]
    n_jobs = len(jobs)
    sks, scores, ms, es, both = ([None] * n_jobs for _ in range(5))

    def score_stage(job):
        (q, k_parts, v_parts, bias_parts, rows), h = jobs[job]
        heads = [ATT_GROUP * h + g for g in range(ATT_GROUP)]
        qg = jnp.concatenate([q[:, i * HEAD_DIM:(i + 1) * HEAD_DIM] for i in heads], axis=0)
        sk = jnp.full((1, ATT_GROUP * tq), sink_ref[0, heads[0]], F32)
        for g in range(1, ATT_GROUP):
            sk = jnp.where(lane >= g * tq, sink_ref[0, heads[g]], sk)
        sks[job] = sk * LOG2E
        m = sks[job]
        scores[job] = []
        for kp, bias in zip(k_parts, bias_parts):
            s = lax.dot_general(kp[:, h * HEAD_DIM:(h + 1) * HEAD_DIM], qg, NT_DIMS, preferred_element_type=F32)
            s = s if bias is None else s + bias
            m = jnp.maximum(m, jnp.max(s, axis=0, keepdims=True))
            scores[job].append(s)
        ms[job] = m

    def weight_stage(job):
        es[job] = jnp.concatenate([jnp.exp2(s - ms[job]).astype(BF16) for s in scores[job]], axis=0)

    def value_stage(job):
        (q, k_parts, v_parts, bias_parts, rows), h = jobs[job]
        block = (h * HEAD_DIM // LANES) * LANES
        upper = (h * HEAD_DIM) % LANES != 0
        v_pair = jnp.concatenate([vp[:, block:block + LANES] for vp in v_parts], axis=0)
        own = (v_lane >= HEAD_DIM) if upper else (v_lane < HEAD_DIM)
        v_aug = jnp.where(own, v_pair, jnp.ones_like(v_pair))
        both[job] = lax.dot_general(v_aug, es[job], TN_DIMS, preferred_element_type=F32)

    stages = (score_stage, weight_stage, value_stage)
    for step in range(n_jobs + len(stages) - 1):
        for depth, stage in enumerate(stages):
            if 0 <= step - depth < n_jobs:
                stage(step - depth)
    for gi, group in enumerate(groups):
        outs = []
        for h in range(n_kv):
            job = h * len(groups) + gi
            upper = (h * HEAD_DIM) % LANES != 0
            num = both[job][HEAD_DIM:] if upper else both[job][:HEAD_DIM]
            total = both[job][:1] if upper else both[job][HEAD_DIM:HEAD_DIM + 1]
            o_t = num / (total + jnp.exp2(sks[job] - ms[job]))
            outs.extend(o_t[:, g * tq:(g + 1) * tq] for g in range(ATT_GROUP))
        o_ref[0, group[4], :] = jnp.concatenate(outs, axis=0).T.astype(o_ref.dtype)


def _band_bias():
    blk = ATT_BLOCK
    key = jnp.arange(blk)[:, None]
    qry = jnp.arange(ATT_GROUP * blk)[None, :] % blk
    prev_ok = key >= qry
    next_ok = key <= qry
    never = jnp.zeros_like(prev_ok)
    variants = [jnp.stack([never if first else prev_ok, never if last else next_ok])
                for last in (False, True) for first in (False, True)]
    return jnp.where(jnp.stack(variants), 0.0, -jnp.inf).astype(F32)


def _attn_band_kernel(*refs, qb):
    q_ref, o_ref, sink_ref = refs[0], refs[-1], refs[-2]
    k_refs = refs[1:qb + 3]
    v_refs = refs[qb + 3:2 * qb + 5]
    kc_ref, vc_ref = refs[2 * qb + 5], refs[2 * qb + 6]
    bias_refs = refs[2 * qb + 7:3 * qb + 7]
    blk = ATT_BLOCK
    groups = []
    for g in range(qb):
        rows = slice(g * blk, (g + 1) * blk)
        groups.append((q_ref[0, rows, :],
                       [kc_ref[0]] + [r[0] for r in k_refs[g:g + 3]],
                       [vc_ref[0]] + [r[0] for r in v_refs[g:g + 3]],
                       [None, bias_refs[g][0, 0], None, bias_refs[g][0, 1]], rows))
    _attn_heads(groups, sink_ref, o_ref)


def _attn_ctx_kernel(q_ref, kc_ref, vc_ref, sink_ref, o_ref):
    _attn_heads([(q_ref[0], [kc_ref[0]], [vc_ref[0]], [None], slice(None))], sink_ref, o_ref)


def _attn_band(q, k, v, kc, vc, sink):
    b, t, d = q.shape
    kv = k.shape[-1]
    n_ctx = kc.shape[1]
    nb = t // ATT_BLOCK
    qb = ATT_BLOCKS_PER_STEP if nb % ATT_BLOCKS_PER_STEP == 0 else 1
    steps = nb // qb
    kv_spec = lambda m: pl.BlockSpec((1, ATT_BLOCK, kv), lambda i, j: (i, jnp.clip(qb * j - 1 + m, 0, nb - 1), 0))
    kv_specs = [kv_spec(m) for m in range(qb + 2)]
    ctx_spec = pl.BlockSpec((1, n_ctx, kv), lambda i, j: (i, 0, 0))
    bias = _band_bias()

    def bias_spec(g):
        first = (lambda j: (j == 0).astype(I32)) if g == 0 else (lambda j: 0)
        last = (lambda j: 2 * (j == steps - 1).astype(I32)) if g == qb - 1 else (lambda j: 0)
        return pl.BlockSpec((1,) + bias.shape[1:], lambda i, j: (first(j) + last(j), 0, 0, 0))

    step_rows = pl.BlockSpec((1, qb * ATT_BLOCK, d), lambda i, j: (i, j, 0))
    return pl.pallas_call(
        functools.partial(_attn_band_kernel, qb=qb),
        grid=(b, steps),
        in_specs=[step_rows] + kv_specs + kv_specs + [ctx_spec, ctx_spec] + [bias_spec(g) for g in range(qb)]
                 + [pl.BlockSpec(memory_space=pltpu.SMEM)],
        out_specs=step_rows,
        out_shape=jax.ShapeDtypeStruct((b, t, d), BF16),
        compiler_params=_params("arbitrary", "arbitrary"),
        name="attn_band",
    )(q, *([k] * (qb + 2)), *([v] * (qb + 2)), kc, vc, *([bias] * qb), sink)


def _attn_ctx(qc, kc, vc, sink):
    b, n_ctx, d = qc.shape
    kv = kc.shape[-1]
    return pl.pallas_call(
        _attn_ctx_kernel,
        grid=(b,),
        in_specs=[pl.BlockSpec((1, n_ctx, d), lambda i: (i, 0, 0)),
                  pl.BlockSpec((1, n_ctx, kv), lambda i: (i, 0, 0)),
                  pl.BlockSpec((1, n_ctx, kv), lambda i: (i, 0, 0)),
                  pl.BlockSpec(memory_space=pltpu.SMEM)],
        out_specs=pl.BlockSpec((1, n_ctx, d), lambda i: (i, 0, 0)),
        out_shape=jax.ShapeDtypeStruct((b, n_ctx, d), BF16),
        compiler_params=_params("arbitrary"),
        name="attn_ctx",
    )(qc, kc, vc, sink)


def _post_kernel(o_ref, w_ref, x_ref, g1_ref, n2_ref, sh_ref, sc_ref, wr_hi_ref, wr_lo_ref,
                 x1_ref, h2_ref, aff_ref, aff_t_ref):
    tm = x_ref.shape[1]
    n_sub = max(2, tm // POST_SUBTILE_ROWS)
    subs = [slice(i * tm // n_sub, (i + 1) * tm // n_sub) for i in range(n_sub)]
    ys = [_dot(o_ref[0, rs, :], w_ref[...]) for rs in subs]
    h2s = []
    for rs, y in zip(subs, ys):
        x1 = x_ref[0, rs, :] + g1_ref[0] * y
        x1_ref[0, rs, :] = x1
        h2 = _norm_mod(x1, n2_ref[...], sh_ref[0], sc_ref[0])
        h2_ref[0, rs, :] = h2.astype(BF16)
        h2s.append(h2)
    lane = lax.broadcasted_iota(I32, (tm // n_sub, LANES), 1)
    for rs, h2 in zip(subs, h2s):
        lg = jnp.where(lane < N_EXPERTS, _dot3(h2, wr_hi_ref[...], wr_lo_ref[...]), -jnp.inf)
        e = jnp.exp(lg - jnp.max(lg, axis=-1, keepdims=True))
        aff = e / jnp.sum(e, axis=-1, keepdims=True)
        aff_ref[0, rs, :] = aff
        aff_t_ref[0, :, rs] = aff.T[:N_EXPERTS]


def _post_mixer(o, w_out, x, g1, n2, shift, scale, wr_hi, wr_lo):
    b, t, d = x.shape
    k_in = o.shape[-1]
    tm = min(ROW_TILE, t)
    row = lambda i, j: (i, j, 0)
    per_b = lambda i, j: (i, 0, 0)
    fixed = lambda i, j: (0, 0)
    return pl.pallas_call(
        _post_kernel,
        grid=(b, t // tm),
        in_specs=[pl.BlockSpec((1, tm, k_in), row),
                  pl.BlockSpec((k_in, d), fixed),
                  pl.BlockSpec((1, tm, d), row),
                  pl.BlockSpec((1, 1, d), per_b),
                  pl.BlockSpec((1, d), fixed),
                  pl.BlockSpec((1, 1, d), per_b),
                  pl.BlockSpec((1, 1, d), per_b),
                  pl.BlockSpec((d, LANES), fixed),
                  pl.BlockSpec((d, LANES), fixed)],
        out_specs=[pl.BlockSpec((1, tm, d), row),
                   pl.BlockSpec((1, tm, d), row),
                   pl.BlockSpec((1, tm, LANES), row),
                   pl.BlockSpec((1, N_EXPERTS, tm), lambda i, j: (i, 0, j))],
        out_shape=[jax.ShapeDtypeStruct((b, t, d), F32),
                   jax.ShapeDtypeStruct((b, t, d), BF16),
                   jax.ShapeDtypeStruct((b, t, LANES), F32),
                   jax.ShapeDtypeStruct((b, N_EXPERTS, t), F32)],
        compiler_params=_params("arbitrary", "arbitrary"),
        name="mixer_out_proj",
    )(o, w_out, x, g1, n2, shift, scale, wr_hi, wr_lo)


def _exclusive_cumsum_lanes(mask):
    rows, t = mask.shape
    width = min(2 * LANES, t)
    r = lax.broadcasted_iota(I32, (width, width), 0)
    c = lax.broadcasted_iota(I32, (width, width), 1)
    upper = (r < c).astype(BF16)
    carry = jnp.zeros((rows, 1), F32)
    outs = []
    for ch in range(t // width):
        m = mask[:, ch * width:(ch + 1) * width]
        outs.append(_dot(m.astype(BF16), upper) + carry)
        carry = carry + jnp.sum(m, axis=1, keepdims=True)
    return jnp.concatenate(outs, axis=1)


LOG_FLOOR = -160.0
BISECT_LOG_STEPS = 40
BISECT_STEPS = 12


def _select_kernel(aff_t_ref, pos_t_ref, starts_ref, *, cap):
    rows = aff_t_ref.shape[0]

    def enough(threshold):
        return jnp.sum((aff_t_ref[...] >= threshold).astype(F32), axis=1, keepdims=True) >= cap

    def bisect_log(_, carry):
        lo_u, hi_u = carry
        mid = lo_u + (hi_u - lo_u) * 0.5
        ok = enough(jnp.exp2(mid))
        return jnp.where(ok, mid, lo_u), jnp.where(ok, hi_u, mid)

    def bisect(_, carry):
        lo, hi = carry
        mid = lo + (hi - lo) * 0.5
        ok = enough(mid)
        return jnp.where(ok, mid, lo), jnp.where(ok, hi, mid)

    lo_u, hi_u = lax.fori_loop(0, BISECT_LOG_STEPS, bisect_log,
                               (jnp.full((rows, 1), LOG_FLOOR, F32), jnp.full((rows, 1), 1.0, F32)))
    lo, hi = lax.fori_loop(0, BISECT_STEPS, bisect, (jnp.exp2(lo_u), jnp.exp2(hi_u)))
    aff_t = aff_t_ref[...]
    above = (aff_t >= hi).astype(F32)
    tied = ((aff_t >= lo) & (aff_t < hi)).astype(F32)
    need = cap - jnp.sum(above, axis=1, keepdims=True)
    sel = above + tied * (_exclusive_cumsum_lanes(tied) < need).astype(F32)
    slot = _exclusive_cumsum_lanes(sel)
    pos_t_ref[...] = jnp.where(sel > 0, slot, -1.0).astype(I32)
    t = aff_t.shape[1]
    tok = lax.broadcasted_iota(I32, (t, LANES), 0)
    edge = lax.broadcasted_iota(I32, (t, LANES), 1) * min(TOKEN_TILE, t)
    before = ((tok < edge) & (edge <= t)).astype(BF16)
    starts_ref[...] = _dot(sel.astype(BF16), before).astype(I32)


def _route(aff_t, cap):
    b, _, t = aff_t.shape
    rows = b * N_EXPERTS
    pos_t, starts = pl.pallas_call(
        functools.partial(_select_kernel, cap=cap),
        grid=(1,),
        in_specs=[pl.BlockSpec((rows, t), lambda i: (0, 0))],
        out_specs=[pl.BlockSpec((rows, t), lambda i: (0, 0)),
                   pl.BlockSpec((rows, LANES), lambda i: (0, 0))],
        out_shape=[jax.ShapeDtypeStruct((rows, t), I32),
                   jax.ShapeDtypeStruct((rows, LANES), I32)],
        compiler_params=_params("arbitrary"),
        name="ec_select",
    )(aff_t.reshape(rows, t))
    return pos_t.reshape(b, N_EXPERTS, t), starts.reshape(b, N_EXPERTS, LANES)


def _gather_stacked_kernel(pos_t_ref, h_ref, *rest, cap):
    xs_ref = rest[-1]
    t = h_ref.shape[1]
    pos_t = pos_t_ref[0]
    slot = lax.broadcasted_iota(I32, (cap, t), 0)
    onehot = jnp.concatenate([(slot == pos_t[e:e + 1, :]).astype(BF16) for e in range(N_EXPERTS)], axis=0)
    z = _dot(onehot, h_ref[0]).astype(BF16)
    for e in range(N_EXPERTS):
        xs_ref[e] = z[e * cap:(e + 1) * cap]


def _gather_stacked(pos_t, h, cap, rows, into=None, first_block=0):
    b, t, d = h.shape
    in_specs = [pl.BlockSpec((1, N_EXPERTS, t), lambda i: (i, 0, 0)),
                pl.BlockSpec((1, t, d), lambda i: (i, 0, 0))]
    args = [pos_t, h]
    if into is not None:
        assert into.shape == (N_EXPERTS, rows, d)
        in_specs.append(pl.BlockSpec(memory_space=pl.ANY))
        args.append(into)
    return pl.pallas_call(
        functools.partial(_gather_stacked_kernel, cap=cap),
        grid=(b,),
        in_specs=in_specs,
        out_specs=pl.BlockSpec((N_EXPERTS, cap, d), lambda i: (0, first_block + i, 0)),
        out_shape=jax.ShapeDtypeStruct((N_EXPERTS, rows, d), BF16),
        input_output_aliases={} if into is None else {2: 0},
        compiler_params=_params("arbitrary"),
        name="ec_gather_stacked",
    )(*args)


def _tile_windows(starts_ref, cap, ntp, sample):
    j = pl.program_id(1)
    wins, fits = [], None
    for e in range(N_EXPERTS):
        at = (sample * N_EXPERTS + e) * ntp + j
        win = jnp.minimum((starts_ref[at] // SLOT_ALIGN) * SLOT_ALIGN, cap - SLOT_WINDOW)
        ok = starts_ref[at + 1] - win <= SLOT_WINDOW
        wins.append(pl.multiple_of(win, SLOT_ALIGN))
        fits = ok if fits is None else jnp.logical_and(fits, ok)
    return wins, fits


def _gather_win_kernel(starts_ref, pos_t_ref, h_ref, xs_ref, *, cap, ntp, n_samples):
    w = SLOT_WINDOW
    tile = h_ref.shape[1]
    real = pl.program_id(0) < n_samples
    wins, fits = _tile_windows(starts_ref, cap, ntp, jnp.minimum(pl.program_id(0), n_samples - 1))
    h = h_ref[0]
    pos_t = pos_t_ref[0]

    @pl.when(pl.program_id(1) == 0)
    def _():
        xs_ref[...] = jnp.zeros_like(xs_ref)

    @pl.when(jnp.logical_and(real, fits))
    def _():
        row = lax.broadcasted_iota(I32, (w, tile), 0)
        onehot = jnp.concatenate([(row + wins[e] == pos_t[e:e + 1, :]).astype(BF16) for e in range(N_EXPERTS)],
                                 axis=0)
        z = _dot(onehot, h).astype(BF16)
        for e in range(N_EXPERTS):
            xs_ref[e, pl.ds(wins[e], w), :] += z[e * w:(e + 1) * w]

    @pl.when(jnp.logical_and(real, jnp.logical_not(fits)))
    def _():
        slot = lax.broadcasted_iota(I32, (cap, tile), 0)
        for e in range(N_EXPERTS):
            onehot = (slot == pos_t[e:e + 1, :]).astype(BF16)
            xs_ref[e] += _dot(onehot, h).astype(BF16)


def _gather_windowed(pos_t, starts, h, cap, rows):
    b, t, d = h.shape
    tile = min(TOKEN_TILE, t)
    nt = t // tile
    spare_blocks = -(-(rows - b * cap) // cap)
    sample = lambda i: jnp.minimum(i, b - 1)
    grid_spec = pltpu.PrefetchScalarGridSpec(
        num_scalar_prefetch=1,
        grid=(b + spare_blocks, nt),
        in_specs=[pl.BlockSpec((1, N_EXPERTS, tile), lambda i, j, s: (sample(i), 0, j)),
                  pl.BlockSpec((1, tile, d), lambda i, j, s: (sample(i), j, 0))],
        out_specs=pl.BlockSpec((N_EXPERTS, cap, d), lambda i, j, s: (0, i, 0)),
    )
    return pl.pallas_call(
        functools.partial(_gather_win_kernel, cap=cap, ntp=nt + 1, n_samples=b),
        grid_spec=grid_spec,
        out_shape=jax.ShapeDtypeStruct((N_EXPERTS, rows, d), BF16),
        compiler_params=_params("arbitrary", "arbitrary"),
        name="ec_gather_win",
    )(starts[:, :, :nt + 1].reshape(-1), pos_t, h)


def _ffn_up_kernel(xs_ref, wg_ref, wu_ref, hm_ref):
    xs = xs_ref[0]
    a = _dot(xs, wg_ref[0, 0].astype(BF16))
    u = _dot(xs, wu_ref[0, 0].astype(BF16))
    hm_ref[0] = (_silu(a) * u).astype(hm_ref.dtype)


def _ffn_down_kernel(hm_ref, wd_ref, y_ref, wd_bf16_ref):
    @pl.when(pl.program_id(1) == 0)
    def _():
        wd_bf16_ref[...] = wd_ref[0, 0].astype(BF16)

    y_ref[0] = _dot(hm_ref[0], wd_bf16_ref[...]).astype(y_ref.dtype)


def _largest_divisor(rows, limit):
    if rows <= limit:
        return rows
    tile = (limit // 256) * 256
    while tile and rows % tile:
        tile -= 256
    assert tile, f"no 256-multiple tile <= {limit} divides {rows} rows"
    return tile


def _expert_ffn(xs, w_gate, w_up, w_down, layer):
    n_exp, rows, d = xs.shape
    ff = w_gate.shape[-1]
    tm = _largest_divisor(rows, MOE_ROW_TILE)
    tf = min(FF_TILE, ff)
    if rows <= FEW_ROWS and ff % FF_TILE_FEW_ROWS == 0:
        tf = FF_TILE_FEW_ROWS
    hm = pl.pallas_call(
        _ffn_up_kernel,
        grid=(n_exp, rows // tm, ff // tf),
        in_specs=[pl.BlockSpec((1, tm, d), lambda e, m, f: (e, m, 0)),
                  pl.BlockSpec((1, 1, d, tf), lambda e, m, f: (layer, e, 0, f)),
                  pl.BlockSpec((1, 1, d, tf), lambda e, m, f: (layer, e, 0, f))],
        out_specs=pl.BlockSpec((1, tm, tf), lambda e, m, f: (e, m, f)),
        out_shape=jax.ShapeDtypeStruct((n_exp, rows, ff), BF16),
        compiler_params=_params("arbitrary", "arbitrary", "arbitrary"),
        name="ec_ffn_up",
    )(xs, w_gate, w_up)
    tm2 = _largest_divisor(rows, MOE_DOWN_ROW_TILE)
    return pl.pallas_call(
        _ffn_down_kernel,
        grid=(n_exp, rows // tm2),
        in_specs=[pl.BlockSpec((1, tm2, ff), lambda e, m: (e, m, 0)),
                  pl.BlockSpec((1, 1, ff, d), lambda e, m: (layer, e, 0, 0))],
        out_specs=pl.BlockSpec((1, tm2, d), lambda e, m: (e, m, 0)),
        out_shape=jax.ShapeDtypeStruct((n_exp, rows, d), BF16),
        scratch_shapes=[pltpu.VMEM((ff, d), BF16)],
        compiler_params=_params("arbitrary", "arbitrary"),
        name="ec_ffn_down",
    )(hm, w_down)


def _combine_kernel(x_ref, g2_ref, pos_t_ref, aff_ref, y_ref, fg_ref, o_ref, *, cap, final_norm):
    tt = x_ref.shape[1]
    pos = _token_major(pos_t_ref)
    aff = aff_ref[0]
    lane = lax.broadcasted_iota(I32, (tt, cap), 1)
    acc = jnp.zeros(x_ref.shape[1:], F32)
    for e in range(N_EXPERTS):
        onehot = (lane == pos[:, e:e + 1]).astype(BF16)
        acc = acc + aff[:, e:e + 1] * _dot(onehot, y_ref[e])
    _finish_combine(x_ref, g2_ref, fg_ref, o_ref, acc, final_norm)


def _combine(x1, g2, pos_t, aff, y, final_g, cap, final_norm, first_block):
    b, t, d = x1.shape
    tt = min(TOKEN_TILE, t)
    row = lambda i, j: (i, j, 0)
    return pl.pallas_call(
        functools.partial(_combine_kernel, cap=cap, final_norm=final_norm),
        grid=(b, t // tt),
        in_specs=[pl.BlockSpec((1, tt, d), row),
                  pl.BlockSpec((1, 1, d), lambda i, j: (i, 0, 0)),
                  pl.BlockSpec((1, N_EXPERTS, tt), lambda i, j: (i, 0, j)),
                  pl.BlockSpec((1, tt, LANES), row),
                  pl.BlockSpec((N_EXPERTS, cap, d), lambda i, j: (0, first_block + i, 0)),
                  pl.BlockSpec((1, d), lambda i, j: (0, 0))],
        out_specs=pl.BlockSpec((1, tt, d), row),
        out_shape=jax.ShapeDtypeStruct((b, t, d), F32),
        compiler_params=_params("arbitrary", "arbitrary"),
        name="ec_combine",
    )(x1, g2, pos_t, aff, y, final_g)


def _token_major(pos_t_ref):
    pos_t = pos_t_ref[0].astype(F32)
    pad = jnp.full((LANES - N_EXPERTS, pos_t.shape[1]), -1.0, F32)
    return jnp.concatenate([pos_t, pad], axis=0).T.astype(I32)


def _finish_combine(x_ref, g2_ref, fg_ref, o_ref, acc, final_norm):
    x2 = x_ref[0] + g2_ref[0] * acc
    if final_norm:
        x2 = (x2 * lax.rsqrt(jnp.mean(x2 * x2, axis=-1, keepdims=True) + NORM_EPS)) * fg_ref[...]
    o_ref[0] = x2


def _combine_win_kernel(starts_ref, x_ref, g2_ref, pos_t_ref, aff_ref, y_ref, fg_ref, o_ref,
                        *, cap, ntp, final_norm):
    w = SLOT_WINDOW
    assert w & (w - 1) == 0, "window column index is taken with a bit mask"
    tile = x_ref.shape[1]
    wins, fits = _tile_windows(starts_ref, cap, ntp, pl.program_id(0))
    pos = _token_major(pos_t_ref)
    aff = aff_ref[0]

    @pl.when(fits)
    def _():
        ywin = jnp.concatenate([y_ref[e, pl.ds(wins[e], w), :] for e in range(N_EXPERTS)], axis=0)
        lane = lax.broadcasted_iota(I32, (1, LANES), 1)
        win_row = jnp.zeros((1, LANES), I32)
        for e in range(N_EXPERTS):
            win_row = jnp.where(lane == e, wins[e], win_row)
        rel = jnp.where(pos >= 0, jnp.clip(pos - win_row, -1, w), -1)
        src = lax.broadcasted_iota(I32, (LANES, N_EXPERTS * w), 0) * w
        col = lax.broadcasted_iota(I32, (LANES, N_EXPERTS * w), 1)
        spread = ((col >= src) & (col < src + w)).astype(BF16)
        wide = _dot(jnp.concatenate([rel.astype(F32).astype(BF16), aff.astype(BF16)], axis=0), spread)
        in_win = lax.broadcasted_iota(I32, (1, N_EXPERTS * w), 1) & (w - 1)
        gates = jnp.where(wide[:tile] == in_win.astype(F32), wide[tile:], 0.0).astype(BF16)
        _finish_combine(x_ref, g2_ref, fg_ref, o_ref, _dot(gates, ywin), final_norm)

    @pl.when(jnp.logical_not(fits))
    def _():
        lane = lax.broadcasted_iota(I32, (tile, cap), 1)
        acc = jnp.zeros(x_ref.shape[1:], F32)
        for e in range(N_EXPERTS):
            onehot = (lane == pos[:, e:e + 1]).astype(BF16)
            acc = acc + aff[:, e:e + 1] * _dot(onehot, y_ref[e])
        _finish_combine(x_ref, g2_ref, fg_ref, o_ref, acc, final_norm)


def _combine_windowed(x1, g2, pos_t, starts, aff, y, final_g, cap, final_norm):
    b, t, d = x1.shape
    tile = min(TOKEN_TILE, t)
    nt = t // tile
    row = lambda i, j, s: (i, j, 0)
    grid_spec = pltpu.PrefetchScalarGridSpec(
        num_scalar_prefetch=1,
        grid=(b, nt),
        in_specs=[pl.BlockSpec((1, tile, d), row),
                  pl.BlockSpec((1, 1, d), lambda i, j, s: (i, 0, 0)),
                  pl.BlockSpec((1, N_EXPERTS, tile), lambda i, j, s: (i, 0, j)),
                  pl.BlockSpec((1, tile, LANES), row),
                  pl.BlockSpec((N_EXPERTS, cap, d), lambda i, j, s: (0, i, 0)),
                  pl.BlockSpec((1, d), lambda i, j, s: (0, 0))],
        out_specs=pl.BlockSpec((1, tile, d), row),
    )
    return pl.pallas_call(
        functools.partial(_combine_win_kernel, cap=cap, ntp=nt + 1, final_norm=final_norm),
        grid_spec=grid_spec,
        out_shape=jax.ShapeDtypeStruct((b, t, d), F32),
        compiler_params=_params("arbitrary", "arbitrary"),
        name="ec_combine_win",
    )(starts[:, :, :nt + 1].reshape(-1), x1, g2, pos_t, aff, y, final_g)


def _ec_moe(streams, w_gate, w_up, w_down, layer, final_g, final_norm):
    b, _, d = streams[0][0].shape
    caps = [CAPACITY_FACTOR * s[0].shape[1] // N_EXPERTS for s in streams]
    rows = sum(b * cap for cap in caps)
    routes, first_blocks, xs = [], [], None
    row0 = 0
    for k, ((x1, h2, aff, aff_t, g2), cap) in enumerate(zip(streams, caps)):
        t = x1.shape[1]
        assert row0 % cap == 0 and cap % SLOT_ALIGN == 0, "a stream's row blocks must tile the slot buffer"
        pos_t, starts = _route(aff_t, cap)
        windowed = k == 0 and t > TOKEN_TILE and cap >= SLOT_WINDOW and (cap - SLOT_WINDOW) % SLOT_ALIGN == 0
        if windowed:
            xs = _gather_windowed(pos_t, starts, h2, cap, rows)
        else:
            if xs is None and rows > b * cap:
                xs = jnp.zeros((N_EXPERTS, rows, d), BF16)
            xs = _gather_stacked(pos_t, h2, cap, rows, into=xs, first_block=row0 // cap)
        routes.append((pos_t, starts, aff, windowed))
        first_blocks.append(row0 // cap)
        row0 += b * cap
    y = _expert_ffn(xs, w_gate, w_up, w_down, layer)
    outs = []
    for (x1, h2, _, _, g2), cap, (pos_t, starts, aff, windowed), first in zip(streams, caps, routes, first_blocks):
        if windowed:
            outs.append(_combine_windowed(x1, g2, pos_t, starts, aff, y, final_g, cap, final_norm))
        else:
            outs.append(_combine(x1, g2, pos_t, aff, y, final_g, cap, final_norm, first))
    return outs


def _pre_gla_kernel(x_ref, g_ref, sh_ref, sc_ref, w_ref, wlr_ref, w2_ref, gb_ref,
                    q_ref, r_ref, k_ref, v_ref, lgf_ref, lgb_ref):
    qk = q_ref.shape[-1]
    vd = v_ref.shape[-1]
    tm = x_ref.shape[1]
    n_sub = max(1, tm // GLA_IN_SUBTILE_ROWS)
    subs = [slice(i * tm // n_sub, (i + 1) * tm // n_sub) for i in range(n_sub)]
    hs = [_norm_mod(x_ref[0, rs, :], g_ref[...], sh_ref[0], sc_ref[0]).astype(BF16) for rs in subs]
    ps = [_dot(h, w_ref[...]) for h in hs]
    lrs = [_dot(h, wlr_ref[...]).astype(BF16) for h in hs]
    zs = [_dot(lr, w2_ref[...]) + gb_ref[...] for lr in lrs]
    for rs, p, z in zip(subs, ps, zs):
        q_ref[0, rs, :] = (p[:, :qk] * (qk // GLA_HEADS) ** -0.5).astype(BF16)
        r_ref[0, rs, :] = _silu(p[:, qk:qk + vd]).astype(BF16)
        k_ref[0, rs, :] = p[:, qk + vd:2 * qk + vd].astype(BF16)
        v_ref[0, rs, :] = p[:, 2 * qk + vd:].astype(BF16)
        lg = (jnp.minimum(z, 0.0) - jnp.log1p(jnp.exp(-jnp.abs(z)))) * (1.0 / GLA_TAU)
        lgf_ref[0, rs, :] = lg[:, :qk]
        lgb_ref[0, rs, :] = lg[:, qk:]


def _pre_gla(x, g, shift, scale, w_main, w_lr, w2, gate_b):
    b, t, d = x.shape
    n_main = w_main.shape[1]
    qk = w2.shape[1] // 2
    vd = (n_main - 2 * qk) // 2
    tm = min(GLA_IN_ROW_TILE, t)
    row = lambda i, j: (i, j, 0)
    per_b = lambda i, j: (i, 0, 0)
    fixed = lambda i, j: (0, 0)
    out = lambda n, dt: (pl.BlockSpec((1, tm, n), row), jax.ShapeDtypeStruct((b, t, n), dt))
    outs = [out(qk, BF16), out(vd, BF16), out(qk, BF16), out(vd, BF16), out(qk, F32), out(qk, F32)]
    return pl.pallas_call(
        _pre_gla_kernel,
        grid=(b, t // tm),
        in_specs=[pl.BlockSpec((1, tm, d), row),
                  pl.BlockSpec((1, d), fixed),
                  pl.BlockSpec((1, 1, d), per_b),
                  pl.BlockSpec((1, 1, d), per_b),
                  pl.BlockSpec((d, n_main), fixed),
                  pl.BlockSpec((d, LANES), fixed),
                  pl.BlockSpec((LANES, 2 * qk), fixed),
                  pl.BlockSpec((1, 2 * qk), fixed)],
        out_specs=[o[0] for o in outs],
        out_shape=[o[1] for o in outs],
        compiler_params=_params("arbitrary", "arbitrary"),
        name="gla_in_proj",
    )(x, g, shift, scale, w_main, w_lr, w2, gate_b)


def _gla_masks(rows):
    ri = lax.broadcasted_iota(I32, (rows, rows), 0)
    ci = lax.broadcasted_iota(I32, (rows, rows), 1)
    same = (ri // GLA_CHUNK) == (ci // GLA_CHUNK)
    return same & (ri >= ci), same & (ri <= ci)


def _gla_tiles(jobs):
    ch = GLA_CHUNK
    n = len(jobs)
    rows, dk = jobs[0][1].shape
    n_chunks = rows // ch
    last = [[c * ch + (ch - 1 if job[6] else 0) for c in range(n_chunks)] for job in jobs]
    b = []
    for q, k, v, lg, st_ref, mask, forward in jobs:
        lg_hi, lg_lo = _split2(lg)
        m = mask.astype(BF16)
        b.append(_dot(m, lg_hi) + _dot(m, lg_lo))
    qt, kt, kd = [None] * n, [None] * n, [None] * n
    for i, (q, k, v, lg, st_ref, mask, forward) in enumerate(jobs):
        b_last = jnp.concatenate([jnp.broadcast_to(b[i][r:r + 1, :], (ch, dk)) for r in last[i]], axis=0)
        kf = k.astype(F32)
        kd[i] = (kf * jnp.exp(b_last - b[i])).astype(BF16)
        if q is not None:
            qt[i] = (q.astype(F32) * jnp.exp(b[i])).astype(BF16)
            kt[i] = (kf * jnp.exp(-b[i])).astype(BF16)
    a = [None if jobs[i][0] is None else lax.dot_general(qt[i], kt[i], NT_DIMS, preferred_element_type=F32)
         for i in range(n)]
    a = [None if a[i] is None else jnp.where(jobs[i][5], a[i], 0.0).astype(BF16) for i in range(n)]
    o_local = [None if a[i] is None else _dot(a[i], jobs[i][2]) for i in range(n)]
    d_st = [[lax.dot_general(jobs[i][2][c * ch:(c + 1) * ch], kd[i][c * ch:(c + 1) * ch], TN_DIMS,
                             preferred_element_type=F32) for c in range(n_chunks)] for i in range(n)]
    results = []
    for i, (q, k, v, lg, st_ref, mask, forward) in enumerate(jobs):
        st = st_ref[...]
        outs = [None] * n_chunks
        for c in (range(n_chunks) if forward else reversed(range(n_chunks))):
            rs = slice(c * ch, (c + 1) * ch)
            if q is not None:
                outs[c] = o_local[i][rs] + lax.dot_general(qt[i][rs], st.astype(BF16), NT_DIMS,
                                                           preferred_element_type=F32)
            st = jnp.exp(b[i][last[i][c]:last[i][c] + 1, :]) * st + d_st[i][c]
        st_ref[...] = st
        results.append(None if q is None else jnp.concatenate(outs, axis=0))
    return results


def _gla_kernel(q_ref, k_ref, v_ref, lgf_ref, lgb_ref, kc_ref, vc_ref, lgcf_ref, lgcb_ref, r_ref, ng_ref,
                o_ref, of_ref, ob_ref, stf_ref, stb_ref):
    t = q_ref.shape[1]
    t_ctx = kc_ref.shape[1]
    n_heads = stf_ref.shape[0]
    dv, dk = stf_ref.shape[1:]
    tile = min(GLA_TILE, t)
    ctile = min(GLA_TILE, t_ctx)
    mask_f, mask_b = _gla_masks(tile)
    cmask_f, cmask_b = (mask_f, mask_b) if ctile == tile else _gla_masks(ctile)
    stf_ref[...] = jnp.zeros_like(stf_ref)
    stb_ref[...] = jnp.zeros_like(stb_ref)
    kcols = [slice(h * dk, (h + 1) * dk) for h in range(n_heads)]
    vcols = [slice(h * dv, (h + 1) * dv) for h in range(n_heads)]

    n_ctx = t_ctx // ctile
    for i in range(n_ctx):
        rf = slice(i * ctile, (i + 1) * ctile)
        rb = slice((n_ctx - 1 - i) * ctile, (n_ctx - i) * ctile)
        jobs = []
        for h in range(n_heads):
            jobs.append((None, kc_ref[0, rf, kcols[h]], vc_ref[0, rf, vcols[h]], lgcf_ref[0, rf, kcols[h]],
                         stf_ref.at[h], cmask_f, True))
            jobs.append((None, kc_ref[0, rb, kcols[h]], vc_ref[0, rb, vcols[h]], lgcb_ref[0, rb, kcols[h]],
                         stb_ref.at[h], cmask_b, False))
        _gla_tiles(jobs)

    n = t // tile

    def body(i, carry):
        rf = pl.ds(pl.multiple_of(i * tile, tile), tile)
        rb = pl.ds(pl.multiple_of((n - 1 - i) * tile, tile), tile)
        jobs = []
        for h in range(n_heads):
            jobs.append((q_ref[0, rf, kcols[h]], k_ref[0, rf, kcols[h]], v_ref[0, rf, vcols[h]],
                         lgf_ref[0, rf, kcols[h]], stf_ref.at[h], mask_f, True))
            jobs.append((q_ref[0, rb, kcols[h]], k_ref[0, rb, kcols[h]], v_ref[0, rb, vcols[h]],
                         lgb_ref[0, rb, kcols[h]], stb_ref.at[h], mask_b, False))
        outs = _gla_tiles(jobs)
        for h in range(n_heads):
            of_ref[rf, vcols[h]] = outs[2 * h]
            ob_ref[rb, vcols[h]] = outs[2 * h + 1]
        return carry

    lax.fori_loop(0, n, body, 0)

    def readout(i, carry):
        rows = pl.ds(pl.multiple_of(i * tile, tile), tile)
        for h in range(n_heads):
            o = of_ref[rows, vcols[h]] + ob_ref[rows, vcols[h]]
            o = o * lax.rsqrt(jnp.mean(o * o, axis=-1, keepdims=True) + NORM_EPS) * ng_ref[:, vcols[h]]
            o_ref[0, rows, vcols[h]] = (o * r_ref[0, rows, vcols[h]].astype(F32)).astype(o_ref.dtype)
        return carry

    lax.fori_loop(0, n, readout, 0)


def _gla(q, k, v, lgf, lgb, kc, vc, lgcf, lgcb, r, norm_g):
    b, t, qk = q.shape
    vd = v.shape[-1]
    t_ctx = kc.shape[1]
    dk, dv = qk // GLA_HEADS, vd // GLA_HEADS
    hp = GLA_HEADS_PER_STEP
    head = lambda i, h: (i, 0, h)
    lat = lambda n: pl.BlockSpec((1, t, hp * n), head)
    ctx = lambda n: pl.BlockSpec((1, t_ctx, hp * n), head)
    return pl.pallas_call(
        _gla_kernel,
        grid=(b, GLA_HEADS // hp),
        in_specs=[lat(dk), lat(dk), lat(dv), lat(dk), lat(dk),
                  ctx(dk), ctx(dv), ctx(dk), ctx(dk),
                  lat(dv),
                  pl.BlockSpec((1, hp * dv), lambda i, h: (0, h))],
        out_specs=lat(dv),
        out_shape=jax.ShapeDtypeStruct((b, t, vd), BF16),
        scratch_shapes=[pltpu.VMEM((t, hp * dv), F32), pltpu.VMEM((t, hp * dv), F32),
                        pltpu.VMEM((hp, dv, dk), F32), pltpu.VMEM((hp, dv, dk), F32)],
        compiler_params=_params("arbitrary", "arbitrary"),
        name="gla_scan",
    )(q, k, v, lgf, lgb, kc, vc, lgcf, lgcb, r, norm_g)


def _rope_tables(t):
    pos = jnp.arange(t)
    row = (pos // GRID_W).astype(F32)
    col = (pos % GRID_W).astype(F32)
    inv = ROPE_BASE ** (-jnp.arange(ROPE_PAIRS, dtype=F32) / ROPE_PAIRS)
    ar, ac = row[:, None] * inv, col[:, None] * inv
    ang = jnp.concatenate([ar, ar, ac, ac], axis=-1)
    sign = jnp.tile(jnp.repeat(jnp.array([-1.0, 1.0], F32), ROPE_PAIRS), 2)
    reps = LANES // HEAD_DIM
    return jnp.tile(jnp.cos(ang), (1, reps)), jnp.tile(jnp.sin(ang) * sign, (1, reps))


def _router_split(w):
    d, n = w.shape
    w = jnp.pad(w, ((0, 0), (0, LANES - n)))
    hi = w.astype(BF16)
    return hi, (w - hi.astype(F32)).astype(BF16)


def kernel(x, c, ctx, c_ctx, ada_w, ada_b, norm1_g, norm2_g, attn_w_in, attn_w_out, attn_sink, gla_w_in, gla_gate_w2, gla_gate_b, gla_norm_g, gla_w_out, router_w, exp_w_gate, exp_w_up, exp_w_down, final_norm_g):
    b, t, d = x.shape
    depth = ada_w.shape[0]
    assert depth == 2, "layer 0 is windowed attention with context output, layer 1 is GLA and last"
    rows = -(-(b + 1) // 16) * 16
    src = jnp.concatenate([c, c_ctx[None, :], jnp.zeros((rows - b - 1, d), F32)], axis=0)
    cos, sin_signed = _rope_tables(t)
    final_g = final_norm_g.reshape(1, d)

    def chunks(i):
        mod = _modulation(src, ada_w, ada_b, i)
        lat = [mod[:b, j * d:(j + 1) * d].reshape(b, 1, d) for j in range(6)]
        cx = [jnp.broadcast_to(mod[b, j * d:(j + 1) * d].reshape(1, 1, d), (b, 1, d)) for j in range(6)]
        return lat, cx

    (sh1, sc1, g1, sh2, sc2, g2), (sh1c, sc1c, g1c, sh2c, sc2c, g2c) = chunks(0)
    n1, n2 = norm1_g[0].reshape(1, d), norm2_g[0].reshape(1, d)
    w_in = attn_w_in[0].astype(BF16)
    w_out = attn_w_out[0].astype(BF16)
    sink = attn_sink[0].reshape(1, -1)
    wr_hi, wr_lo = _router_split(router_w[0])
    q, k, v = _pre_attn(x, n1, sh1, sc1, w_in, cos, sin_signed, rope=True)
    t_ctx = ctx.shape[1]
    qc, kc, vc = _pre_attn(ctx, n1, sh1c, sc1c, w_in, cos[:t_ctx], sin_signed[:t_ctx], rope=False)
    o = _attn_band(q, k, v, kc, vc, sink)
    oc = _attn_ctx(qc, kc, vc, sink)
    lat = _post_mixer(o, w_out, x, g1, n2, sh2, sc2, wr_hi, wr_lo)
    cxt = _post_mixer(oc, w_out, ctx, g1c, n2, sh2c, sc2c, wr_hi, wr_lo)
    x, xc = _ec_moe([(*lat, g2), (*cxt, g2c)], exp_w_gate, exp_w_up, exp_w_down, 0, final_g, False)

    (sh1, sc1, g1, sh2, sc2, g2), (sh1c, sc1c, _, _, _, _) = chunks(1)
    n1, n2 = norm1_g[1].reshape(1, d), norm2_g[1].reshape(1, d)
    qk = gla_gate_w2.shape[-1]
    n_main = gla_w_in.shape[-1] - 2 * GLA_RANK
    w_main = gla_w_in[0][:, :n_main].astype(BF16)
    w_lr = jnp.pad(gla_w_in[0][:, n_main:], ((0, 0), (0, LANES - 2 * GLA_RANK))).astype(BF16)
    w2 = jnp.zeros((LANES, 2 * qk), F32)
    w2 = w2.at[:GLA_RANK, :qk].set(gla_gate_w2[0, 0]).at[GLA_RANK:2 * GLA_RANK, qk:].set(gla_gate_w2[0, 1])
    w2 = w2.astype(BF16)
    gate_b = gla_gate_b[0].reshape(1, 2 * qk)
    gq, gr, gk, gv, lgf, lgb = _pre_gla(x, n1, sh1, sc1, w_main, w_lr, w2, gate_b)
    _, _, gkc, gvc, lgcf, lgcb = _pre_gla(xc, n1, sh1c, sc1c, w_main, w_lr, w2, gate_b)
    og = _gla(gq, gk, gv, lgf, lgb, gkc, gvc, lgcf, lgcb, gr, gla_norm_g[0].reshape(1, -1))
    wr_hi, wr_lo = _router_split(router_w[1])
    lat = _post_mixer(og, gla_w_out[0].astype(BF16), x, g1, n2, sh2, sc2, wr_hi, wr_lo)
    return _ec_moe([(*lat, g2)], exp_w_gate, exp_w_up, exp_w_down, 1, final_g, True)[0]
```
